```python
import math
import jax, jax.numpy as jnp
from jax import lax
import numpy as np

D_MODEL = 2048
BATCH = 4
SEQ = 4096
DEPTH = 2

MIX_WIDTH = D_MODEL
DIFF_V_DIM = 128
DIFF_QK_DIM = 64
N_DIFF_HEADS = (MIX_WIDTH // 2) // DIFF_V_DIM
SB_HEAD_DIM = 128
N_SB_HEADS = (MIX_WIDTH // 2) // SB_HEAD_DIM
DIFF_QK_COLS = N_DIFF_HEADS * 2 * DIFF_QK_DIM
DIFF_V_COLS = N_DIFF_HEADS * DIFF_V_DIM
SB_COLS = N_SB_HEADS * SB_HEAD_DIM
IN_COLS = 2 * DIFF_QK_COLS + DIFF_V_COLS + 3 * SB_COLS
Q_BLOCK = 128
NUM_BUCKETS = 32
MAX_DISTANCE = 128
N_GROUPS = 4
EXPERTS_PER_GROUP = 8
N_EXPERTS = N_GROUPS * EXPERTS_PER_GROUP
TOP_K = 2
D_EXPERT = D_MODEL // 2
MOE_BLOCK = 128
LN_EPS = 1e-5
SUBLN_EPS = 1e-5
ADA_SCALE = 0.1
DEEPNORM_ALPHA = (2 * DEPTH) ** 0.25
DEEPNORM_BETA = (8 * DEPTH) ** -0.25

kernel_name = "hymba_diff_stickbreak_hmoe_deepnorm"


def layer_norm(h, g, b):
    hf = h.astype(jnp.float32)
    mu = jnp.mean(hf, axis=-1, keepdims=True)
    var = jnp.mean(jnp.square(hf - mu), axis=-1, keepdims=True)
    out = (hf - mu) * lax.rsqrt(var + LN_EPS) * g.astype(jnp.float32) + b.astype(jnp.float32)
    return out.astype(h.dtype)


def t5_bucket(rel):
    n = jnp.maximum(rel, 0)
    max_exact = NUM_BUCKETS // 2
    large = max_exact + (jnp.log(jnp.maximum(n, 1).astype(jnp.float32) / max_exact)
                         / math.log(MAX_DISTANCE / max_exact) * (NUM_BUCKETS - max_exact)).astype(jnp.int32)
    large = jnp.minimum(large, NUM_BUCKETS - 1)
    return jnp.where(n < max_exact, n, large)


def diff_attention(q, k, v, bias_table, lam, lam_init, subln_g):
    B, S, H, _, DQK = q.shape
    nqb = S // Q_BLOCK
    scale = DQK ** -0.5
    kpos = jnp.arange(S)
    qb = q.reshape(B, nqb, Q_BLOCK, H, 2, DQK).transpose(1, 0, 2, 3, 4, 5)

    def block(args):
        qi, bi = args
        qpos = bi * Q_BLOCK + jnp.arange(Q_BLOCK)
        logits = jnp.einsum('bqhmd,bkhmd->bmhqk', qi, k).astype(jnp.float32) * scale
        bias = bias_table[t5_bucket(qpos[:, None] - kpos[None, :])]
        logits = logits + jnp.transpose(bias, (3, 2, 0, 1)).astype(jnp.float32)[None]
        causal = kpos[None, :] <= qpos[:, None]
        p = jax.nn.softmax(jnp.where(causal, logits, -jnp.inf), axis=-1)
        a = p[:, 0] - lam * p[:, 1]
        return jnp.einsum('bhqk,bkhd->bqhd', a.astype(v.dtype), v)

    o = lax.map(block, (qb, jnp.arange(nqb)))
    o = o.transpose(1, 0, 2, 3, 4).reshape(B, S, H, -1)
    of = o.astype(jnp.float32)
    of = of * lax.rsqrt(jnp.mean(jnp.square(of), axis=-1, keepdims=True) + SUBLN_EPS)
    of = of * subln_g.astype(jnp.float32) * (1.0 - lam_init)
    return of.astype(v.dtype).reshape(B, S, -1)


def stick_breaking_attention(q, k, v):
    B, S, H, DH = q.shape
    nqb = S // Q_BLOCK
    scale = DH ** -0.5
    kpos = jnp.arange(S)
    qb = q.reshape(B, nqb, Q_BLOCK, H, DH).transpose(1, 0, 2, 3, 4)

    def block(args):
        qi, bi = args
        qpos = bi * Q_BLOCK + jnp.arange(Q_BLOCK)
        z = jnp.einsum('bqhd,bkhd->bhqk', qi, k).astype(jnp.float32) * scale
        strict = kpos[None, :] < qpos[:, None]
        log_keep = jnp.where(strict, jax.nn.log_sigmoid(-z), 0.0)
        suffix = lax.cumsum(log_keep, axis=3, reverse=True) - log_keep
        a = jnp.where(strict, jnp.exp(jax.nn.log_sigmoid(z) + suffix), 0.0)
        return jnp.einsum('bhqk,bkhd->bqhd', a.astype(v.dtype), v)

    o = lax.map(block, (qb, jnp.arange(nqb)))
    return o.transpose(1, 0, 2, 3, 4).reshape(B, S, H * DH)


def hierarchical_moe(u, w_group, b_group, w_router, b_router, w_gate, w_up, w_down):
    B, S, D = u.shape
    M = B * S
    xt = u.reshape(M, D)
    gprob = jax.nn.softmax((xt @ w_group + b_group).astype(jnp.float32), axis=-1)
    gp, gidx = lax.top_k(gprob, 1)
    elog = (xt @ w_router + b_router).astype(jnp.float32).reshape(M, N_GROUPS, EXPERTS_PER_GROUP)
    sel = jnp.take_along_axis(elog, gidx[:, :, None], axis=1)[:, 0]
    ev, eidx = lax.top_k(sel, TOP_K)
    ew = jax.nn.softmax(ev, axis=-1) * gp
    eid = gidx * EXPERTS_PER_GROUP + eidx

    n_assign = M * TOP_K
    e_flat = eid.reshape(-1).astype(jnp.int32)
    w_flat = ew.reshape(-1)
    tok_flat = jnp.repeat(jnp.arange(M, dtype=jnp.int32), TOP_K)
    order = jnp.argsort(e_flat)
    sorted_e = e_flat[order]
    counts = jnp.bincount(e_flat, length=N_EXPERTS)
    starts = jnp.cumsum(counts) - counts
    padded = ((counts + MOE_BLOCK - 1) // MOE_BLOCK) * MOE_BLOCK
    pends = jnp.cumsum(padded)
    pstarts = pends - padded
    dest = pstarts[sorted_e] + (jnp.arange(n_assign, dtype=jnp.int32) - starts[sorted_e])
    n_blocks = (n_assign + N_EXPERTS * (MOE_BLOCK - 1) + MOE_BLOCK - 1) // MOE_BLOCK
    P = n_blocks * MOE_BLOCK
    pad_tok = jnp.zeros((P,), jnp.int32).at[dest].set(tok_flat[order])
    pad_w = jnp.zeros((P,), jnp.float32).at[dest].set(w_flat[order])
    block_e = jnp.minimum(jnp.searchsorted(pends, jnp.arange(n_blocks, dtype=jnp.int32) * MOE_BLOCK,
                                           side='right'), N_EXPERTS - 1)
    xg = xt[pad_tok].reshape(n_blocks, MOE_BLOCK, D)

    def expert_block(args):
        xb, e = args
        h = jax.nn.silu(xb @ w_gate[e]) * (xb @ w_up[e])
        return h @ w_down[e]

    yb = lax.map(expert_block, (xg, block_e)).reshape(P, D)
    y = jnp.zeros((M, D), u.dtype).at[pad_tok].add(yb * pad_w[:, None].astype(u.dtype))
    return y.reshape(B, S, D)


def setup_inputs(seed: int = 0) -> dict:
    key = jax.random.key(seed)
    ks = jax.random.split(key, 18)
    D = D_MODEL
    nrm = jax.random.normal
    beta = DEEPNORM_BETA
    col_scale = jnp.concatenate([
        jnp.ones((2 * DIFF_QK_COLS,)), jnp.full((DIFF_V_COLS,), beta),
        jnp.ones((2 * SB_COLS,)), jnp.full((SB_COLS,), beta)])
    return {
        "x": nrm(ks[0], (BATCH, SEQ, D), jnp.float32),
        "c": nrm(ks[1], (BATCH, D), jnp.float32),
        "w_ada": nrm(ks[2], (DEPTH, D, 6 * D), jnp.float32) * (D ** -0.5) * ADA_SCALE,
        "b_ada": 0.02 * nrm(ks[3], (DEPTH, 6 * D), jnp.float32),
        "w_in": nrm(ks[4], (DEPTH, D, IN_COLS), jnp.float32) * (D ** -0.5) * col_scale,
        "diff_lambda": 0.1 * nrm(ks[5], (DEPTH, 4, DIFF_QK_DIM), jnp.float32),
        "diff_subln_g": 1.0 + 0.02 * nrm(ks[6], (DEPTH, DIFF_V_DIM), jnp.float32),
        "rel_bias": 0.2 * nrm(ks[7], (NUM_BUCKETS, N_DIFF_HEADS, 2), jnp.float32),
        "w_o": nrm(ks[8], (DEPTH, MIX_WIDTH, D), jnp.float32) * (MIX_WIDTH ** -0.5) * beta,
        "ln_g": 1.0 + 0.02 * nrm(ks[9], (DEPTH, 2, D), jnp.float32),
        "ln_b": 0.02 * nrm(ks[10], (DEPTH, 2, D), jnp.float32),
        "w_group": nrm(ks[11], (DEPTH, D, N_GROUPS), jnp.float32) * (D ** -0.5),
        "b_group": 0.01 * nrm(ks[12], (DEPTH, N_GROUPS), jnp.float32),
        "w_router": nrm(ks[13], (DEPTH, D, N_EXPERTS), jnp.float32) * (D ** -0.5),
        "b_router": 0.01 * nrm(ks[14], (DEPTH, N_EXPERTS), jnp.float32),
        "w_gate": nrm(ks[15], (DEPTH, N_EXPERTS, D, D_EXPERT), jnp.float32) * (D ** -0.5),
        "w_up": nrm(ks[16], (DEPTH, N_EXPERTS, D, D_EXPERT), jnp.float32) * (D ** -0.5),
        "w_down": nrm(ks[17], (DEPTH, N_EXPERTS, D_EXPERT, D), jnp.float32) * (D_EXPERT ** -0.5) * beta,
    }


def reference(x, c, w_ada, b_ada, w_in, diff_lambda, diff_subln_g, rel_bias, w_o, ln_g, ln_b,
              w_group, b_group, w_router, b_router, w_gate, w_up, w_down):
    B, S, D = x.shape
    splits = [DIFF_QK_COLS, 2 * DIFF_QK_COLS, 2 * DIFF_QK_COLS + DIFF_V_COLS,
              2 * DIFF_QK_COLS + DIFF_V_COLS + SB_COLS, 2 * DIFF_QK_COLS + DIFF_V_COLS + 2 * SB_COLS]
    for l in range(DEPTH):
        mod = jax.nn.silu(c) @ w_ada[l] + b_ada[l]
        sh1, sc1, g1, sh2, sc2, g2 = jnp.split(mod[:, None, :], 6, axis=-1)

        u = x * (1.0 + sc1) + sh1
        proj = u @ w_in[l]
        dq, dk, dv, sq, sk, sv = jnp.split(proj, splits, axis=-1)
        lam_init = 0.8 - 0.6 * math.exp(-0.3 * l)
        lam_p = diff_lambda[l].astype(jnp.float32)
        lam = jnp.exp(jnp.sum(lam_p[0] * lam_p[1])) - jnp.exp(jnp.sum(lam_p[2] * lam_p[3])) + lam_init
        a_out = diff_attention(
            dq.reshape(B, S, N_DIFF_HEADS, 2, DIFF_QK_DIM),
            dk.reshape(B, S, N_DIFF_HEADS, 2, DIFF_QK_DIM),
            dv.reshape(B, S, N_DIFF_HEADS, DIFF_V_DIM),
            rel_bias, lam, lam_init, diff_subln_g[l])
        b_out = stick_breaking_attention(
            sq.reshape(B, S, N_SB_HEADS, SB_HEAD_DIM),
            sk.reshape(B, S, N_SB_HEADS, SB_HEAD_DIM),
            sv.reshape(B, S, N_SB_HEADS, SB_HEAD_DIM))
        mix = jnp.concatenate([a_out, b_out], axis=-1) @ w_o[l]
        x = layer_norm(DEEPNORM_ALPHA * x + (1.0 + g1) * mix, ln_g[l, 0], ln_b[l, 0])

        u2 = x * (1.0 + sc2) + sh2
        y = hierarchical_moe(u2, w_group[l], b_group[l], w_router[l], b_router[l],
                             w_gate[l], w_up[l], w_down[l])
        x = layer_norm(DEEPNORM_ALPHA * x + (1.0 + g2) * y, ln_g[l, 1], ln_b[l, 1])
    return x
```

```python
import functools
import math

import jax
import jax.numpy as jnp
import numpy as np
from jax import lax
from jax.experimental import pallas as pl
from jax.experimental.pallas import tpu as pltpu

F32 = jnp.float32
BF16 = jnp.bfloat16

N_DIFF_HEADS = 8
N_SB_HEADS = 8
HEAD_DIM = 128
DIFF_QK_DIM = 64
NUM_BUCKETS = 32
MAX_DISTANCE = 128
N_GROUPS = 4
EXPERTS_PER_GROUP = 8
N_EXPERTS = N_GROUPS * EXPERTS_PER_GROUP
TOP_K = 2
LN_EPS = 1e-5
SUBLN_EPS = 1e-5

LANES = 128
MASK_VALUE = -1e30
VMEM_LIMIT = 56 * 1024 * 1024


def _cparams(sem):
    return pltpu.CompilerParams(dimension_semantics=sem, vmem_limit_bytes=VMEM_LIMIT)


def _dot(a, b):
    return jnp.dot(a, b, preferred_element_type=F32)


def _dot_nt(a, b):
    return lax.dot_general(a, b, (((1,), (1,)), ((), ())), preferred_element_type=F32)


def _split(a):
    hi = a.astype(BF16)
    lo = (a - hi.astype(F32)).astype(BF16)
    return hi, lo


def _dot3(a, b):
    a_hi, a_lo = _split(a)
    b_hi, b_lo = _split(b)
    return _dot(a_hi, b_hi) + (_dot(a_hi, b_lo) + _dot(a_lo, b_hi))


def _layer_norm(h, g, b):
    mu = jnp.mean(h, axis=-1, keepdims=True)
    d = h - mu
    var = jnp.mean(d * d, axis=-1, keepdims=True)
    return d * lax.rsqrt(var + LN_EPS) * g + b


def _ada_kernel(c_ref, w_ref, b_ref, o_ref):
    c = c_ref[...]
    s = c * jax.nn.sigmoid(c)
    o_ref[...] = _dot3(s, w_ref[...]) + b_ref[...]


def ada_modulation(c8, w_ada, b_ada, tn=1024):
    L, D, N = w_ada.shape
    return pl.pallas_call(
        _ada_kernel,
        grid=(L, N // tn),
        in_specs=[
            pl.BlockSpec((8, D), lambda l, j: (0, 0)),
            pl.BlockSpec((None, D, tn), lambda l, j: (l, 0, j)),
            pl.BlockSpec((None, 1, tn), lambda l, j: (l, 0, j)),
        ],
        out_specs=pl.BlockSpec((None, 8, tn), lambda l, j: (l, 0, j)),
        out_shape=jax.ShapeDtypeStruct((L, 8, N), F32),
        compiler_params=_cparams(("arbitrary", "arbitrary")),
        name="ada",
    )(c8, w_ada, b_ada.reshape(L, 1, N))


def _inproj_kernel(x_ref, sc_ref, sh_ref, w_ref, o_ref, u_ref):
    @pl.when(pl.program_id(1) == 0)
    def _():
        u_ref[...] = (x_ref[...] * (1.0 + sc_ref[...]) + sh_ref[...]).astype(BF16)

    o_ref[...] = _dot(u_ref[...], w_ref[...]).astype(o_ref.dtype)


def in_projection(x2d, sc, sh, w_bf16, seq, tm=1024, tn=1536):
    M, D = x2d.shape
    N = w_bf16.shape[1]
    tm = min(tm, seq)
    tn = min(tn, N)
    bpb = seq // tm
    return pl.pallas_call(
        _inproj_kernel,
        grid=(M // tm, N // tn),
        in_specs=[
            pl.BlockSpec((tm, D), lambda i, j: (i, 0)),
            pl.BlockSpec((None, 1, D), lambda i, j: (i // bpb, 0, 0)),
            pl.BlockSpec((None, 1, D), lambda i, j: (i // bpb, 0, 0)),
            pl.BlockSpec((D, tn), lambda i, j: (0, j)),
        ],
        out_specs=pl.BlockSpec((tm, tn), lambda i, j: (i, j)),
        out_shape=jax.ShapeDtypeStruct((M, N), BF16),
        scratch_shapes=[pltpu.VMEM((tm, D), BF16)],
        compiler_params=_cparams(("arbitrary", "arbitrary")),
        name="inproj",
    )(x2d, sc, sh, w_bf16)


def _t5_bucket_np(rel):
    n = np.maximum(rel, 0)
    max_exact = NUM_BUCKETS // 2
    ratio = np.maximum(n, 1).astype(np.float32) / np.float32(max_exact)
    large = max_exact + (np.log(ratio).astype(np.float32) / np.float32(math.log(MAX_DISTANCE / max_exact))
                         * np.float32(NUM_BUCKETS - max_exact)).astype(np.int32)
    large = np.minimum(large, NUM_BUCKETS - 1)
    return np.where(n < max_exact, n, large)


def rel_bias_tiles(rel_bias, t):
    assert t >= MAX_DISTANCE
    i = np.arange(t)[:, None]
    j = np.arange(t)[None, :]
    rel = np.stack([i - j, t + i - j])
    idx = _t5_bucket_np(rel)
    tiles = rel_bias[jnp.asarray(idx)]
    tiles = jnp.transpose(tiles, (3, 4, 0, 1, 2)).astype(F32)
    mask = jnp.asarray(rel < 0)[None, None]
    return jnp.where(mask, MASK_VALUE, tiles)


def _diff_kernel(scal_ref, q_ref, k_ref, v_ref, bias_ref, g_ref, o_ref, m_ref, l_ref, acc_ref, *, t, out_scale):
    h = pl.program_id(0)
    qi = pl.program_id(2)
    lam = scal_ref[0]
    far_bias = (scal_ref[1 + 2 * h], scal_ref[2 + 2 * h])

    q = q_ref[...] * BF16(DIFF_QK_DIM ** -0.5)
    lane = lax.broadcasted_iota(jnp.int32, q.shape, 1)
    zero = jnp.zeros_like(q)
    qm = (jnp.where(lane < DIFF_QK_DIM, q, zero), jnp.where(lane >= DIFF_QK_DIM, q, zero))

    m_ref[...] = jnp.full(m_ref.shape, MASK_VALUE, F32)
    l_ref[...] = jnp.zeros(l_ref.shape, F32)
    acc_ref[...] = jnp.zeros(acc_ref.shape, F32)

    def step(j, bias):
        kb = k_ref[pl.ds(pl.multiple_of(j * t, t), t), :]
        vb = v_ref[pl.ds(pl.multiple_of(j * t, t), t), :]
        for mp in range(2):
            s = _dot_nt(qm[mp], kb) + bias[mp]
            m_old = m_ref[mp]
            m_new = jnp.maximum(m_old, jnp.max(s, axis=1, keepdims=True))
            p = jnp.exp(s - m_new)
            alpha = jnp.exp(m_old - m_new)
            l_ref[mp] = alpha * l_ref[mp] + jnp.sum(p, axis=1, keepdims=True)
            acc_ref[mp] = alpha * acc_ref[mp] + _dot(p.astype(BF16), vb)
            m_ref[mp] = m_new

    def far_body(j, carry):
        step(j, far_bias)
        return carry

    lax.fori_loop(0, jnp.maximum(qi - 1, 0), far_body, 0)

    @pl.when(qi >= 1)
    def _():
        step(qi - 1, (bias_ref[0, 1], bias_ref[1, 1]))

    step(qi, (bias_ref[0, 0], bias_ref[1, 0]))

    o = acc_ref[0] / l_ref[0] - lam * (acc_ref[1] / l_ref[1])
    o = o * lax.rsqrt(jnp.mean(o * o, axis=-1, keepdims=True) + SUBLN_EPS)
    o_ref[...] = (o * g_ref[...] * out_scale).astype(o_ref.dtype)


def diff_attention(proj, scal, bias_tiles, subln_g, batch, seq, lam_init, t):
    M = proj.shape[0]
    H = N_DIFF_HEADS
    nq = seq // t
    kern = functools.partial(_diff_kernel, t=t, out_scale=1.0 - lam_init)
    grid_spec = pltpu.PrefetchScalarGridSpec(
        num_scalar_prefetch=1,
        grid=(H, batch, nq),
        in_specs=[
            pl.BlockSpec((t, HEAD_DIM), lambda h, b, qi, s: (b * nq + qi, h)),
            pl.BlockSpec((seq, HEAD_DIM), lambda h, b, qi, s: (b, H + h)),
            pl.BlockSpec((seq, HEAD_DIM), lambda h, b, qi, s: (b, 2 * H + h)),
            pl.BlockSpec((None, 2, 2, t, t), lambda h, b, qi, s: (h, 0, 0, 0, 0)),
            pl.BlockSpec((1, HEAD_DIM), lambda h, b, qi, s: (0, 0)),
        ],
        out_specs=pl.BlockSpec((t, HEAD_DIM), lambda h, b, qi, s: (b * nq + qi, h)),
        scratch_shapes=[
            pltpu.VMEM((2, t, 1), F32),
            pltpu.VMEM((2, t, 1), F32),
            pltpu.VMEM((2, t, HEAD_DIM), F32),
        ],
    )
    return pl.pallas_call(
        kern,
        grid_spec=grid_spec,
        out_shape=jax.ShapeDtypeStruct((M, H * HEAD_DIM), BF16),
        compiler_params=_cparams(("arbitrary", "arbitrary", "arbitrary")),
        name="diffattn",
    )(scal, proj, proj, proj, bias_tiles, subln_g.reshape(1, HEAD_DIM))


def _sb_kernel(q_ref, k_ref, v_ref, o_ref, carry_ref, acc_ref, *, t, scale):
    qi = pl.program_id(2)
    q = q_ref[...]
    row = lax.broadcasted_iota(jnp.int32, (t, t), 0)
    col = lax.broadcasted_iota(jnp.int32, (t, t), 1)
    upper = jnp.where(row > col, 1.0, 0.0).astype(BF16)
    strict = col < row

    carry_ref[...] = jnp.zeros(carry_ref.shape, F32)
    acc_ref[...] = jnp.zeros(acc_ref.shape, F32)

    def step(j, diagonal):
        kb = k_ref[pl.ds(pl.multiple_of(j * t, t), t), :]
        vb = v_ref[pl.ds(pl.multiple_of(j * t, t), t), :]
        z = _dot_nt(q, kb) * scale
        lp = jnp.log(1.0 + jnp.exp(-jnp.abs(z)))
        log_keep = jnp.minimum(-z, 0.0) - lp
        log_beta = jnp.minimum(z, 0.0) - lp
        if diagonal:
            log_keep = jnp.where(strict, log_keep, 0.0)
        hi, lo = _split(log_keep)
        suffix = _dot(hi, upper) + _dot(lo, upper)
        a = jnp.exp(log_beta + suffix + carry_ref[...])
        if diagonal:
            a = jnp.where(strict, a, 0.0)
        acc_ref[...] += _dot(a.astype(BF16), vb)
        carry_ref[...] += jnp.sum(log_keep, axis=1, keepdims=True)

    step(qi, True)

    def body(n, c):
        step(qi - 1 - n, False)
        return c

    lax.fori_loop(0, qi, body, 0)
    o_ref[...] = acc_ref[...].astype(o_ref.dtype)


def sb_attention(proj, batch, seq, t):
    M = proj.shape[0]
    H = N_SB_HEADS
    base = 3 * N_DIFF_HEADS
    nq = seq // t
    kern = functools.partial(_sb_kernel, t=t, scale=HEAD_DIM ** -0.5)
    return pl.pallas_call(
        kern,
        grid=(H, batch, nq),
        in_specs=[
            pl.BlockSpec((t, HEAD_DIM), lambda h, b, qi: (b * nq + qi, base + h)),
            pl.BlockSpec((seq, HEAD_DIM), lambda h, b, qi: (b, base + H + h)),
            pl.BlockSpec((seq, HEAD_DIM), lambda h, b, qi: (b, base + 2 * H + h)),
        ],
        out_specs=pl.BlockSpec((t, HEAD_DIM), lambda h, b, qi: (b * nq + qi, h)),
        out_shape=jax.ShapeDtypeStruct((M, H * HEAD_DIM), BF16),
        scratch_shapes=[pltpu.VMEM((t, 1), F32), pltpu.VMEM((t, HEAD_DIM), F32)],
        compiler_params=_cparams(("arbitrary", "arbitrary", "arbitrary")),
        name="sbattn",
    )(proj, proj, proj)


ROUTE_E1, ROUTE_E2, ROUTE_W1, ROUTE_W2, ROUTE_R1, ROUTE_R2 = range(6)


def _mix_kernel(a_ref, b_ref, wo_ref, x_ref, g1_ref, sc2_ref, sh2_ref, lng_ref, lnb_ref, wrt_ref, brt_ref,
                x1_ref, u2_ref, route_ref, cnt_ref, carry_ref, *, alpha, half):
    i = pl.program_id(0)

    @pl.when(i == 0)
    def _():
        carry_ref[...] = jnp.zeros(carry_ref.shape, F32)

    mix = _dot(a_ref[...], wo_ref[pl.ds(0, half), :]) + _dot(b_ref[...], wo_ref[pl.ds(half, half), :])
    h = alpha * x_ref[...] + (1.0 + g1_ref[...]) * mix
    x1 = _layer_norm(h, lng_ref[...], lnb_ref[...])
    x1_ref[...] = x1
    u2 = x1 * (1.0 + sc2_ref[...]) + sh2_ref[...]
    u2_ref[...] = u2

    logits = _dot3(u2, wrt_ref[...]) + brt_ref[...]
    tm = logits.shape[0]
    lane = lax.broadcasted_iota(jnp.int32, (tm, LANES), 1).astype(F32)
    gmask = lane < N_GROUPS
    gl = jnp.where(gmask, logits, MASK_VALUE)
    gmax = jnp.max(gl, axis=1, keepdims=True)
    gidx = jnp.min(jnp.where(gmask & (gl == gmax), lane, float(LANES)), axis=1, keepdims=True)
    gsum = jnp.sum(jnp.where(gmask, jnp.exp(gl - gmax), 0.0), axis=1, keepdims=True)
    gp = 1.0 / gsum

    lo = N_GROUPS + gidx * EXPERTS_PER_GROUP
    emask = (lane >= lo) & (lane < lo + EXPERTS_PER_GROUP)
    el = jnp.where(emask, logits, MASK_VALUE)
    v1 = jnp.max(el, axis=1, keepdims=True)
    i1 = jnp.min(jnp.where(emask & (el == v1), lane, float(LANES)), axis=1, keepdims=True)
    emask2 = emask & (lane != i1)
    el2 = jnp.where(emask2, logits, MASK_VALUE)
    v2 = jnp.max(el2, axis=1, keepdims=True)
    i2 = jnp.min(jnp.where(emask2 & (el2 == v2), lane, float(LANES)), axis=1, keepdims=True)
    tt = jnp.exp(v2 - v1)
    w1 = gp / (1.0 + tt)
    w2 = gp * tt / (1.0 + tt)
    e1 = i1 - N_GROUPS
    e2 = i2 - N_GROUPS

    oh1 = lane == e1
    oh2 = lane == e2
    oh1b = jnp.where(oh1, 1.0, 0.0).astype(BF16)
    oh2b = jnp.where(oh2, 1.0, 0.0).astype(BF16)
    row = lax.broadcasted_iota(jnp.int32, (tm, tm), 0)
    col = lax.broadcasted_iota(jnp.int32, (tm, tm), 1)
    lower = jnp.where(col < row, 1.0, 0.0).astype(BF16)
    c1 = jnp.sum(jnp.where(oh1, 1.0, 0.0), axis=0, keepdims=True)
    c2 = jnp.sum(jnp.where(oh2, 1.0, 0.0), axis=0, keepdims=True)
    carry = carry_ref[...]
    r1 = _dot(lower, oh1b) + carry
    r2 = _dot(lower, oh2b) + (carry + c1)
    rank1 = jnp.sum(jnp.where(oh1, r1, 0.0), axis=1, keepdims=True)
    rank2 = jnp.sum(jnp.where(oh2, r2, 0.0), axis=1, keepdims=True)
    carry = carry + c1 + c2
    carry_ref[...] = carry
    cnt_ref[...] = jnp.broadcast_to(carry, cnt_ref.shape)

    out = jnp.zeros((tm, LANES), F32)
    for ln, val in ((ROUTE_E1, e1), (ROUTE_E2, e2), (ROUTE_W1, w1), (ROUTE_W2, w2),
                    (ROUTE_R1, rank1), (ROUTE_R2, rank2)):
        out = jnp.where(lane == float(ln), val, out)
    route_ref[...] = out


def mix_ln_route(a_out, b_out, wo_bf16, x2d, g1, sc2, sh2, ln_g, ln_b, w_rt, b_rt, seq, alpha, tm=256):
    M, D = x2d.shape
    half = a_out.shape[1]
    bpb = seq // tm
    row = lambda i: (i // bpb, 0, 0)
    full = lambda i: (0, 0)
    kern = functools.partial(_mix_kernel, alpha=alpha, half=half)
    return pl.pallas_call(
        kern,
        grid=(M // tm,),
        in_specs=[
            pl.BlockSpec((tm, half), lambda i: (i, 0)),
            pl.BlockSpec((tm, half), lambda i: (i, 0)),
            pl.BlockSpec((2 * half, D), full),
            pl.BlockSpec((tm, D), lambda i: (i, 0)),
            pl.BlockSpec((None, 1, D), row),
            pl.BlockSpec((None, 1, D), row),
            pl.BlockSpec((None, 1, D), row),
            pl.BlockSpec((1, D), full),
            pl.BlockSpec((1, D), full),
            pl.BlockSpec((D, LANES), full),
            pl.BlockSpec((1, LANES), full),
        ],
        out_specs=[
            pl.BlockSpec((tm, D), lambda i: (i, 0)),
            pl.BlockSpec((tm, D), lambda i: (i, 0)),
            pl.BlockSpec((tm, LANES), lambda i: (i, 0)),
            pl.BlockSpec((8, LANES), full),
        ],
        out_shape=[
            jax.ShapeDtypeStruct((M, D), F32),
            jax.ShapeDtypeStruct((M, D), F32),
            jax.ShapeDtypeStruct((M, LANES), F32),
            jax.ShapeDtypeStruct((8, LANES), F32),
        ],
        scratch_shapes=[pltpu.VMEM((1, LANES), F32)],
        compiler_params=_cparams(("arbitrary",)),
        name="mix",
    )(a_out, b_out, wo_bf16, x2d, g1, sc2, sh2, ln_g.reshape(1, D), ln_b.reshape(1, D), w_rt, b_rt)


def _row_copy(src_hbm, dst_vmem, src_row, dst_row, sem):
    return pltpu.make_async_copy(src_hbm.at[pl.ds(src_row, 1), :], dst_vmem.at[pl.ds(dst_row, 1), :], sem)


def _gather_kernel(tok_ref, u_hbm, o_ref, buf_ref, sem, *, tb):
    base = pl.program_id(0) * tb

    def issue(r, c):
        _row_copy(u_hbm, buf_ref, tok_ref[base + r], r, sem).start()
        return c

    lax.fori_loop(0, tb, issue, 0)
    pltpu.make_async_copy(u_hbm.at[pl.ds(0, tb), :], buf_ref, sem).wait()
    o_ref[...] = buf_ref[...].astype(o_ref.dtype)


def gather_rows(pad_tok, u2, tb):
    P = pad_tok.shape[0]
    D = u2.shape[1]
    grid_spec = pltpu.PrefetchScalarGridSpec(
        num_scalar_prefetch=1,
        grid=(P // tb,),
        in_specs=[pl.BlockSpec(memory_space=pl.ANY)],
        out_specs=pl.BlockSpec((tb, D), lambda i, tok: (i, 0)),
        scratch_shapes=[pltpu.VMEM((tb, D), F32), pltpu.SemaphoreType.DMA(())],
    )
    return pl.pallas_call(
        functools.partial(_gather_kernel, tb=tb),
        grid_spec=grid_spec,
        out_shape=jax.ShapeDtypeStruct((P, D), BF16),
        compiler_params=_cparams(("arbitrary",)),
        name="gather",
    )(pad_tok, u2)


def _expert_kernel(be_ref, nused_ref, x_ref, wg_ref, wu_ref, wd_ref, o_ref):
    i = pl.program_id(0)

    @pl.when(i < nused_ref[0])
    def _():
        x = x_ref[...]
        g = _dot(x, wg_ref[...])
        u = _dot(x, wu_ref[...])
        h = (g * jax.nn.sigmoid(g) * u).astype(BF16)
        o_ref[...] = _dot(h, wd_ref[...])

    @pl.when(i >= nused_ref[0])
    def _():
        o_ref[...] = jnp.zeros(o_ref.shape, o_ref.dtype)


def expert_mlp(block_e, n_used, xg, wg, wu, wd, tb):
    P, D = xg.shape
    F = wg.shape[2]

    def xmap(i, be, nu):
        return (jnp.minimum(i, jnp.maximum(nu[0] - 1, 0)), 0)

    def wmap(i, be, nu):
        return (be[jnp.minimum(i, jnp.maximum(nu[0] - 1, 0))], 0, 0)

    grid_spec = pltpu.PrefetchScalarGridSpec(
        num_scalar_prefetch=2,
        grid=(P // tb,),
        in_specs=[
            pl.BlockSpec((tb, D), xmap),
            pl.BlockSpec((None, D, F), wmap),
            pl.BlockSpec((None, D, F), wmap),
            pl.BlockSpec((None, F, D), wmap),
        ],
        out_specs=pl.BlockSpec((tb, D), lambda i, be, nu: (i, 0)),
    )
    return pl.pallas_call(
        _expert_kernel,
        grid_spec=grid_spec,
        out_shape=jax.ShapeDtypeStruct((P, D), F32),
        compiler_params=_cparams(("arbitrary",)),
        name="expert",
    )(block_e, n_used, xg, wg, wu, wd)


def _final_kernel(dest_ref, x1_ref, route_ref, g2_ref, lng_ref, lnb_ref, yb_hbm, o_ref, buf_ref, sem, *, alpha, tm, m):
    base = pl.program_id(0) * tm

    def issue(r, c):
        _row_copy(yb_hbm, buf_ref.at[0], dest_ref[base + r], r, sem).start()
        _row_copy(yb_hbm, buf_ref.at[1], dest_ref[m + base + r], r, sem).start()
        return c

    lax.fori_loop(0, tm, issue, 0)
    for k in range(TOP_K):
        pltpu.make_async_copy(yb_hbm.at[pl.ds(0, tm), :], buf_ref.at[k], sem).wait()

    route = route_ref[...]
    w1 = route[:, ROUTE_W1:ROUTE_W1 + 1]
    w2 = route[:, ROUTE_W2:ROUTE_W2 + 1]
    y = w1 * buf_ref[0] + w2 * buf_ref[1]
    h = alpha * x1_ref[...] + (1.0 + g2_ref[...]) * y
    o_ref[...] = _layer_norm(h, lng_ref[...], lnb_ref[...])


def combine_ln(dest_flat, x1, route, g2, ln_g, ln_b, yb, seq, alpha, tm=256):
    M, D = x1.shape
    bpb = seq // tm
    grid_spec = pltpu.PrefetchScalarGridSpec(
        num_scalar_prefetch=1,
        grid=(M // tm,),
        in_specs=[
            pl.BlockSpec((tm, D), lambda i, d: (i, 0)),
            pl.BlockSpec((tm, LANES), lambda i, d: (i, 0)),
            pl.BlockSpec((None, 1, D), lambda i, d: (i // bpb, 0, 0)),
            pl.BlockSpec((1, D), lambda i, d: (0, 0)),
            pl.BlockSpec((1, D), lambda i, d: (0, 0)),
            pl.BlockSpec(memory_space=pl.ANY),
        ],
        out_specs=pl.BlockSpec((tm, D), lambda i, d: (i, 0)),
        scratch_shapes=[pltpu.VMEM((TOP_K, tm, D), F32), pltpu.SemaphoreType.DMA(())],
    )
    return pl.pallas_call(
        functools.partial(_final_kernel, alpha=alpha, tm=tm, m=M),
        grid_spec=grid_spec,
        out_shape=jax.ShapeDtypeStruct((M, D), F32),
        compiler_params=_cparams(("arbitrary",)),
        name="final",
    )(dest_flat, x1, route, g2, ln_g.reshape(1, D), ln_b.reshape(1, D), yb)


def moe_layout(route, cnt, tb):
    M = route.shape[0]
    n_assign = M * TOP_K
    counts = cnt[0, :N_EXPERTS].astype(jnp.int32)
    padded = ((counts + tb - 1) // tb) * tb
    pends = jnp.cumsum(padded)
    pstarts = pends - padded
    n_tiles = (n_assign + N_EXPERTS * (tb - 1) + tb - 1) // tb
    e = route[:, ROUTE_E1:ROUTE_E2 + 1].astype(jnp.int32)
    rank = route[:, ROUTE_R1:ROUTE_R2 + 1].astype(jnp.int32)
    dest = pstarts[e] + rank
    dest_flat = dest.T.reshape(-1)
    tok = jnp.tile(jnp.arange(M, dtype=jnp.int32), TOP_K)
    pad_tok = jnp.zeros((n_tiles * tb,), jnp.int32).at[dest_flat].set(tok)
    block_e = jnp.minimum(jnp.searchsorted(pends, jnp.arange(n_tiles, dtype=jnp.int32) * tb, side='right'),
                          N_EXPERTS - 1).astype(jnp.int32)
    n_used = (pends[-1] // tb).astype(jnp.int32).reshape(1)
    return dest_flat, pad_tok, block_e, n_used


def _forward(x, c, w_ada, b_ada, w_in, diff_lambda, diff_subln_g, rel_bias, w_o, ln_g, ln_b, w_group, b_group,
             w_router, b_router, w_gate, w_up, w_down, *, attn_tile, moe_tile, ada_tn):
    B, S, D = x.shape
    depth = w_ada.shape[0]
    M = B * S
    alpha = (2 * depth) ** 0.25

    c8 = jnp.zeros((8, D), F32).at[:B].set(c)
    mod = ada_modulation(c8, w_ada, b_ada, ada_tn)[:, :B]
    mod = mod.reshape(depth, B, 6, 1, D)
    bias_tiles = rel_bias_tiles(rel_bias, attn_tile)
    far_bias = rel_bias[NUM_BUCKETS - 1].reshape(-1)

    x2d = x.reshape(M, D)
    for l in range(depth):
        sh1, sc1, g1, sh2, sc2, g2 = (mod[l, :, n] for n in range(6))
        proj = in_projection(x2d, sc1, sh1, w_in[l].astype(BF16), S)

        lam_init = 0.8 - 0.6 * math.exp(-0.3 * l)
        lp = diff_lambda[l].astype(F32)
        lam = jnp.exp(jnp.sum(lp[0] * lp[1])) - jnp.exp(jnp.sum(lp[2] * lp[3])) + lam_init
        scal = jnp.concatenate([lam.reshape(1), far_bias]).astype(F32)
        a_out = diff_attention(proj, scal, bias_tiles, diff_subln_g[l], B, S, lam_init, attn_tile)
        b_out = sb_attention(proj, B, S, attn_tile)

        w_rt = jnp.zeros((D, LANES), F32).at[:, :N_GROUPS].set(w_group[l])
        w_rt = w_rt.at[:, N_GROUPS:N_GROUPS + N_EXPERTS].set(w_router[l])
        b_rt = jnp.zeros((1, LANES), F32).at[0, :N_GROUPS].set(b_group[l])
        b_rt = b_rt.at[0, N_GROUPS:N_GROUPS + N_EXPERTS].set(b_router[l])
        x1, u2, route, cnt = mix_ln_route(a_out, b_out, w_o[l].astype(BF16), x2d, g1, sc2, sh2,
                                          ln_g[l, 0], ln_b[l, 0], w_rt, b_rt, S, alpha)

        dest_flat, pad_tok, block_e, n_used = moe_layout(route, cnt, moe_tile)
        xg = gather_rows(pad_tok, u2, moe_tile)
        yb = expert_mlp(block_e, n_used, xg, w_gate[l].astype(BF16), w_up[l].astype(BF16),
                        w_down[l].astype(BF16), moe_tile)
        x2d = combine_ln(dest_flat, x1, route, g2, ln_g[l, 1], ln_b[l, 1], yb, S, alpha)
    return x2d.reshape(B, S, D)


def kernel(x, c, w_ada, b_ada, w_in, diff_lambda, diff_subln_g, rel_bias, w_o, ln_g, ln_b, w_group, b_group,
           w_router, b_router, w_gate, w_up, w_down):
    return _forward(x, c, w_ada, b_ada, w_in, diff_lambda, diff_subln_g, rel_bias, w_o, ln_g, ln_b, w_group,
                    b_group, w_router, b_router, w_gate, w_up, w_down, attn_tile=256, moe_tile=256, ada_tn=1024)
```

```python
import functools
import math

import jax
import jax.numpy as jnp
import numpy as np
from jax import lax
from jax.experimental import pallas as pl
from jax.experimental.pallas import tpu as pltpu

F32 = jnp.float32
BF16 = jnp.bfloat16

N_DIFF_HEADS = 8
N_SB_HEADS = 8
HEAD_DIM = 128
DIFF_QK_DIM = 64
NUM_BUCKETS = 32
MAX_DISTANCE = 128
N_GROUPS = 4
EXPERTS_PER_GROUP = 8
N_EXPERTS = N_GROUPS * EXPERTS_PER_GROUP
TOP_K = 2
LN_EPS = 1e-5
SUBLN_EPS = 1e-5

LANES = 128
MASK_VALUE = -1e30
SB_EXIT = -104.0
DIFF_ROW_CHUNK = 128
VMEM_LIMIT = 56 * 1024 * 1024


def _cparams(sem):
    return pltpu.CompilerParams(dimension_semantics=sem, vmem_limit_bytes=VMEM_LIMIT)


def _dot(a, b):
    return jnp.dot(a, b, preferred_element_type=F32)


def _dot_nt(a, b):
    return lax.dot_general(a, b, (((1,), (1,)), ((), ())), preferred_element_type=F32)


def _split(a):
    hi = a.astype(BF16)
    lo = (a - hi.astype(F32)).astype(BF16)
    return hi, lo


def _dot3(a, b):
    a_hi, a_lo = _split(a)
    b_hi, b_lo = _split(b)
    return _dot(a_hi, b_hi) + (_dot(a_hi, b_lo) + _dot(a_lo, b_hi))


def _layer_norm(h, g, b):
    mu = jnp.mean(h, axis=-1, keepdims=True)
    d = h - mu
    var = jnp.mean(d * d, axis=-1, keepdims=True)
    return d * lax.rsqrt(var + LN_EPS) * g + b


def _ada_kernel(c_ref, w_ref, b_ref, o_ref):
    c = c_ref[...]
    s = c * jax.nn.sigmoid(c)
    o_ref[...] = _dot3(s, w_ref[...]) + b_ref[...]


def ada_modulation(c8, w_ada, b_ada, tn=1024):
    L, D, N = w_ada.shape
    return pl.pallas_call(
        _ada_kernel,
        grid=(L, N // tn),
        in_specs=[
            pl.BlockSpec((8, D), lambda l, j: (0, 0)),
            pl.BlockSpec((None, D, tn), lambda l, j: (l, 0, j)),
            pl.BlockSpec((None, 1, tn), lambda l, j: (l, 0, j)),
        ],
        out_specs=pl.BlockSpec((None, 8, tn), lambda l, j: (l, 0, j)),
        out_shape=jax.ShapeDtypeStruct((L, 8, N), F32),
        compiler_params=_cparams(("arbitrary", "arbitrary")),
        name="ada",
    )(c8, w_ada, b_ada.reshape(L, 1, N))


def _inproj_kernel(x_ref, sc_ref, sh_ref, w_ref, o_ref, u_ref):
    @pl.when(pl.program_id(1) == 0)
    def _():
        u_ref[...] = (x_ref[...] * (1.0 + sc_ref[...]) + sh_ref[...]).astype(BF16)

    o_ref[...] = _dot(u_ref[...], w_ref[...]).astype(o_ref.dtype)


def in_projection(x2d, sc, sh, w_bf16, layer, seq, tm=1024, tn=1536):
    M, D = x2d.shape
    N = w_bf16.shape[2]
    tm = min(tm, seq)
    tn = min(tn, N)
    bpb = seq // tm
    return pl.pallas_call(
        _inproj_kernel,
        grid=(M // tm, N // tn),
        in_specs=[
            pl.BlockSpec((tm, D), lambda i, j: (i, 0)),
            pl.BlockSpec((None, 1, D), lambda i, j: (i // bpb, 0, 0)),
            pl.BlockSpec((None, 1, D), lambda i, j: (i // bpb, 0, 0)),
            pl.BlockSpec((None, D, tn), lambda i, j: (layer, 0, j)),
        ],
        out_specs=pl.BlockSpec((tm, tn), lambda i, j: (i, j)),
        out_shape=jax.ShapeDtypeStruct((M, N), BF16),
        scratch_shapes=[pltpu.VMEM((tm, D), BF16)],
        compiler_params=_cparams(("arbitrary", "arbitrary")),
        name="inproj",
    )(x2d, sc, sh, w_bf16)


def _t5_bucket_np(rel):
    n = np.maximum(rel, 0)
    max_exact = NUM_BUCKETS // 2
    ratio = np.maximum(n, 1).astype(np.float32) / np.float32(max_exact)
    large = max_exact + (np.log(ratio).astype(np.float32) / np.float32(math.log(MAX_DISTANCE / max_exact))
                         * np.float32(NUM_BUCKETS - max_exact)).astype(np.int32)
    large = np.minimum(large, NUM_BUCKETS - 1)
    return np.where(n < max_exact, n, large)


def _bucket_thresholds():
    buckets = _t5_bucket_np(np.arange(2 * MAX_DISTANCE))
    assert np.all(np.diff(buckets) >= 0) and buckets[MAX_DISTANCE] == NUM_BUCKETS - 1
    return [int(np.argmax(buckets >= k)) for k in range(NUM_BUCKETS)]


def _relbias_kernel(tab_ref, o_ref, *, t, n_heads, thresholds):
    h = pl.program_id(0)
    row = lax.broadcasted_iota(jnp.int32, (t, t), 0)
    col = lax.broadcasted_iota(jnp.int32, (t, t), 1)
    for blk in range(2):
        rel = row - col + blk * t
        for mp in range(2):
            entry = lambda k: tab_ref[(k * n_heads + h) * 2 + mp]
            far = entry(NUM_BUCKETS - 1)
            val = jnp.full((t, t), entry(0) - far, F32)
            for k in range(1, NUM_BUCKETS):
                val = jnp.where(rel >= thresholds[k], entry(k) - far, val)
            if blk == 0:
                val = jnp.where(rel < 0, MASK_VALUE, val)
            o_ref[blk, mp * t:(mp + 1) * t, :] = val


def rel_bias_tiles(rel_bias, t):
    assert t >= MAX_DISTANCE
    n_heads = rel_bias.shape[1]
    kern = functools.partial(_relbias_kernel, t=t, n_heads=n_heads, thresholds=_bucket_thresholds())
    grid_spec = pltpu.PrefetchScalarGridSpec(
        num_scalar_prefetch=1,
        grid=(n_heads,),
        in_specs=[],
        out_specs=pl.BlockSpec((None, 2, 2 * t, t), lambda h, tab: (h, 0, 0, 0)),
    )
    return pl.pallas_call(
        kern,
        grid_spec=grid_spec,
        out_shape=jax.ShapeDtypeStruct((n_heads, 2, 2 * t, t), F32),
        compiler_params=_cparams(("arbitrary",)),
        name="relbias",
    )(rel_bias.reshape(-1).astype(F32))


def _diff_kernel(lam_ref, q_ref, k_ref, v_ref, bias_ref, g_ref, o_ref, qs_ref, m_ref, acc_ref,
                 *, t, rc, out_scale):
    qi = pl.program_id(2)
    q = q_ref[...] * BF16(DIFF_QK_DIM ** -0.5)
    lane = lax.broadcasted_iota(jnp.int32, q.shape, 1)
    zero = jnp.zeros_like(q)
    qs_ref[0:t, :] = jnp.where(lane < DIFF_QK_DIM, q, zero)
    qs_ref[t:2 * t, :] = jnp.where(lane >= DIFF_QK_DIM, q, zero)
    m_ref[...] = jnp.full(m_ref.shape, MASK_VALUE, F32)
    acc_ref[...] = jnp.zeros(acc_ref.shape, F32)
    ones = jnp.ones((t, HEAD_DIM), BF16)

    def step(j, bias_blk):
        kb = k_ref[pl.ds(pl.multiple_of(j * t, t), t), :]
        vb = v_ref[pl.ds(pl.multiple_of(j * t, t), t), :]
        vb1 = jnp.concatenate([vb, ones], axis=1)
        for r in range(2 * t // rc):
            rows = pl.ds(r * rc, rc)
            s = _dot_nt(qs_ref[rows, :], kb)
            if bias_blk is not None:
                s = s + bias_ref[bias_blk, rows, :]
            m_old = m_ref[rows, :]
            m_new = jnp.maximum(m_old, jnp.max(s, axis=1, keepdims=True))
            p = jnp.exp(s - jnp.concatenate([m_new] * (t // LANES), axis=1)).astype(BF16)
            alpha = jnp.exp(m_old - m_new)
            acc_ref[rows, :] = jnp.concatenate([alpha, alpha], axis=1) * acc_ref[rows, :] + _dot(p, vb1)
            m_ref[rows, :] = m_new

    def far_body(j, carry):
        step(j, None)
        return carry

    lax.fori_loop(0, jnp.maximum(qi - 1, 0), far_body, 0)

    @pl.when(qi >= 1)
    def _():
        step(qi - 1, 1)

    step(qi, 0)

    acc = acc_ref[...]
    r = acc[:, :HEAD_DIM] / acc[:, HEAD_DIM:]
    o = r[:t] - lam_ref[0] * r[t:]
    o = o * lax.rsqrt(jnp.mean(o * o, axis=-1, keepdims=True) + SUBLN_EPS)
    o_ref[...] = (o * g_ref[...] * out_scale).astype(o_ref.dtype)


def diff_attention(proj, lam, bias_tiles, subln_g, batch, seq, lam_init, t):
    M = proj.shape[0]
    H = N_DIFF_HEADS
    nq = seq // t
    kern = functools.partial(_diff_kernel, t=t, rc=min(DIFF_ROW_CHUNK, t), out_scale=1.0 - lam_init)
    grid_spec = pltpu.PrefetchScalarGridSpec(
        num_scalar_prefetch=1,
        grid=(H, batch, nq),
        in_specs=[
            pl.BlockSpec((t, HEAD_DIM), lambda h, b, qi, s: (b * nq + qi, h)),
            pl.BlockSpec((seq, HEAD_DIM), lambda h, b, qi, s: (b, H + h)),
            pl.BlockSpec((seq, HEAD_DIM), lambda h, b, qi, s: (b, 2 * H + h)),
            pl.BlockSpec((None, 2, 2 * t, t), lambda h, b, qi, s: (h, 0, 0, 0)),
            pl.BlockSpec((1, HEAD_DIM), lambda h, b, qi, s: (0, 0)),
        ],
        out_specs=pl.BlockSpec((t, HEAD_DIM), lambda h, b, qi, s: (b * nq + qi, h)),
        scratch_shapes=[
            pltpu.VMEM((2 * t, HEAD_DIM), BF16),
            pltpu.VMEM((2 * t, LANES), F32),
            pltpu.VMEM((2 * t, 2 * HEAD_DIM), F32),
        ],
    )
    return pl.pallas_call(
        kern,
        grid_spec=grid_spec,
        out_shape=jax.ShapeDtypeStruct((M, H * HEAD_DIM), BF16),
        compiler_params=_cparams(("arbitrary", "arbitrary", "arbitrary")),
        name="diffattn",
    )(lam.reshape(1).astype(F32), proj, proj, proj, bias_tiles, subln_g.reshape(1, HEAD_DIM))


def _sb_kernel(q_ref, k_ref, v_ref, o_ref, carry_ref, acc_ref, *, t, scale):
    qi = pl.program_id(2)
    q = q_ref[...]
    row = lax.broadcasted_iota(jnp.int32, (t, t), 0)
    col = lax.broadcasted_iota(jnp.int32, (t, t), 1)
    upper = jnp.where(row > col, 1.0, 0.0).astype(BF16)
    strict = col < row

    carry_ref[...] = jnp.zeros(carry_ref.shape, F32)
    acc_ref[...] = jnp.zeros(acc_ref.shape, F32)

    def step(j, diagonal):
        kb = k_ref[pl.ds(pl.multiple_of(j * t, t), t), :]
        vb = v_ref[pl.ds(pl.multiple_of(j * t, t), t), :]
        z = _dot_nt(q, kb) * scale
        lp = jnp.log(1.0 + jnp.exp(-jnp.abs(z)))
        log_keep = jnp.minimum(-z, 0.0) - lp
        log_beta = jnp.minimum(z, 0.0) - lp
        if diagonal:
            log_keep = jnp.where(strict, log_keep, 0.0)
        hi, lo = _split(log_keep)
        suffix = _dot(hi, upper) + _dot(lo, upper)
        a = jnp.exp(log_beta + suffix + carry_ref[...])
        if diagonal:
            a = jnp.where(strict, a, 0.0)
        acc_ref[...] += _dot(a.astype(BF16), vb)
        carry_ref[...] += jnp.sum(log_keep, axis=1, keepdims=True)

    step(qi, True)

    def cond(state):
        n, live = state
        return jnp.logical_and(n < qi, live > SB_EXIT)

    def body(state):
        n, _ = state
        step(qi - 1 - n, False)
        return n + 1, jnp.max(carry_ref[...])

    lax.while_loop(cond, body, (jnp.int32(0), jnp.max(carry_ref[...])))
    o_ref[...] = acc_ref[...].astype(o_ref.dtype)


def sb_attention(proj, batch, seq, t):
    M = proj.shape[0]
    H = N_SB_HEADS
    base = 3 * N_DIFF_HEADS
    nq = seq // t
    kern = functools.partial(_sb_kernel, t=t, scale=HEAD_DIM ** -0.5)
    return pl.pallas_call(
        kern,
        grid=(H, batch, nq),
        in_specs=[
            pl.BlockSpec((t, HEAD_DIM), lambda h, b, qi: (b * nq + qi, base + h)),
            pl.BlockSpec((seq, HEAD_DIM), lambda h, b, qi: (b, base + H + h)),
            pl.BlockSpec((seq, HEAD_DIM), lambda h, b, qi: (b, base + 2 * H + h)),
        ],
        out_specs=pl.BlockSpec((t, HEAD_DIM), lambda h, b, qi: (b * nq + qi, h)),
        out_shape=jax.ShapeDtypeStruct((M, H * HEAD_DIM), BF16),
        scratch_shapes=[pltpu.VMEM((t, 1), F32), pltpu.VMEM((t, HEAD_DIM), F32)],
        compiler_params=_cparams(("arbitrary", "arbitrary", "arbitrary")),
        name="sbattn",
    )(proj, proj, proj)


ROUTE_E1, ROUTE_E2, ROUTE_W1, ROUTE_W2, ROUTE_R1, ROUTE_R2 = range(6)


def _mix_kernel(a_ref, b_ref, wo_ref, x_ref, g1_ref, sc2_ref, sh2_ref, lng_ref, lnb_ref, wrt_ref, brt_ref,
                x1_ref, u2_ref, route_ref, cnt_ref, carry_ref, *, alpha, half):
    i = pl.program_id(0)

    @pl.when(i == 0)
    def _():
        carry_ref[...] = jnp.zeros(carry_ref.shape, F32)

    mix = _dot(a_ref[...], wo_ref[pl.ds(0, half), :]) + _dot(b_ref[...], wo_ref[pl.ds(half, half), :])
    h = alpha * x_ref[...] + (1.0 + g1_ref[...]) * mix
    x1 = _layer_norm(h, lng_ref[...], lnb_ref[...])
    x1_ref[...] = x1
    u2 = x1 * (1.0 + sc2_ref[...]) + sh2_ref[...]
    u2_ref[...] = u2

    logits = _dot3(u2, wrt_ref[...]) + brt_ref[...]
    tm = logits.shape[0]
    lane = lax.broadcasted_iota(jnp.int32, (tm, LANES), 1).astype(F32)
    gmask = lane < N_GROUPS
    gl = jnp.where(gmask, logits, MASK_VALUE)
    gmax = jnp.max(gl, axis=1, keepdims=True)
    gidx = jnp.min(jnp.where(gmask & (gl == gmax), lane, float(LANES)), axis=1, keepdims=True)
    gsum = jnp.sum(jnp.where(gmask, jnp.exp(gl - gmax), 0.0), axis=1, keepdims=True)
    gp = 1.0 / gsum

    lo = N_GROUPS + gidx * EXPERTS_PER_GROUP
    emask = (lane >= lo) & (lane < lo + EXPERTS_PER_GROUP)
    el = jnp.where(emask, logits, MASK_VALUE)
    v1 = jnp.max(el, axis=1, keepdims=True)
    i1 = jnp.min(jnp.where(emask & (el == v1), lane, float(LANES)), axis=1, keepdims=True)
    emask2 = emask & (lane != i1)
    el2 = jnp.where(emask2, logits, MASK_VALUE)
    v2 = jnp.max(el2, axis=1, keepdims=True)
    i2 = jnp.min(jnp.where(emask2 & (el2 == v2), lane, float(LANES)), axis=1, keepdims=True)
    tt = jnp.exp(v2 - v1)
    w1 = gp / (1.0 + tt)
    w2 = gp * tt / (1.0 + tt)
    e1 = i1 - N_GROUPS
    e2 = i2 - N_GROUPS

    oh1 = lane == e1
    oh2 = lane == e2
    oh1b = jnp.where(oh1, 1.0, 0.0).astype(BF16)
    oh2b = jnp.where(oh2, 1.0, 0.0).astype(BF16)
    row = lax.broadcasted_iota(jnp.int32, (tm, tm), 0)
    col = lax.broadcasted_iota(jnp.int32, (tm, tm), 1)
    lower = jnp.where(col < row, 1.0, 0.0).astype(BF16)
    c1 = jnp.sum(jnp.where(oh1, 1.0, 0.0), axis=0, keepdims=True)
    c2 = jnp.sum(jnp.where(oh2, 1.0, 0.0), axis=0, keepdims=True)
    carry = carry_ref[...]
    r1 = _dot(lower, oh1b) + carry
    r2 = _dot(lower, oh2b) + (carry + c1)
    rank1 = jnp.sum(jnp.where(oh1, r1, 0.0), axis=1, keepdims=True)
    rank2 = jnp.sum(jnp.where(oh2, r2, 0.0), axis=1, keepdims=True)
    carry = carry + c1 + c2
    carry_ref[...] = carry
    cnt_ref[...] = jnp.broadcast_to(carry, cnt_ref.shape)

    out = jnp.zeros((tm, LANES), F32)
    for ln, val in ((ROUTE_E1, e1), (ROUTE_E2, e2), (ROUTE_W1, w1), (ROUTE_W2, w2),
                    (ROUTE_R1, rank1), (ROUTE_R2, rank2)):
        out = jnp.where(lane == float(ln), val, out)
    route_ref[...] = out


def mix_ln_route(a_out, b_out, wo_bf16, layer, x2d, g1, sc2, sh2, ln_g, ln_b, w_rt, b_rt, seq, alpha, tm=256):
    M, D = x2d.shape
    half = a_out.shape[1]
    bpb = seq // tm
    row = lambda i: (i // bpb, 0, 0)
    full = lambda i: (0, 0)
    kern = functools.partial(_mix_kernel, alpha=alpha, half=half)
    return pl.pallas_call(
        kern,
        grid=(M // tm,),
        in_specs=[
            pl.BlockSpec((tm, half), lambda i: (i, 0)),
            pl.BlockSpec((tm, half), lambda i: (i, 0)),
            pl.BlockSpec((None, 2 * half, D), lambda i: (layer, 0, 0)),
            pl.BlockSpec((tm, D), lambda i: (i, 0)),
            pl.BlockSpec((None, 1, D), row),
            pl.BlockSpec((None, 1, D), row),
            pl.BlockSpec((None, 1, D), row),
            pl.BlockSpec((1, D), full),
            pl.BlockSpec((1, D), full),
            pl.BlockSpec((D, LANES), full),
            pl.BlockSpec((1, LANES), full),
        ],
        out_specs=[
            pl.BlockSpec((tm, D), lambda i: (i, 0)),
            pl.BlockSpec((tm, D), lambda i: (i, 0)),
            pl.BlockSpec((tm, LANES), lambda i: (i, 0)),
            pl.BlockSpec((8, LANES), full),
        ],
        out_shape=[
            jax.ShapeDtypeStruct((M, D), F32),
            jax.ShapeDtypeStruct((M, D), F32),
            jax.ShapeDtypeStruct((M, LANES), F32),
            jax.ShapeDtypeStruct((8, LANES), F32),
        ],
        scratch_shapes=[pltpu.VMEM((1, LANES), F32)],
        compiler_params=_cparams(("arbitrary",)),
        name="mix",
    )(a_out, b_out, wo_bf16, x2d, g1, sc2, sh2, ln_g.reshape(1, D), ln_b.reshape(1, D), w_rt, b_rt)


def _row_copy(src_hbm, dst_vmem, src_row, dst_row, sem):
    return pltpu.make_async_copy(src_hbm.at[pl.ds(src_row, 1), :], dst_vmem.at[pl.ds(dst_row, 1), :], sem)


def _start_rows(idx_ref, base, n_rows, src_hbm, dst_vmem, sem):
    def issue(r, c):
        _row_copy(src_hbm, dst_vmem, idx_ref[base + r], r, sem).start()
        return c

    lax.fori_loop(0, n_rows, issue, 0)


def _wait_rows(n_rows, src_hbm, dst_vmem, sem):
    pltpu.make_async_copy(src_hbm.at[pl.ds(0, n_rows), :], dst_vmem, sem).wait()


def _expert_kernel(tok_ref, be_ref, nused_ref, u_hbm, wg_ref, wu_ref, wd_ref, o_ref, xbuf_ref, sems, *, tb):
    i = pl.program_id(0)
    n_used = nused_ref[0]
    slot = i % 2

    def start(tile, s):
        _start_rows(tok_ref, tile * tb, tb, u_hbm, xbuf_ref.at[s], sems.at[s])

    @pl.when(i == 0)
    def _():
        start(0, 0)

    @pl.when(i + 1 < n_used)
    def _():
        start(i + 1, 1 - slot)

    @pl.when(i < n_used)
    def _():
        _wait_rows(tb, u_hbm, xbuf_ref.at[slot], sems.at[slot])
        x = xbuf_ref[slot].astype(BF16)
        g = _dot(x, wg_ref[...])
        u = _dot(x, wu_ref[...])
        h = (g * jax.nn.sigmoid(g) * u).astype(BF16)
        o_ref[...] = _dot(h, wd_ref[...])

    @pl.when(i >= n_used)
    def _():
        o_ref[...] = jnp.zeros(o_ref.shape, o_ref.dtype)


def expert_mlp(pad_tok, block_e, n_used, u2, wg, wu, wd, layer, tb):
    P = pad_tok.shape[0]
    D = u2.shape[1]
    F = wg.shape[3]

    def wmap(i, tok, be, nu):
        return (layer, be[jnp.minimum(i, jnp.maximum(nu[0] - 1, 0))], 0, 0)

    grid_spec = pltpu.PrefetchScalarGridSpec(
        num_scalar_prefetch=3,
        grid=(P // tb,),
        in_specs=[
            pl.BlockSpec(memory_space=pl.ANY),
            pl.BlockSpec((None, None, D, F), wmap),
            pl.BlockSpec((None, None, D, F), wmap),
            pl.BlockSpec((None, None, F, D), wmap),
        ],
        out_specs=pl.BlockSpec((tb, D), lambda i, tok, be, nu: (i, 0)),
        scratch_shapes=[pltpu.VMEM((2, tb, D), F32), pltpu.SemaphoreType.DMA((2,))],
    )
    return pl.pallas_call(
        functools.partial(_expert_kernel, tb=tb),
        grid_spec=grid_spec,
        out_shape=jax.ShapeDtypeStruct((P, D), F32),
        compiler_params=_cparams(("arbitrary",)),
        name="expert",
    )(pad_tok, block_e, n_used, u2, wg, wu, wd)


def _final_kernel(dest_ref, x1_ref, route_ref, g2_ref, lng_ref, lnb_ref, yb_hbm, o_ref, buf_ref, sems,
                  *, alpha, tm, m):
    i = pl.program_id(0)
    slot = i % 2

    def start(tile, s):
        for k in range(TOP_K):
            _start_rows(dest_ref, k * m + tile * tm, tm, yb_hbm, buf_ref.at[s, k], sems.at[s])

    @pl.when(i == 0)
    def _():
        start(0, 0)

    @pl.when(i + 1 < pl.num_programs(0))
    def _():
        start(i + 1, 1 - slot)

    for k in range(TOP_K):
        _wait_rows(tm, yb_hbm, buf_ref.at[slot, k], sems.at[slot])

    route = route_ref[...]
    w1 = route[:, ROUTE_W1:ROUTE_W1 + 1]
    w2 = route[:, ROUTE_W2:ROUTE_W2 + 1]
    y = w1 * buf_ref[slot, 0] + w2 * buf_ref[slot, 1]
    h = alpha * x1_ref[...] + (1.0 + g2_ref[...]) * y
    o_ref[...] = _layer_norm(h, lng_ref[...], lnb_ref[...])


def combine_ln(dest_flat, x1, route, g2, ln_g, ln_b, yb, seq, alpha, tm=256):
    M, D = x1.shape
    bpb = seq // tm
    grid_spec = pltpu.PrefetchScalarGridSpec(
        num_scalar_prefetch=1,
        grid=(M // tm,),
        in_specs=[
            pl.BlockSpec((tm, D), lambda i, d: (i, 0)),
            pl.BlockSpec((tm, LANES), lambda i, d: (i, 0)),
            pl.BlockSpec((None, 1, D), lambda i, d: (i // bpb, 0, 0)),
            pl.BlockSpec((1, D), lambda i, d: (0, 0)),
            pl.BlockSpec((1, D), lambda i, d: (0, 0)),
            pl.BlockSpec(memory_space=pl.ANY),
        ],
        out_specs=pl.BlockSpec((tm, D), lambda i, d: (i, 0)),
        scratch_shapes=[pltpu.VMEM((2, TOP_K, tm, D), F32), pltpu.SemaphoreType.DMA((2,))],
    )
    return pl.pallas_call(
        functools.partial(_final_kernel, alpha=alpha, tm=tm, m=M),
        grid_spec=grid_spec,
        out_shape=jax.ShapeDtypeStruct((M, D), F32),
        compiler_params=_cparams(("arbitrary",)),
        name="final",
    )(dest_flat, x1, route, g2, ln_g.reshape(1, D), ln_b.reshape(1, D), yb)


def moe_layout(route, cnt, tb):
    M = route.shape[0]
    n_assign = M * TOP_K
    counts = cnt[0, :N_EXPERTS].astype(jnp.int32)
    padded = ((counts + tb - 1) // tb) * tb
    pends = jnp.cumsum(padded)
    pstarts = pends - padded
    n_tiles = (n_assign + N_EXPERTS * (tb - 1) + tb - 1) // tb
    e = route[:, ROUTE_E1:ROUTE_E2 + 1].astype(jnp.int32)
    rank = route[:, ROUTE_R1:ROUTE_R2 + 1].astype(jnp.int32)
    dest = pstarts[e] + rank
    dest_flat = dest.T.reshape(-1)
    tok = jnp.tile(jnp.arange(M, dtype=jnp.int32), TOP_K)
    pad_tok = jnp.zeros((n_tiles * tb,), jnp.int32).at[dest_flat].set(tok)
    tile_start = jnp.arange(n_tiles, dtype=jnp.int32) * tb
    block_e = jnp.sum((pends[None, :] <= tile_start[:, None]).astype(jnp.int32), axis=1)
    block_e = jnp.minimum(block_e, N_EXPERTS - 1)
    n_used = (pends[-1] // tb).astype(jnp.int32).reshape(1)
    return dest_flat, pad_tok, block_e, n_used


def _forward(x, c, w_ada, b_ada, w_in, diff_lambda, diff_subln_g, rel_bias, w_o, ln_g, ln_b, w_group, b_group,
             w_router, b_router, w_gate, w_up, w_down, *, diff_tile, sb_tile, moe_tile, ada_tn):
    B, S, D = x.shape
    depth = w_ada.shape[0]
    M = B * S
    alpha = (2 * depth) ** 0.25

    c8 = jnp.zeros((8, D), F32).at[:B].set(c)
    mod = ada_modulation(c8, w_ada, b_ada, ada_tn)[:, :B]
    mod = mod.reshape(depth, B, 6, 1, D)
    bias_tiles = rel_bias_tiles(rel_bias, diff_tile)
    w_in_b, w_o_b = w_in.astype(BF16), w_o.astype(BF16)
    w_gate_b, w_up_b, w_down_b = w_gate.astype(BF16), w_up.astype(BF16), w_down.astype(BF16)

    x2d = x.reshape(M, D)
    for l in range(depth):
        sh1, sc1, g1, sh2, sc2, g2 = (mod[l, :, n] for n in range(6))
        proj = in_projection(x2d, sc1, sh1, w_in_b, l, S)

        lam_init = 0.8 - 0.6 * math.exp(-0.3 * l)
        lp = diff_lambda[l].astype(F32)
        lam = jnp.exp(jnp.sum(lp[0] * lp[1])) - jnp.exp(jnp.sum(lp[2] * lp[3])) + lam_init
        a_out = diff_attention(proj, lam, bias_tiles, diff_subln_g[l], B, S, lam_init, diff_tile)
        b_out = sb_attention(proj, B, S, sb_tile)

        w_rt = jnp.zeros((D, LANES), F32).at[:, :N_GROUPS].set(w_group[l])
        w_rt = w_rt.at[:, N_GROUPS:N_GROUPS + N_EXPERTS].set(w_router[l])
        b_rt = jnp.zeros((1, LANES), F32).at[0, :N_GROUPS].set(b_group[l])
        b_rt = b_rt.at[0, N_GROUPS:N_GROUPS + N_EXPERTS].set(b_router[l])
        x1, u2, route, cnt = mix_ln_route(a_out, b_out, w_o_b, l, x2d, g1, sc2, sh2,
                                          ln_g[l, 0], ln_b[l, 0], w_rt, b_rt, S, alpha)

        dest_flat, pad_tok, block_e, n_used = moe_layout(route, cnt, moe_tile)
        yb = expert_mlp(pad_tok, block_e, n_used, u2, w_gate_b, w_up_b, w_down_b, l, moe_tile)
        x2d = combine_ln(dest_flat, x1, route, g2, ln_g[l, 1], ln_b[l, 1], yb, S, alpha)
    return x2d.reshape(B, S, D)


def kernel(x, c, w_ada, b_ada, w_in, diff_lambda, diff_subln_g, rel_bias, w_o, ln_g, ln_b, w_group, b_group,
           w_router, b_router, w_gate, w_up, w_down):
    return _forward(x, c, w_ada, b_ada, w_in, diff_lambda, diff_subln_g, rel_bias, w_o, ln_g, ln_b, w_group,
                    b_group, w_router, b_router, w_gate, w_up, w_down,
                    diff_tile=512, sb_tile=256, moe_tile=256, ada_tn=1024)
```

```python
import functools
import math

import jax
import jax.numpy as jnp
import numpy as np
from jax import lax
from jax.experimental import pallas as pl
from jax.experimental.pallas import tpu as pltpu

F32 = jnp.float32
BF16 = jnp.bfloat16

N_DIFF_HEADS = 8
N_SB_HEADS = 8
HEAD_DIM = 128
DIFF_QK_DIM = 64
NUM_BUCKETS = 32
MAX_DISTANCE = 128
N_GROUPS = 4
EXPERTS_PER_GROUP = 8
N_EXPERTS = N_GROUPS * EXPERTS_PER_GROUP
TOP_K = 2
LN_EPS = 1e-5
SUBLN_EPS = 1e-5

LANES = 128
MASK_VALUE = -1e30
SB_EXIT = -104.0
DIFF_ROW_CHUNK = 128
SB_HEADS_PER_STEP = 2
HI_HALF = 0xFFFF0000
VMEM_LIMIT = 56 * 1024 * 1024


def _cparams(sem):
    return pltpu.CompilerParams(dimension_semantics=sem, vmem_limit_bytes=VMEM_LIMIT)


def _dot(a, b):
    return jnp.dot(a, b, preferred_element_type=F32)


def _dot_nt(a, b):
    return lax.dot_general(a, b, (((1,), (1,)), ((), ())), preferred_element_type=F32)


def _split(a):
    hi = a.astype(BF16)
    lo = (a - hi.astype(F32)).astype(BF16)
    return hi, lo


def _dot3(a, b):
    a_hi, a_lo = _split(a)
    b_hi, b_lo = _split(b)
    return _dot(a_hi, b_hi) + (_dot(a_hi, b_lo) + _dot(a_lo, b_hi))


def _layer_norm(h, g, b):
    mu = jnp.mean(h, axis=-1, keepdims=True)
    d = h - mu
    var = jnp.mean(d * d, axis=-1, keepdims=True)
    return d * lax.rsqrt(var + LN_EPS) * g + b


def _ada_kernel(c_ref, w_ref, b_ref, o_ref):
    c = c_ref[...]
    s = c * jax.nn.sigmoid(c)
    o_ref[...] = _dot3(s, w_ref[...]) + b_ref[...]


def ada_modulation(c8, w_ada, b_ada, tn=1024):
    L, D, N = w_ada.shape
    return pl.pallas_call(
        _ada_kernel,
        grid=(L, N // tn),
        in_specs=[
            pl.BlockSpec((8, D), lambda l, j: (0, 0)),
            pl.BlockSpec((None, D, tn), lambda l, j: (l, 0, j)),
            pl.BlockSpec((None, 1, tn), lambda l, j: (l, 0, j)),
        ],
        out_specs=pl.BlockSpec((None, 8, tn), lambda l, j: (l, 0, j)),
        out_shape=jax.ShapeDtypeStruct((L, 8, N), F32),
        compiler_params=_cparams(("arbitrary", "arbitrary")),
        name="ada",
    )(c8, w_ada, b_ada.reshape(L, 1, N))


def _inproj_kernel(x_ref, sc_ref, sh_ref, w_ref, o_ref, u_ref):
    @pl.when(pl.program_id(1) == 0)
    def _():
        u_ref[...] = (x_ref[...] * (1.0 + sc_ref[...]) + sh_ref[...]).astype(BF16)

    o_ref[...] = _dot(u_ref[...], w_ref[...]).astype(o_ref.dtype)


def in_projection(x2d, sc, sh, w_bf16, layer, seq, tm=1024, tn=1536):
    M, D = x2d.shape
    N = w_bf16.shape[2]
    tm = min(tm, seq)
    tn = min(tn, N)
    bpb = seq // tm
    return pl.pallas_call(
        _inproj_kernel,
        grid=(M // tm, N // tn),
        in_specs=[
            pl.BlockSpec((tm, D), lambda i, j: (i, 0)),
            pl.BlockSpec((None, 1, D), lambda i, j: (i // bpb, 0, 0)),
            pl.BlockSpec((None, 1, D), lambda i, j: (i // bpb, 0, 0)),
            pl.BlockSpec((None, D, tn), lambda i, j: (layer, 0, j)),
        ],
        out_specs=pl.BlockSpec((tm, tn), lambda i, j: (i, j)),
        out_shape=jax.ShapeDtypeStruct((M, N), BF16),
        scratch_shapes=[pltpu.VMEM((tm, D), BF16)],
        compiler_params=_cparams(("arbitrary", "arbitrary")),
        name="inproj",
    )(x2d, sc, sh, w_bf16)


def _t5_bucket_np(rel):
    n = np.maximum(rel, 0)
    max_exact = NUM_BUCKETS // 2
    ratio = np.maximum(n, 1).astype(np.float32) / np.float32(max_exact)
    large = max_exact + (np.log(ratio).astype(np.float32) / np.float32(math.log(MAX_DISTANCE / max_exact))
                         * np.float32(NUM_BUCKETS - max_exact)).astype(np.int32)
    large = np.minimum(large, NUM_BUCKETS - 1)
    return np.where(n < max_exact, n, large)


def _bucket_thresholds():
    buckets = _t5_bucket_np(np.arange(2 * MAX_DISTANCE))
    assert np.all(np.diff(buckets) >= 0) and buckets[MAX_DISTANCE] == NUM_BUCKETS - 1
    return [int(np.argmax(buckets >= k)) for k in range(NUM_BUCKETS)]


def _relbias_kernel(tab_ref, o_ref, *, t, n_heads, thresholds):
    h = pl.program_id(0)
    row = lax.broadcasted_iota(jnp.int32, (t, t), 0)
    col = lax.broadcasted_iota(jnp.int32, (t, t), 1)
    for blk in range(2):
        rel = row - col + blk * t
        for mp in range(2):
            entry = lambda k: tab_ref[(k * n_heads + h) * 2 + mp]
            far = entry(NUM_BUCKETS - 1)
            val = jnp.full((t, t), entry(0) - far, F32)
            for k in range(1, NUM_BUCKETS):
                val = jnp.where(rel >= thresholds[k], entry(k) - far, val)
            if blk == 0:
                val = jnp.where(rel < 0, MASK_VALUE, val)
            o_ref[blk, mp * t:(mp + 1) * t, :] = val


def rel_bias_tiles(rel_bias, t):
    assert t >= MAX_DISTANCE
    n_heads = rel_bias.shape[1]
    kern = functools.partial(_relbias_kernel, t=t, n_heads=n_heads, thresholds=_bucket_thresholds())
    grid_spec = pltpu.PrefetchScalarGridSpec(
        num_scalar_prefetch=1,
        grid=(n_heads,),
        in_specs=[],
        out_specs=pl.BlockSpec((None, 2, 2 * t, t), lambda h, tab: (h, 0, 0, 0)),
    )
    return pl.pallas_call(
        kern,
        grid_spec=grid_spec,
        out_shape=jax.ShapeDtypeStruct((n_heads, 2, 2 * t, t), F32),
        compiler_params=_cparams(("arbitrary",)),
        name="relbias",
    )(rel_bias.reshape(-1).astype(F32))


def _diff_kernel(lam_ref, q_ref, k_ref, v_ref, bias_ref, g_ref, o_ref, qs_ref, m_ref, acc_ref,
                 *, t, rc, out_scale):
    qi = pl.program_id(2)
    q = q_ref[...] * BF16(DIFF_QK_DIM ** -0.5)
    lane = lax.broadcasted_iota(jnp.int32, q.shape, 1)
    zero = jnp.zeros_like(q)
    qs_ref[0:t, :] = jnp.where(lane < DIFF_QK_DIM, q, zero)
    qs_ref[t:2 * t, :] = jnp.where(lane >= DIFF_QK_DIM, q, zero)
    m_ref[...] = jnp.full(m_ref.shape, MASK_VALUE, F32)
    acc_ref[...] = jnp.zeros(acc_ref.shape, F32)
    ones = jnp.ones((t, HEAD_DIM), BF16)

    def step(j, bias_blk):
        kb = k_ref[pl.ds(pl.multiple_of(j * t, t), t), :]
        vb = v_ref[pl.ds(pl.multiple_of(j * t, t), t), :]
        vb1 = jnp.concatenate([vb, ones], axis=1)
        for r in range(2 * t // rc):
            rows = pl.ds(r * rc, rc)
            s = _dot_nt(qs_ref[rows, :], kb)
            if bias_blk is not None:
                s = s + bias_ref[bias_blk, rows, :]
            m_old = m_ref[rows, :]
            m_new = jnp.maximum(m_old, jnp.max(s, axis=1, keepdims=True))
            p = jnp.exp(s - jnp.concatenate([m_new] * (t // LANES), axis=1)).astype(BF16)
            alpha = jnp.exp(m_old - m_new)
            acc_ref[rows, :] = jnp.concatenate([alpha, alpha], axis=1) * acc_ref[rows, :] + _dot(p, vb1)
            m_ref[rows, :] = m_new

    def far_body(j, carry):
        step(j, None)
        return carry

    lax.fori_loop(0, jnp.maximum(qi - 1, 0), far_body, 0)

    @pl.when(qi >= 1)
    def _():
        step(qi - 1, 1)

    step(qi, 0)

    acc = acc_ref[...]
    r = acc[:, :HEAD_DIM] / acc[:, HEAD_DIM:]
    o = r[:t] - lam_ref[0] * r[t:]
    o = o * lax.rsqrt(jnp.mean(o * o, axis=-1, keepdims=True) + SUBLN_EPS)
    o_ref[...] = (o * g_ref[...] * out_scale).astype(o_ref.dtype)


def diff_attention(proj, lam, bias_tiles, subln_g, batch, seq, lam_init, t):
    M = proj.shape[0]
    H = N_DIFF_HEADS
    nq = seq // t
    kern = functools.partial(_diff_kernel, t=t, rc=min(DIFF_ROW_CHUNK, t), out_scale=1.0 - lam_init)
    grid_spec = pltpu.PrefetchScalarGridSpec(
        num_scalar_prefetch=1,
        grid=(H, batch, nq),
        in_specs=[
            pl.BlockSpec((t, HEAD_DIM), lambda h, b, qi, s: (b * nq + qi, h)),
            pl.BlockSpec((seq, HEAD_DIM), lambda h, b, qi, s: (b, H + h)),
            pl.BlockSpec((seq, HEAD_DIM), lambda h, b, qi, s: (b, 2 * H + h)),
            pl.BlockSpec((None, 2, 2 * t, t), lambda h, b, qi, s: (h, 0, 0, 0)),
            pl.BlockSpec((1, HEAD_DIM), lambda h, b, qi, s: (0, 0)),
        ],
        out_specs=pl.BlockSpec((t, HEAD_DIM), lambda h, b, qi, s: (b * nq + qi, h)),
        scratch_shapes=[
            pltpu.VMEM((2 * t, HEAD_DIM), BF16),
            pltpu.VMEM((2 * t, LANES), F32),
            pltpu.VMEM((2 * t, 2 * HEAD_DIM), F32),
        ],
    )
    return pl.pallas_call(
        kern,
        grid_spec=grid_spec,
        out_shape=jax.ShapeDtypeStruct((M, H * HEAD_DIM), BF16),
        compiler_params=_cparams(("arbitrary", "arbitrary", "arbitrary")),
        name="diffattn",
    )(lam.reshape(1).astype(F32), proj, proj, proj, bias_tiles, subln_g.reshape(1, HEAD_DIM))


def _sb_kernel(q_ref, k_ref, v_ref, o_ref, carry_ref, acc_ref, *, t, hp, scale):
    qi = pl.program_id(2)
    row = lax.broadcasted_iota(jnp.int32, (t, t), 0)
    col = lax.broadcasted_iota(jnp.int32, (t, t), 1)
    upper = jnp.where(row > col, 1.0, 0.0).astype(BF16)
    strict = col < row

    carry_ref[...] = jnp.zeros(carry_ref.shape, F32)
    acc_ref[...] = jnp.zeros(acc_ref.shape, F32)

    def step(j, diagonal):
        keys = pl.ds(pl.multiple_of(j * t, t), t)
        for hh in range(hp):
            cols = slice(hh * HEAD_DIM, (hh + 1) * HEAD_DIM)
            z = _dot_nt(q_ref[:, cols], k_ref[keys, cols]) * scale
            lp = jnp.log(1.0 + jnp.exp(-jnp.abs(z)))
            log_keep = jnp.minimum(-z, 0.0) - lp
            log_beta = jnp.minimum(z, 0.0) - lp
            if diagonal:
                log_keep = jnp.where(strict, log_keep, 0.0)
            hi, lo = _split(log_keep)
            suffix = _dot(hi, upper) + _dot(lo, upper)
            carry = carry_ref[hh]
            a = jnp.exp(log_beta + suffix + jnp.concatenate([carry] * (t // LANES), axis=1))
            if diagonal:
                a = jnp.where(strict, a, 0.0)
            acc_ref[:, cols] += _dot(a.astype(BF16), v_ref[keys, cols])
            carry_ref[hh] = carry + jnp.sum(log_keep, axis=1, keepdims=True)

    step(qi, True)

    def cond(state):
        n, live = state
        return jnp.logical_and(n < qi, live > SB_EXIT)

    def body(state):
        n, _ = state
        step(qi - 1 - n, False)
        return n + 1, jnp.max(carry_ref[...])

    lax.while_loop(cond, body, (jnp.int32(0), jnp.max(carry_ref[...])))
    o_ref[...] = acc_ref[...].astype(o_ref.dtype)


def sb_attention(proj, batch, seq, t, hp=SB_HEADS_PER_STEP):
    M = proj.shape[0]
    H = N_SB_HEADS
    base = 3 * N_DIFF_HEADS // hp
    nq = seq // t
    w = hp * HEAD_DIM
    kern = functools.partial(_sb_kernel, t=t, hp=hp, scale=HEAD_DIM ** -0.5)
    return pl.pallas_call(
        kern,
        grid=(H // hp, batch, nq),
        in_specs=[
            pl.BlockSpec((t, w), lambda h, b, qi: (b * nq + qi, base + h)),
            pl.BlockSpec((seq, w), lambda h, b, qi: (b, base + H // hp + h)),
            pl.BlockSpec((seq, w), lambda h, b, qi: (b, base + 2 * (H // hp) + h)),
        ],
        out_specs=pl.BlockSpec((t, w), lambda h, b, qi: (b * nq + qi, h)),
        out_shape=jax.ShapeDtypeStruct((M, H * HEAD_DIM), BF16),
        scratch_shapes=[pltpu.VMEM((hp, t, LANES), F32), pltpu.VMEM((t, w), F32)],
        compiler_params=_cparams(("arbitrary", "arbitrary", "arbitrary")),
        name="sbattn",
    )(proj, proj, proj)


ROUTE_E1, ROUTE_E2, ROUTE_W1, ROUTE_W2, ROUTE_R1, ROUTE_R2 = range(6)


def _mix_kernel(a_ref, b_ref, wo_ref, x_ref, g1_ref, sc2_ref, sh2_ref, lng_ref, lnb_ref, wrt_ref, brt_ref,
                x1_ref, u2_ref, route_ref, cnt_ref, carry_ref, *, alpha, half):
    i = pl.program_id(0)

    @pl.when(i == 0)
    def _():
        carry_ref[...] = jnp.zeros(carry_ref.shape, F32)

    mix = _dot(a_ref[...], wo_ref[pl.ds(0, half), :]) + _dot(b_ref[...], wo_ref[pl.ds(half, half), :])
    h = alpha * x_ref[...] + (1.0 + g1_ref[...]) * mix
    x1 = _layer_norm(h, lng_ref[...], lnb_ref[...])
    x1_ref[...] = x1
    u2 = x1 * (1.0 + sc2_ref[...]) + sh2_ref[...]
    nseg = u2.shape[1] // (2 * LANES)
    for s in range(nseg):
        lo = u2[:, 2 * s * LANES:(2 * s + 1) * LANES].astype(BF16).astype(F32)
        hi = u2[:, (2 * s + 1) * LANES:(2 * s + 2) * LANES].astype(BF16).astype(F32)
        lo_bits = lax.shift_right_logical(lax.bitcast_convert_type(lo, jnp.uint32), jnp.uint32(16))
        hi_bits = lax.bitcast_convert_type(hi, jnp.uint32) & jnp.uint32(HI_HALF)
        u2_ref[pl.ds(s, u2.shape[0], stride=nseg), :] = lo_bits | hi_bits

    logits = _dot3(u2, wrt_ref[...]) + brt_ref[...]
    tm = logits.shape[0]
    lane = lax.broadcasted_iota(jnp.int32, (tm, LANES), 1).astype(F32)
    gmask = lane < N_GROUPS
    gl = jnp.where(gmask, logits, MASK_VALUE)
    gmax = jnp.max(gl, axis=1, keepdims=True)
    gidx = jnp.min(jnp.where(gmask & (gl == gmax), lane, float(LANES)), axis=1, keepdims=True)
    gsum = jnp.sum(jnp.where(gmask, jnp.exp(gl - gmax), 0.0), axis=1, keepdims=True)
    gp = 1.0 / gsum

    lo = N_GROUPS + gidx * EXPERTS_PER_GROUP
    emask = (lane >= lo) & (lane < lo + EXPERTS_PER_GROUP)
    el = jnp.where(emask, logits, MASK_VALUE)
    v1 = jnp.max(el, axis=1, keepdims=True)
    i1 = jnp.min(jnp.where(emask & (el == v1), lane, float(LANES)), axis=1, keepdims=True)
    emask2 = emask & (lane != i1)
    el2 = jnp.where(emask2, logits, MASK_VALUE)
    v2 = jnp.max(el2, axis=1, keepdims=True)
    i2 = jnp.min(jnp.where(emask2 & (el2 == v2), lane, float(LANES)), axis=1, keepdims=True)
    tt = jnp.exp(v2 - v1)
    w1 = gp / (1.0 + tt)
    w2 = gp * tt / (1.0 + tt)
    e1 = i1 - N_GROUPS
    e2 = i2 - N_GROUPS

    oh1 = lane == e1
    oh2 = lane == e2
    oh1b = jnp.where(oh1, 1.0, 0.0).astype(BF16)
    oh2b = jnp.where(oh2, 1.0, 0.0).astype(BF16)
    row = lax.broadcasted_iota(jnp.int32, (tm, tm), 0)
    col = lax.broadcasted_iota(jnp.int32, (tm, tm), 1)
    lower = jnp.where(col < row, 1.0, 0.0).astype(BF16)
    c1 = jnp.sum(jnp.where(oh1, 1.0, 0.0), axis=0, keepdims=True)
    c2 = jnp.sum(jnp.where(oh2, 1.0, 0.0), axis=0, keepdims=True)
    carry = carry_ref[...]
    r1 = _dot(lower, oh1b) + carry
    r2 = _dot(lower, oh2b) + (carry + c1)
    rank1 = jnp.sum(jnp.where(oh1, r1, 0.0), axis=1, keepdims=True)
    rank2 = jnp.sum(jnp.where(oh2, r2, 0.0), axis=1, keepdims=True)
    carry = carry + c1 + c2
    carry_ref[...] = carry
    cnt_ref[...] = jnp.broadcast_to(carry, cnt_ref.shape)

    out = jnp.zeros((tm, LANES), F32)
    for ln, val in ((ROUTE_E1, e1), (ROUTE_E2, e2), (ROUTE_W1, w1), (ROUTE_W2, w2),
                    (ROUTE_R1, rank1), (ROUTE_R2, rank2)):
        out = jnp.where(lane == float(ln), val, out)
    route_ref[...] = out


def mix_ln_route(a_out, b_out, wo_bf16, layer, x2d, g1, sc2, sh2, ln_g, ln_b, w_rt, b_rt, seq, alpha, tm=256):
    M, D = x2d.shape
    half = a_out.shape[1]
    bpb = seq // tm
    nseg = D // (2 * LANES)
    row = lambda i: (i // bpb, 0, 0)
    full = lambda i: (0, 0)
    kern = functools.partial(_mix_kernel, alpha=alpha, half=half)
    return pl.pallas_call(
        kern,
        grid=(M // tm,),
        in_specs=[
            pl.BlockSpec((tm, half), lambda i: (i, 0)),
            pl.BlockSpec((tm, half), lambda i: (i, 0)),
            pl.BlockSpec((None, 2 * half, D), lambda i: (layer, 0, 0)),
            pl.BlockSpec((tm, D), lambda i: (i, 0)),
            pl.BlockSpec((None, 1, D), row),
            pl.BlockSpec((None, 1, D), row),
            pl.BlockSpec((None, 1, D), row),
            pl.BlockSpec((1, D), full),
            pl.BlockSpec((1, D), full),
            pl.BlockSpec((D, LANES), full),
            pl.BlockSpec((1, LANES), full),
        ],
        out_specs=[
            pl.BlockSpec((tm, D), lambda i: (i, 0)),
            pl.BlockSpec((tm * nseg, LANES), lambda i: (i, 0)),
            pl.BlockSpec((tm, LANES), lambda i: (i, 0)),
            pl.BlockSpec((8, LANES), full),
        ],
        out_shape=[
            jax.ShapeDtypeStruct((M, D), F32),
            jax.ShapeDtypeStruct((M * nseg, LANES), jnp.uint32),
            jax.ShapeDtypeStruct((M, LANES), F32),
            jax.ShapeDtypeStruct((8, LANES), F32),
        ],
        scratch_shapes=[pltpu.VMEM((1, LANES), F32)],
        compiler_params=_cparams(("arbitrary",)),
        name="mix",
    )(a_out, b_out, wo_bf16, x2d, g1, sc2, sh2, ln_g.reshape(1, D), ln_b.reshape(1, D), w_rt, b_rt)


def _item_copy(src_hbm, dst_vmem, src_item, dst_item, rpi, sem):
    src = src_hbm.at[pl.ds(pl.multiple_of(src_item * rpi, rpi), rpi), :]
    dst = dst_vmem.at[pl.ds(pl.multiple_of(dst_item * rpi, rpi), rpi), :]
    return pltpu.make_async_copy(src, dst, sem)


def _start_items(idx_ref, base, n_items, rpi, src_hbm, dst_vmem, sem):
    def issue(r, c):
        _item_copy(src_hbm, dst_vmem, idx_ref[base + r], r, rpi, sem).start()
        return c

    lax.fori_loop(0, n_items, issue, 0)


def _wait_items(n_items, rpi, src_hbm, dst_vmem, sem):
    pltpu.make_async_copy(src_hbm.at[pl.ds(0, n_items * rpi), :], dst_vmem, sem).wait()


def _expert_kernel(tok_ref, be_ref, nused_ref, u_hbm, wg_ref, wu_ref, wd_ref, o_ref, xbuf_ref, xs_ref, sems,
                   *, tb, nseg, nrow):
    i = pl.program_id(0)
    n_used = nused_ref[0]
    slot = i % 2

    def start(tile, s):
        _start_items(tok_ref, tile * tb, tb, nseg, u_hbm, xbuf_ref.at[s], sems.at[s])

    @pl.when(i == 0)
    def _():
        start(0, 0)

    @pl.when(i + 1 < n_used)
    def _():
        start(i + 1, 1 - slot)

    @pl.when(i < n_used)
    def _():
        _wait_items(tb, nseg, u_hbm, xbuf_ref.at[slot], sems.at[slot])
        for s in range(nseg):
            w = xbuf_ref[slot, pl.ds(s, tb, stride=nseg), :]
            lo = lax.bitcast_convert_type(lax.shift_left(w, jnp.uint32(16)), F32)
            hi = lax.bitcast_convert_type(w & jnp.uint32(HI_HALF), F32)
            xs_ref[:, 2 * s * LANES:(2 * s + 1) * LANES] = lo.astype(BF16)
            xs_ref[:, (2 * s + 1) * LANES:(2 * s + 2) * LANES] = hi.astype(BF16)
        x = xs_ref[...]
        g = _dot(x, wg_ref[...])
        u = _dot(x, wu_ref[...])
        h = (g * jax.nn.sigmoid(g) * u).astype(BF16)
        y = _dot(h, wd_ref[...])
        for s in range(nrow):
            o_ref[pl.ds(s, tb, stride=nrow), :] = y[:, s * LANES:(s + 1) * LANES]

    @pl.when(i >= n_used)
    def _():
        o_ref[...] = jnp.zeros(o_ref.shape, o_ref.dtype)


def expert_mlp(pad_tok, block_e, n_used, u2p, wg, wu, wd, layer, tb):
    P = pad_tok.shape[0]
    D = wg.shape[2]
    F = wg.shape[3]
    nseg = D // (2 * LANES)
    nrow = D // LANES

    def wmap(i, tok, be, nu):
        return (layer, be[jnp.minimum(i, jnp.maximum(nu[0] - 1, 0))], 0, 0)

    grid_spec = pltpu.PrefetchScalarGridSpec(
        num_scalar_prefetch=3,
        grid=(P // tb,),
        in_specs=[
            pl.BlockSpec(memory_space=pl.ANY),
            pl.BlockSpec((None, None, D, F), wmap),
            pl.BlockSpec((None, None, D, F), wmap),
            pl.BlockSpec((None, None, F, D), wmap),
        ],
        out_specs=pl.BlockSpec((tb * nrow, LANES), lambda i, tok, be, nu: (i, 0)),
        scratch_shapes=[pltpu.VMEM((2, tb * nseg, LANES), jnp.uint32), pltpu.VMEM((tb, D), BF16),
                        pltpu.SemaphoreType.DMA((2,))],
    )
    return pl.pallas_call(
        functools.partial(_expert_kernel, tb=tb, nseg=nseg, nrow=nrow),
        grid_spec=grid_spec,
        out_shape=jax.ShapeDtypeStruct((P * nrow, LANES), F32),
        compiler_params=_cparams(("arbitrary",)),
        name="expert",
    )(pad_tok, block_e, n_used, u2p, wg, wu, wd)


def _final_kernel(dest_ref, x1_ref, route_ref, g2_ref, lng_ref, lnb_ref, yb_hbm, o_ref, buf_ref, sems,
                  *, alpha, tm, m, nrow):
    i = pl.program_id(0)
    slot = i % 2

    def start(tile, s):
        for k in range(TOP_K):
            _start_items(dest_ref, k * m + tile * tm, tm, nrow, yb_hbm, buf_ref.at[s, k], sems.at[s])

    @pl.when(i == 0)
    def _():
        start(0, 0)

    @pl.when(i + 1 < pl.num_programs(0))
    def _():
        start(i + 1, 1 - slot)

    for k in range(TOP_K):
        _wait_items(tm, nrow, yb_hbm, buf_ref.at[slot, k], sems.at[slot])

    def expert_out(k):
        chunks = [buf_ref[slot, k, pl.ds(s, tm, stride=nrow), :] for s in range(nrow)]
        return jnp.concatenate(chunks, axis=1)

    route = route_ref[...]
    w1 = route[:, ROUTE_W1:ROUTE_W1 + 1]
    w2 = route[:, ROUTE_W2:ROUTE_W2 + 1]
    y = w1 * expert_out(0) + w2 * expert_out(1)
    h = alpha * x1_ref[...] + (1.0 + g2_ref[...]) * y
    o_ref[...] = _layer_norm(h, lng_ref[...], lnb_ref[...])


def combine_ln(dest_flat, x1, route, g2, ln_g, ln_b, yb, seq, alpha, tm=256):
    M, D = x1.shape
    bpb = seq // tm
    nrow = D // LANES
    grid_spec = pltpu.PrefetchScalarGridSpec(
        num_scalar_prefetch=1,
        grid=(M // tm,),
        in_specs=[
            pl.BlockSpec((tm, D), lambda i, d: (i, 0)),
            pl.BlockSpec((tm, LANES), lambda i, d: (i, 0)),
            pl.BlockSpec((None, 1, D), lambda i, d: (i // bpb, 0, 0)),
            pl.BlockSpec((1, D), lambda i, d: (0, 0)),
            pl.BlockSpec((1, D), lambda i, d: (0, 0)),
            pl.BlockSpec(memory_space=pl.ANY),
        ],
        out_specs=pl.BlockSpec((tm, D), lambda i, d: (i, 0)),
        scratch_shapes=[pltpu.VMEM((2, TOP_K, tm * nrow, LANES), F32), pltpu.SemaphoreType.DMA((2,))],
    )
    return pl.pallas_call(
        functools.partial(_final_kernel, alpha=alpha, tm=tm, m=M, nrow=nrow),
        grid_spec=grid_spec,
        out_shape=jax.ShapeDtypeStruct((M, D), F32),
        compiler_params=_cparams(("arbitrary",)),
        name="final",
    )(dest_flat, x1, route, g2, ln_g.reshape(1, D), ln_b.reshape(1, D), yb)


def moe_layout(route, cnt, tb):
    M = route.shape[0]
    n_assign = M * TOP_K
    counts = cnt[0, :N_EXPERTS].astype(jnp.int32)
    padded = ((counts + tb - 1) // tb) * tb
    pends = jnp.cumsum(padded)
    pstarts = pends - padded
    n_tiles = (n_assign + N_EXPERTS * (tb - 1) + tb - 1) // tb
    e = route[:, ROUTE_E1:ROUTE_E2 + 1].astype(jnp.int32)
    rank = route[:, ROUTE_R1:ROUTE_R2 + 1].astype(jnp.int32)
    dest = pstarts[e] + rank
    dest_flat = dest.T.reshape(-1)
    tok = jnp.tile(jnp.arange(M, dtype=jnp.int32), TOP_K)
    pad_tok = jnp.zeros((n_tiles * tb,), jnp.int32).at[dest_flat].set(tok)
    tile_start = jnp.arange(n_tiles, dtype=jnp.int32) * tb
    block_e = jnp.sum((pends[None, :] <= tile_start[:, None]).astype(jnp.int32), axis=1)
    block_e = jnp.minimum(block_e, N_EXPERTS - 1)
    n_used = (pends[-1] // tb).astype(jnp.int32).reshape(1)
    return dest_flat, pad_tok, block_e, n_used


def _forward(x, c, w_ada, b_ada, w_in, diff_lambda, diff_subln_g, rel_bias, w_o, ln_g, ln_b, w_group, b_group,
             w_router, b_router, w_gate, w_up, w_down, *, diff_tile, sb_tile, moe_tile, ada_tn):
    B, S, D = x.shape
    depth = w_ada.shape[0]
    M = B * S
    alpha = (2 * depth) ** 0.25

    c8 = jnp.zeros((8, D), F32).at[:B].set(c)
    mod = ada_modulation(c8, w_ada, b_ada, ada_tn)[:, :B]
    mod = mod.reshape(depth, B, 6, 1, D)
    bias_tiles = rel_bias_tiles(rel_bias, diff_tile)
    w_in_b, w_o_b = w_in.astype(BF16), w_o.astype(BF16)
    w_gate_b, w_up_b, w_down_b = w_gate.astype(BF16), w_up.astype(BF16), w_down.astype(BF16)

    x2d = x.reshape(M, D)
    for l in range(depth):
        sh1, sc1, g1, sh2, sc2, g2 = (mod[l, :, n] for n in range(6))
        proj = in_projection(x2d, sc1, sh1, w_in_b, l, S)

        lam_init = 0.8 - 0.6 * math.exp(-0.3 * l)
        lp = diff_lambda[l].astype(F32)
        lam = jnp.exp(jnp.sum(lp[0] * lp[1])) - jnp.exp(jnp.sum(lp[2] * lp[3])) + lam_init
        a_out = diff_attention(proj, lam, bias_tiles, diff_subln_g[l], B, S, lam_init, diff_tile)
        b_out = sb_attention(proj, B, S, sb_tile)

        w_rt = jnp.zeros((D, LANES), F32).at[:, :N_GROUPS].set(w_group[l])
        w_rt = w_rt.at[:, N_GROUPS:N_GROUPS + N_EXPERTS].set(w_router[l])
        b_rt = jnp.zeros((1, LANES), F32).at[0, :N_GROUPS].set(b_group[l])
        b_rt = b_rt.at[0, N_GROUPS:N_GROUPS + N_EXPERTS].set(b_router[l])
        x1, u2, route, cnt = mix_ln_route(a_out, b_out, w_o_b, l, x2d, g1, sc2, sh2,
                                          ln_g[l, 0], ln_b[l, 0], w_rt, b_rt, S, alpha)

        dest_flat, pad_tok, block_e, n_used = moe_layout(route, cnt, moe_tile)
        yb = expert_mlp(pad_tok, block_e, n_used, u2, w_gate_b, w_up_b, w_down_b, l, moe_tile)
        x2d = combine_ln(dest_flat, x1, route, g2, ln_g[l, 1], ln_b[l, 1], yb, S, alpha)
    return x2d.reshape(B, S, D)


def kernel(x, c, w_ada, b_ada, w_in, diff_lambda, diff_subln_g, rel_bias, w_o, ln_g, ln_b, w_group, b_group,
           w_router, b_router, w_gate, w_up, w_down):
    return _forward(x, c, w_ada, b_ada, w_in, diff_lambda, diff_subln_g, rel_bias, w_o, ln_g, ln_b, w_group,
                    b_group, w_router, b_router, w_gate, w_up, w_down,
                    diff_tile=512, sb_tile=256, moe_tile=256, ada_tn=1024)
```

```python
import functools
import math

import jax
import jax.numpy as jnp
import numpy as np
from jax import lax
from jax.experimental import pallas as pl
from jax.experimental.pallas import tpu as pltpu

F32 = jnp.float32
BF16 = jnp.bfloat16

N_DIFF_HEADS = 8
N_SB_HEADS = 8
HEAD_DIM = 128
DIFF_QK_DIM = 64
NUM_BUCKETS = 32
MAX_DISTANCE = 128
N_GROUPS = 4
EXPERTS_PER_GROUP = 8
N_EXPERTS = N_GROUPS * EXPERTS_PER_GROUP
TOP_K = 2
LN_EPS = 1e-5
SUBLN_EPS = 1e-5

LANES = 128
MASK_VALUE = -1e30
SB_EXIT = -104.0
DIFF_ROW_CHUNK = 128
SB_HEADS_PER_STEP = 2
HI_HALF = 0xFFFF0000
VMEM_LIMIT = 56 * 1024 * 1024


def _cparams(sem):
    return pltpu.CompilerParams(dimension_semantics=sem, vmem_limit_bytes=VMEM_LIMIT)


def _dot(a, b):
    return jnp.dot(a, b, preferred_element_type=F32)


def _dot_nt(a, b):
    return lax.dot_general(a, b, (((1,), (1,)), ((), ())), preferred_element_type=F32)


def _split(a):
    hi = a.astype(BF16)
    lo = (a - hi.astype(F32)).astype(BF16)
    return hi, lo


def _dot3(a, b):
    a_hi, a_lo = _split(a)
    b_hi, b_lo = _split(b)
    return _dot(a_hi, b_hi) + (_dot(a_hi, b_lo) + _dot(a_lo, b_hi))


def _layer_norm(h, g, b):
    mu = jnp.mean(h, axis=-1, keepdims=True)
    d = h - mu
    var = jnp.mean(d * d, axis=-1, keepdims=True)
    return d * lax.rsqrt(var + LN_EPS) * g + b


def _ada_kernel(c_ref, w_ref, b_ref, o_ref):
    c = c_ref[...]
    s = c * jax.nn.sigmoid(c)
    o_ref[...] = _dot3(s, w_ref[...]) + b_ref[...]


def ada_modulation(c8, w_ada, b_ada, tn=1024):
    L, D, N = w_ada.shape
    return pl.pallas_call(
        _ada_kernel,
        grid=(L, N // tn),
        in_specs=[
            pl.BlockSpec((8, D), lambda l, j: (0, 0)),
            pl.BlockSpec((None, D, tn), lambda l, j: (l, 0, j)),
            pl.BlockSpec((None, 1, tn), lambda l, j: (l, 0, j)),
        ],
        out_specs=pl.BlockSpec((None, 8, tn), lambda l, j: (l, 0, j)),
        out_shape=jax.ShapeDtypeStruct((L, 8, N), F32),
        compiler_params=_cparams(("arbitrary", "arbitrary")),
        name="ada",
    )(c8, w_ada, b_ada.reshape(L, 1, N))


def _inproj_kernel(x_ref, sc_ref, sh_ref, w_ref, o_ref, u_ref):
    @pl.when(pl.program_id(1) == 0)
    def _():
        u_ref[...] = (x_ref[...] * (1.0 + sc_ref[...]) + sh_ref[...]).astype(BF16)

    o_ref[...] = _dot(u_ref[...], w_ref[...]).astype(o_ref.dtype)


def in_projection(x2d, sc, sh, w_bf16, layer, seq, tm=1024, tn=1536):
    M, D = x2d.shape
    N = w_bf16.shape[2]
    tm = min(tm, seq)
    tn = min(tn, N)
    bpb = seq // tm
    return pl.pallas_call(
        _inproj_kernel,
        grid=(M // tm, N // tn),
        in_specs=[
            pl.BlockSpec((tm, D), lambda i, j: (i, 0)),
            pl.BlockSpec((None, 1, D), lambda i, j: (i // bpb, 0, 0)),
            pl.BlockSpec((None, 1, D), lambda i, j: (i // bpb, 0, 0)),
            pl.BlockSpec((None, D, tn), lambda i, j: (layer, 0, j)),
        ],
        out_specs=pl.BlockSpec((tm, tn), lambda i, j: (i, j)),
        out_shape=jax.ShapeDtypeStruct((M, N), BF16),
        scratch_shapes=[pltpu.VMEM((tm, D), BF16)],
        compiler_params=_cparams(("arbitrary", "arbitrary")),
        name="inproj",
    )(x2d, sc, sh, w_bf16)


def _t5_bucket_np(rel):
    n = np.maximum(rel, 0)
    max_exact = NUM_BUCKETS // 2
    ratio = np.maximum(n, 1).astype(np.float32) / np.float32(max_exact)
    large = max_exact + (np.log(ratio).astype(np.float32) / np.float32(math.log(MAX_DISTANCE / max_exact))
                         * np.float32(NUM_BUCKETS - max_exact)).astype(np.int32)
    large = np.minimum(large, NUM_BUCKETS - 1)
    return np.where(n < max_exact, n, large)


def _bucket_thresholds():
    buckets = _t5_bucket_np(np.arange(2 * MAX_DISTANCE))
    assert np.all(np.diff(buckets) >= 0) and buckets[MAX_DISTANCE] == NUM_BUCKETS - 1
    return [int(np.argmax(buckets >= k)) for k in range(NUM_BUCKETS)]


def _relbias_kernel(tab_ref, o_ref, *, t, n_heads, thresholds):
    h = pl.program_id(0)
    row = lax.broadcasted_iota(jnp.int32, (t, t), 0)
    col = lax.broadcasted_iota(jnp.int32, (t, t), 1)
    for blk in range(2):
        rel = row - col + blk * t
        for mp in range(2):
            entry = lambda k: tab_ref[(k * n_heads + h) * 2 + mp]
            far = entry(NUM_BUCKETS - 1)
            val = jnp.full((t, t), entry(0) - far, F32)
            for k in range(1, NUM_BUCKETS):
                val = jnp.where(rel >= thresholds[k], entry(k) - far, val)
            if blk == 0:
                val = jnp.where(rel < 0, MASK_VALUE, val)
            o_ref[blk, mp * t:(mp + 1) * t, :] = val


def rel_bias_tiles(rel_bias, t):
    assert t >= MAX_DISTANCE
    n_heads = rel_bias.shape[1]
    kern = functools.partial(_relbias_kernel, t=t, n_heads=n_heads, thresholds=_bucket_thresholds())
    grid_spec = pltpu.PrefetchScalarGridSpec(
        num_scalar_prefetch=1,
        grid=(n_heads,),
        in_specs=[],
        out_specs=pl.BlockSpec((None, 2, 2 * t, t), lambda h, tab: (h, 0, 0, 0)),
    )
    return pl.pallas_call(
        kern,
        grid_spec=grid_spec,
        out_shape=jax.ShapeDtypeStruct((n_heads, 2, 2 * t, t), F32),
        compiler_params=_cparams(("arbitrary",)),
        name="relbias",
    )(rel_bias.reshape(-1).astype(F32))


def _diff_kernel(lam_ref, q_ref, k_ref, v_ref, bias_ref, g_ref, o_ref, qs_ref, s_ref, m_ref, acc_ref,
                 *, t, rc, out_scale):
    qi = pl.program_id(2)
    q = q_ref[...] * BF16(DIFF_QK_DIM ** -0.5)
    lane = lax.broadcasted_iota(jnp.int32, q.shape, 1)
    zero = jnp.zeros_like(q)
    qs_ref[0:t, :] = jnp.where(lane < DIFF_QK_DIM, q, zero)
    qs_ref[t:2 * t, :] = jnp.where(lane >= DIFF_QK_DIM, q, zero)
    m_ref[...] = jnp.full(m_ref.shape, MASK_VALUE, F32)
    acc_ref[...] = jnp.zeros(acc_ref.shape, F32)
    ones = jnp.ones((t, HEAD_DIM), BF16)

    def keys(j):
        return pl.ds(pl.multiple_of(j * t, t), t)

    def logits(j):
        kb = k_ref[keys(j), :]
        for r in range(2 * t // rc):
            rows = pl.ds(r * rc, rc)
            s_ref[rows, :] = _dot_nt(qs_ref[rows, :], kb)

    def step(j, bias_blk, produce_next):
        vb = v_ref[keys(j), :]
        vb1 = jnp.concatenate([vb, ones], axis=1)
        if produce_next:
            kb_next = k_ref[keys(j + 1), :]
        for r in range(2 * t // rc):
            rows = pl.ds(r * rc, rc)
            s = s_ref[rows, :]
            if produce_next:
                s_ref[rows, :] = _dot_nt(qs_ref[rows, :], kb_next)
            if bias_blk is not None:
                s = s + bias_ref[bias_blk, rows, :]
            m_old = m_ref[rows, :]
            m_new = jnp.maximum(m_old, jnp.max(s, axis=1, keepdims=True))
            p = jnp.exp(s - jnp.concatenate([m_new] * (t // LANES), axis=1)).astype(BF16)
            alpha = jnp.exp(m_old - m_new)
            acc_ref[rows, :] = jnp.concatenate([alpha, alpha], axis=1) * acc_ref[rows, :] + _dot(p, vb1)
            m_ref[rows, :] = m_new

    logits(0)

    def far_body(j, carry):
        step(j, None, True)
        return carry

    lax.fori_loop(0, jnp.maximum(qi - 1, 0), far_body, 0)

    @pl.when(qi >= 1)
    def _():
        step(qi - 1, 1, True)

    step(qi, 0, False)

    acc = acc_ref[...]
    r = acc[:, :HEAD_DIM] / acc[:, HEAD_DIM:]
    o = r[:t] - lam_ref[0] * r[t:]
    o = o * lax.rsqrt(jnp.mean(o * o, axis=-1, keepdims=True) + SUBLN_EPS)
    o_ref[...] = (o * g_ref[...] * out_scale).astype(o_ref.dtype)


def diff_attention(proj, lam, bias_tiles, subln_g, batch, seq, lam_init, t):
    M = proj.shape[0]
    H = N_DIFF_HEADS
    nq = seq // t
    kern = functools.partial(_diff_kernel, t=t, rc=min(DIFF_ROW_CHUNK, t), out_scale=1.0 - lam_init)
    grid_spec = pltpu.PrefetchScalarGridSpec(
        num_scalar_prefetch=1,
        grid=(H, batch, nq),
        in_specs=[
            pl.BlockSpec((t, HEAD_DIM), lambda h, b, qi, s: (b * nq + qi, h)),
            pl.BlockSpec((seq, HEAD_DIM), lambda h, b, qi, s: (b, H + h)),
            pl.BlockSpec((seq, HEAD_DIM), lambda h, b, qi, s: (b, 2 * H + h)),
            pl.BlockSpec((None, 2, 2 * t, t), lambda h, b, qi, s: (h, 0, 0, 0)),
            pl.BlockSpec((1, HEAD_DIM), lambda h, b, qi, s: (0, 0)),
        ],
        out_specs=pl.BlockSpec((t, HEAD_DIM), lambda h, b, qi, s: (b * nq + qi, h)),
        scratch_shapes=[
            pltpu.VMEM((2 * t, HEAD_DIM), BF16),
            pltpu.VMEM((2 * t, t), F32),
            pltpu.VMEM((2 * t, LANES), F32),
            pltpu.VMEM((2 * t, 2 * HEAD_DIM), F32),
        ],
    )
    return pl.pallas_call(
        kern,
        grid_spec=grid_spec,
        out_shape=jax.ShapeDtypeStruct((M, H * HEAD_DIM), BF16),
        compiler_params=_cparams(("arbitrary", "arbitrary", "arbitrary")),
        name="diffattn",
    )(lam.reshape(1).astype(F32), proj, proj, proj, bias_tiles, subln_g.reshape(1, HEAD_DIM))


def _sb_kernel(q_ref, k_ref, v_ref, o_ref, carry_ref, acc_ref, *, t, hp, scale):
    qi = pl.program_id(2)
    row = lax.broadcasted_iota(jnp.int32, (t, t), 0)
    col = lax.broadcasted_iota(jnp.int32, (t, t), 1)
    upper = jnp.where(row > col, 1.0, 0.0).astype(BF16)
    strict = col < row

    carry_ref[...] = jnp.zeros(carry_ref.shape, F32)
    acc_ref[...] = jnp.zeros(acc_ref.shape, F32)

    def step(j, diagonal):
        keys = pl.ds(pl.multiple_of(j * t, t), t)
        for hh in range(hp):
            cols = slice(hh * HEAD_DIM, (hh + 1) * HEAD_DIM)
            z = _dot_nt(q_ref[:, cols], k_ref[keys, cols]) * scale
            lp = jnp.log(1.0 + jnp.exp(-jnp.abs(z)))
            log_keep = jnp.minimum(-z, 0.0) - lp
            log_beta = jnp.minimum(z, 0.0) - lp
            if diagonal:
                log_keep = jnp.where(strict, log_keep, 0.0)
            hi, lo = _split(log_keep)
            suffix = _dot(hi, upper) + _dot(lo, upper)
            carry = carry_ref[hh]
            a = jnp.exp(log_beta + suffix + jnp.concatenate([carry] * (t // LANES), axis=1))
            if diagonal:
                a = jnp.where(strict, a, 0.0)
            acc_ref[:, cols] += _dot(a.astype(BF16), v_ref[keys, cols])
            carry_ref[hh] = carry + jnp.sum(log_keep, axis=1, keepdims=True)

    step(qi, True)

    def cond(state):
        n, live = state
        return jnp.logical_and(n < qi, live > SB_EXIT)

    def body(state):
        n, _ = state
        step(qi - 1 - n, False)
        return n + 1, jnp.max(carry_ref[...])

    lax.while_loop(cond, body, (jnp.int32(0), jnp.max(carry_ref[...])))
    o_ref[...] = acc_ref[...].astype(o_ref.dtype)


def sb_attention(proj, batch, seq, t, hp=SB_HEADS_PER_STEP):
    M = proj.shape[0]
    H = N_SB_HEADS
    base = 3 * N_DIFF_HEADS // hp
    nq = seq // t
    w = hp * HEAD_DIM
    kern = functools.partial(_sb_kernel, t=t, hp=hp, scale=HEAD_DIM ** -0.5)
    return pl.pallas_call(
        kern,
        grid=(H // hp, batch, nq),
        in_specs=[
            pl.BlockSpec((t, w), lambda h, b, qi: (b * nq + qi, base + h)),
            pl.BlockSpec((seq, w), lambda h, b, qi: (b, base + H // hp + h)),
            pl.BlockSpec((seq, w), lambda h, b, qi: (b, base + 2 * (H // hp) + h)),
        ],
        out_specs=pl.BlockSpec((t, w), lambda h, b, qi: (b * nq + qi, h)),
        out_shape=jax.ShapeDtypeStruct((M, H * HEAD_DIM), BF16),
        scratch_shapes=[pltpu.VMEM((hp, t, LANES), F32), pltpu.VMEM((t, w), F32)],
        compiler_params=_cparams(("arbitrary", "arbitrary", "arbitrary")),
        name="sbattn",
    )(proj, proj, proj)


ROUTE_E1, ROUTE_E2, ROUTE_W1, ROUTE_W2, ROUTE_R1, ROUTE_R2 = range(6)


def _mix_kernel(a_ref, b_ref, wo_ref, x_ref, g1_ref, sc2_ref, sh2_ref, lng_ref, lnb_ref, wrt_ref, brt_ref,
                x1_ref, u2_ref, route_ref, cnt_ref, carry_ref, *, alpha, half):
    i = pl.program_id(0)

    @pl.when(i == 0)
    def _():
        carry_ref[...] = jnp.zeros(carry_ref.shape, F32)

    mix = _dot(a_ref[...], wo_ref[pl.ds(0, half), :]) + _dot(b_ref[...], wo_ref[pl.ds(half, half), :])
    h = alpha * x_ref[...] + (1.0 + g1_ref[...]) * mix
    x1 = _layer_norm(h, lng_ref[...], lnb_ref[...])
    x1_ref[...] = x1
    u2 = x1 * (1.0 + sc2_ref[...]) + sh2_ref[...]
    nseg = u2.shape[1] // (2 * LANES)
    for s in range(nseg):
        lo = u2[:, 2 * s * LANES:(2 * s + 1) * LANES].astype(BF16).astype(F32)
        hi = u2[:, (2 * s + 1) * LANES:(2 * s + 2) * LANES].astype(BF16).astype(F32)
        lo_bits = lax.shift_right_logical(lax.bitcast_convert_type(lo, jnp.uint32), jnp.uint32(16))
        hi_bits = lax.bitcast_convert_type(hi, jnp.uint32) & jnp.uint32(HI_HALF)
        u2_ref[pl.ds(s, u2.shape[0], stride=nseg), :] = lo_bits | hi_bits

    logits = _dot3(u2, wrt_ref[...]) + brt_ref[...]
    tm = logits.shape[0]
    lane = lax.broadcasted_iota(jnp.int32, (tm, LANES), 1).astype(F32)
    gmask = lane < N_GROUPS
    gl = jnp.where(gmask, logits, MASK_VALUE)
    gmax = jnp.max(gl, axis=1, keepdims=True)
    gidx = jnp.min(jnp.where(gmask & (gl == gmax), lane, float(LANES)), axis=1, keepdims=True)
    gsum = jnp.sum(jnp.where(gmask, jnp.exp(gl - gmax), 0.0), axis=1, keepdims=True)
    gp = 1.0 / gsum

    lo = N_GROUPS + gidx * EXPERTS_PER_GROUP
    emask = (lane >= lo) & (lane < lo + EXPERTS_PER_GROUP)
    el = jnp.where(emask, logits, MASK_VALUE)
    v1 = jnp.max(el, axis=1, keepdims=True)
    i1 = jnp.min(jnp.where(emask & (el == v1), lane, float(LANES)), axis=1, keepdims=True)
    emask2 = emask & (lane != i1)
    el2 = jnp.where(emask2, logits, MASK_VALUE)
    v2 = jnp.max(el2, axis=1, keepdims=True)
    i2 = jnp.min(jnp.where(emask2 & (el2 == v2), lane, float(LANES)), axis=1, keepdims=True)
    tt = jnp.exp(v2 - v1)
    w1 = gp / (1.0 + tt)
    w2 = gp * tt / (1.0 + tt)
    e1 = i1 - N_GROUPS
    e2 = i2 - N_GROUPS

    oh1 = lane == e1
    oh2 = lane == e2
    oh1b = jnp.where(oh1, 1.0, 0.0).astype(BF16)
    oh2b = jnp.where(oh2, 1.0, 0.0).astype(BF16)
    row = lax.broadcasted_iota(jnp.int32, (tm, tm), 0)
    col = lax.broadcasted_iota(jnp.int32, (tm, tm), 1)
    lower = jnp.where(col < row, 1.0, 0.0).astype(BF16)
    c1 = jnp.sum(jnp.where(oh1, 1.0, 0.0), axis=0, keepdims=True)
    c2 = jnp.sum(jnp.where(oh2, 1.0, 0.0), axis=0, keepdims=True)
    carry = carry_ref[...]
    r1 = _dot(lower, oh1b) + carry
    r2 = _dot(lower, oh2b) + (carry + c1)
    rank1 = jnp.sum(jnp.where(oh1, r1, 0.0), axis=1, keepdims=True)
    rank2 = jnp.sum(jnp.where(oh2, r2, 0.0), axis=1, keepdims=True)
    carry = carry + c1 + c2
    carry_ref[...] = carry
    cnt_ref[...] = jnp.broadcast_to(carry, cnt_ref.shape)

    out = jnp.zeros((tm, LANES), F32)
    for ln, val in ((ROUTE_E1, e1), (ROUTE_E2, e2), (ROUTE_W1, w1), (ROUTE_W2, w2),
                    (ROUTE_R1, rank1), (ROUTE_R2, rank2)):
        out = jnp.where(lane == float(ln), val, out)
    route_ref[...] = out


def mix_ln_route(a_out, b_out, wo_bf16, layer, x2d, g1, sc2, sh2, ln_g, ln_b, w_rt, b_rt, seq, alpha, tm=256):
    M, D = x2d.shape
    half = a_out.shape[1]
    bpb = seq // tm
    nseg = D // (2 * LANES)
    row = lambda i: (i // bpb, 0, 0)
    full = lambda i: (0, 0)
    kern = functools.partial(_mix_kernel, alpha=alpha, half=half)
    return pl.pallas_call(
        kern,
        grid=(M // tm,),
        in_specs=[
            pl.BlockSpec((tm, half), lambda i: (i, 0)),
            pl.BlockSpec((tm, half), lambda i: (i, 0)),
            pl.BlockSpec((None, 2 * half, D), lambda i: (layer, 0, 0)),
            pl.BlockSpec((tm, D), lambda i: (i, 0)),
            pl.BlockSpec((None, 1, D), row),
            pl.BlockSpec((None, 1, D), row),
            pl.BlockSpec((None, 1, D), row),
            pl.BlockSpec((1, D), full),
            pl.BlockSpec((1, D), full),
            pl.BlockSpec((D, LANES), full),
            pl.BlockSpec((1, LANES), full),
        ],
        out_specs=[
            pl.BlockSpec((tm, D), lambda i: (i, 0)),
            pl.BlockSpec((tm * nseg, LANES), lambda i: (i, 0)),
            pl.BlockSpec((tm, LANES), lambda i: (i, 0)),
            pl.BlockSpec((8, LANES), full),
        ],
        out_shape=[
            jax.ShapeDtypeStruct((M, D), F32),
            jax.ShapeDtypeStruct((M * nseg, LANES), jnp.uint32),
            jax.ShapeDtypeStruct((M, LANES), F32),
            jax.ShapeDtypeStruct((8, LANES), F32),
        ],
        scratch_shapes=[pltpu.VMEM((1, LANES), F32)],
        compiler_params=_cparams(("arbitrary",)),
        name="mix",
    )(a_out, b_out, wo_bf16, x2d, g1, sc2, sh2, ln_g.reshape(1, D), ln_b.reshape(1, D), w_rt, b_rt)


def _item_copy(src_hbm, dst_vmem, src_item, dst_item, rpi, sem):
    src = src_hbm.at[pl.ds(pl.multiple_of(src_item * rpi, rpi), rpi), :]
    dst = dst_vmem.at[pl.ds(pl.multiple_of(dst_item * rpi, rpi), rpi), :]
    return pltpu.make_async_copy(src, dst, sem)


def _start_items(idx_ref, base, n_items, rpi, src_hbm, dst_vmem, sem, unroll=False):
    def issue(r, c):
        _item_copy(src_hbm, dst_vmem, idx_ref[base + r], r, rpi, sem).start()
        return c

    lax.fori_loop(0, n_items, issue, 0, unroll=unroll)


def _wait_items(n_items, rpi, src_hbm, dst_vmem, sem):
    pltpu.make_async_copy(src_hbm.at[pl.ds(0, n_items * rpi), :], dst_vmem, sem).wait()


def _expert_kernel(tok_ref, be_ref, nused_ref, u_hbm, wg_ref, wu_ref, wd_ref, o_ref, xbuf_ref, xs_ref, sems,
                   *, tb, nseg, nrow):
    i = pl.program_id(0)
    n_used = nused_ref[0]
    slot = i % 2

    def start(tile, s, unroll):
        _start_items(tok_ref, tile * tb, tb, nseg, u_hbm, xbuf_ref.at[s], sems.at[s], unroll)

    @pl.when(i == 0)
    def _():
        start(0, 0, False)

    @pl.when(i < n_used)
    def _():
        _wait_items(tb, nseg, u_hbm, xbuf_ref.at[slot], sems.at[slot])
        start(jnp.minimum(i + 1, n_used - 1), 1 - slot, True)
        for s in range(nseg):
            w = xbuf_ref[slot, pl.ds(s, tb, stride=nseg), :]
            lo = lax.bitcast_convert_type(lax.shift_left(w, jnp.uint32(16)), F32)
            hi = lax.bitcast_convert_type(w & jnp.uint32(HI_HALF), F32)
            xs_ref[:, 2 * s * LANES:(2 * s + 1) * LANES] = lo.astype(BF16)
            xs_ref[:, (2 * s + 1) * LANES:(2 * s + 2) * LANES] = hi.astype(BF16)
        x = xs_ref[...]
        g = _dot(x, wg_ref[...])
        u = _dot(x, wu_ref[...])
        h = (g * jax.nn.sigmoid(g) * u).astype(BF16)
        y = _dot(h, wd_ref[...])
        for s in range(nrow):
            o_ref[pl.ds(s, tb, stride=nrow), :] = y[:, s * LANES:(s + 1) * LANES]

    @pl.when(i == n_used - 1)
    def _():
        _wait_items(tb, nseg, u_hbm, xbuf_ref.at[1 - slot], sems.at[1 - slot])

    @pl.when(i >= n_used)
    def _():
        o_ref[...] = jnp.zeros(o_ref.shape, o_ref.dtype)


def expert_mlp(pad_tok, block_e, n_used, u2p, wg, wu, wd, layer, tb):
    P = pad_tok.shape[0]
    D = wg.shape[2]
    F = wg.shape[3]
    nseg = D // (2 * LANES)
    nrow = D // LANES

    def wmap(i, tok, be, nu):
        return (layer, be[jnp.minimum(i, jnp.maximum(nu[0] - 1, 0))], 0, 0)

    grid_spec = pltpu.PrefetchScalarGridSpec(
        num_scalar_prefetch=3,
        grid=(P // tb,),
        in_specs=[
            pl.BlockSpec(memory_space=pl.ANY),
            pl.BlockSpec((None, None, D, F), wmap),
            pl.BlockSpec((None, None, D, F), wmap),
            pl.BlockSpec((None, None, F, D), wmap),
        ],
        out_specs=pl.BlockSpec((tb * nrow, LANES), lambda i, tok, be, nu: (i, 0)),
        scratch_shapes=[pltpu.VMEM((2, tb * nseg, LANES), jnp.uint32), pltpu.VMEM((tb, D), BF16),
                        pltpu.SemaphoreType.DMA((2,))],
    )
    return pl.pallas_call(
        functools.partial(_expert_kernel, tb=tb, nseg=nseg, nrow=nrow),
        grid_spec=grid_spec,
        out_shape=jax.ShapeDtypeStruct((P * nrow, LANES), F32),
        compiler_params=_cparams(("arbitrary",)),
        name="expert",
    )(pad_tok, block_e, n_used, u2p, wg, wu, wd)


def _final_kernel(dest_ref, x1_ref, route_ref, g2_ref, lng_ref, lnb_ref, yb_hbm, o_ref, buf_ref, sems,
                  *, alpha, tm, m, nrow):
    i = pl.program_id(0)
    slot = i % 2

    last = pl.num_programs(0) - 1

    def start(tile, s, unroll):
        for k in range(TOP_K):
            _start_items(dest_ref, k * m + tile * tm, tm, nrow, yb_hbm, buf_ref.at[s, k], sems.at[s], unroll)

    def wait(s):
        for k in range(TOP_K):
            _wait_items(tm, nrow, yb_hbm, buf_ref.at[s, k], sems.at[s])

    @pl.when(i == 0)
    def _():
        start(0, 0, False)

    wait(slot)
    start(jnp.minimum(i + 1, last), 1 - slot, True)

    def expert_out(k):
        chunks = [buf_ref[slot, k, pl.ds(s, tm, stride=nrow), :] for s in range(nrow)]
        return jnp.concatenate(chunks, axis=1)

    route = route_ref[...]
    w1 = route[:, ROUTE_W1:ROUTE_W1 + 1]
    w2 = route[:, ROUTE_W2:ROUTE_W2 + 1]
    y = w1 * expert_out(0) + w2 * expert_out(1)
    h = alpha * x1_ref[...] + (1.0 + g2_ref[...]) * y
    o_ref[...] = _layer_norm(h, lng_ref[...], lnb_ref[...])

    @pl.when(i == last)
    def _():
        wait(1 - slot)


def combine_ln(dest_flat, x1, route, g2, ln_g, ln_b, yb, seq, alpha, tm=256):
    M, D = x1.shape
    bpb = seq // tm
    nrow = D // LANES
    grid_spec = pltpu.PrefetchScalarGridSpec(
        num_scalar_prefetch=1,
        grid=(M // tm,),
        in_specs=[
            pl.BlockSpec((tm, D), lambda i, d: (i, 0)),
            pl.BlockSpec((tm, LANES), lambda i, d: (i, 0)),
            pl.BlockSpec((None, 1, D), lambda i, d: (i // bpb, 0, 0)),
            pl.BlockSpec((1, D), lambda i, d: (0, 0)),
            pl.BlockSpec((1, D), lambda i, d: (0, 0)),
            pl.BlockSpec(memory_space=pl.ANY),
        ],
        out_specs=pl.BlockSpec((tm, D), lambda i, d: (i, 0)),
        scratch_shapes=[pltpu.VMEM((2, TOP_K, tm * nrow, LANES), F32), pltpu.SemaphoreType.DMA((2,))],
    )
    return pl.pallas_call(
        functools.partial(_final_kernel, alpha=alpha, tm=tm, m=M, nrow=nrow),
        grid_spec=grid_spec,
        out_shape=jax.ShapeDtypeStruct((M, D), F32),
        compiler_params=_cparams(("arbitrary",)),
        name="final",
    )(dest_flat, x1, route, g2, ln_g.reshape(1, D), ln_b.reshape(1, D), yb)


def moe_layout(route, cnt, tb):
    M = route.shape[0]
    n_assign = M * TOP_K
    counts = cnt[0, :N_EXPERTS].astype(jnp.int32)
    padded = ((counts + tb - 1) // tb) * tb
    pends = jnp.cumsum(padded)
    pstarts = pends - padded
    n_tiles = (n_assign + N_EXPERTS * (tb - 1) + tb - 1) // tb
    e = route[:, ROUTE_E1:ROUTE_E2 + 1].astype(jnp.int32)
    rank = route[:, ROUTE_R1:ROUTE_R2 + 1].astype(jnp.int32)
    dest = pstarts[e] + rank
    dest_flat = dest.T.reshape(-1)
    tok = jnp.tile(jnp.arange(M, dtype=jnp.int32), TOP_K)
    pad_tok = jnp.zeros((n_tiles * tb,), jnp.int32).at[dest_flat].set(tok)
    tile_start = jnp.arange(n_tiles, dtype=jnp.int32) * tb
    block_e = jnp.sum((pends[None, :] <= tile_start[:, None]).astype(jnp.int32), axis=1)
    block_e = jnp.minimum(block_e, N_EXPERTS - 1)
    n_used = (pends[-1] // tb).astype(jnp.int32).reshape(1)
    return dest_flat, pad_tok, block_e, n_used


def _forward(x, c, w_ada, b_ada, w_in, diff_lambda, diff_subln_g, rel_bias, w_o, ln_g, ln_b, w_group, b_group,
             w_router, b_router, w_gate, w_up, w_down, *, diff_tile, sb_tile, moe_tile, ada_tn):
    B, S, D = x.shape
    depth = w_ada.shape[0]
    M = B * S
    alpha = (2 * depth) ** 0.25

    c8 = jnp.zeros((8, D), F32).at[:B].set(c)
    mod = ada_modulation(c8, w_ada, b_ada, ada_tn)[:, :B]
    mod = mod.reshape(depth, B, 6, 1, D)
    bias_tiles = rel_bias_tiles(rel_bias, diff_tile)
    w_in_b, w_o_b = w_in.astype(BF16), w_o.astype(BF16)
    w_gate_b, w_up_b, w_down_b = w_gate.astype(BF16), w_up.astype(BF16), w_down.astype(BF16)

    x2d = x.reshape(M, D)
    for l in range(depth):
        sh1, sc1, g1, sh2, sc2, g2 = (mod[l, :, n] for n in range(6))
        proj = in_projection(x2d, sc1, sh1, w_in_b, l, S)

        lam_init = 0.8 - 0.6 * math.exp(-0.3 * l)
        lp = diff_lambda[l].astype(F32)
        lam = jnp.exp(jnp.sum(lp[0] * lp[1])) - jnp.exp(jnp.sum(lp[2] * lp[3])) + lam_init
        a_out = diff_attention(proj, lam, bias_tiles, diff_subln_g[l], B, S, lam_init, diff_tile)
        b_out = sb_attention(proj, B, S, sb_tile)

        w_rt = jnp.zeros((D, LANES), F32).at[:, :N_GROUPS].set(w_group[l])
        w_rt = w_rt.at[:, N_GROUPS:N_GROUPS + N_EXPERTS].set(w_router[l])
        b_rt = jnp.zeros((1, LANES), F32).at[0, :N_GROUPS].set(b_group[l])
        b_rt = b_rt.at[0, N_GROUPS:N_GROUPS + N_EXPERTS].set(b_router[l])
        x1, u2, route, cnt = mix_ln_route(a_out, b_out, w_o_b, l, x2d, g1, sc2, sh2,
                                          ln_g[l, 0], ln_b[l, 0], w_rt, b_rt, S, alpha)

        dest_flat, pad_tok, block_e, n_used = moe_layout(route, cnt, moe_tile)
        yb = expert_mlp(pad_tok, block_e, n_used, u2, w_gate_b, w_up_b, w_down_b, l, moe_tile)
        x2d = combine_ln(dest_flat, x1, route, g2, ln_g[l, 1], ln_b[l, 1], yb, S, alpha)
    return x2d.reshape(B, S, D)


def kernel(x, c, w_ada, b_ada, w_in, diff_lambda, diff_subln_g, rel_bias, w_o, ln_g, ln_b, w_group, b_group,
           w_router, b_router, w_gate, w_up, w_down):
    return _forward(x, c, w_ada, b_ada, w_in, diff_lambda, diff_subln_g, rel_bias, w_o, ln_g, ln_b, w_group,
                    b_group, w_router, b_router, w_gate, w_up, w_down,
                    diff_tile=512, sb_tile=256, moe_tile=256, ada_tn=1024)
```

```python
import functools
import math

import jax
import jax.numpy as jnp
import numpy as np
from jax import lax
from jax.experimental import pallas as pl
from jax.experimental.pallas import tpu as pltpu

F32 = jnp.float32
BF16 = jnp.bfloat16

N_DIFF_HEADS = 8
N_SB_HEADS = 8
HEAD_DIM = 128
DIFF_QK_DIM = 64
NUM_BUCKETS = 32
MAX_DISTANCE = 128
N_GROUPS = 4
EXPERTS_PER_GROUP = 8
N_EXPERTS = N_GROUPS * EXPERTS_PER_GROUP
TOP_K = 2
LN_EPS = 1e-5
SUBLN_EPS = 1e-5

LANES = 128
MASK_VALUE = -1e30
SB_EXIT = -104.0
DIFF_ROW_CHUNK = 128
SB_HEADS_PER_STEP = 2
HI_HALF = 0xFFFF0000
GATHER_SLOTS = 3
VMEM_LIMIT = 56 * 1024 * 1024


def _cparams(sem):
    return pltpu.CompilerParams(dimension_semantics=sem, vmem_limit_bytes=VMEM_LIMIT)


def _dot(a, b):
    return jnp.dot(a, b, preferred_element_type=F32)


def _dot_nt(a, b):
    return lax.dot_general(a, b, (((1,), (1,)), ((), ())), preferred_element_type=F32)


def _split(a):
    hi = a.astype(BF16)
    lo = (a - hi.astype(F32)).astype(BF16)
    return hi, lo


def _dot3(a, b):
    a_hi, a_lo = _split(a)
    b_hi, b_lo = _split(b)
    return _dot(a_hi, b_hi) + (_dot(a_hi, b_lo) + _dot(a_lo, b_hi))


def _layer_norm(h, g, b):
    mu = jnp.mean(h, axis=-1, keepdims=True)
    d = h - mu
    var = jnp.mean(d * d, axis=-1, keepdims=True)
    return d * lax.rsqrt(var + LN_EPS) * g + b


def _ada_kernel(c_ref, w_ref, b_ref, o_ref):
    c = c_ref[...]
    s = c * jax.nn.sigmoid(c)
    o_ref[...] = _dot3(s, w_ref[...]) + b_ref[...]


def ada_modulation(c8, w_ada, b_ada, tn=1024):
    L, D, N = w_ada.shape
    return pl.pallas_call(
        _ada_kernel,
        grid=(L, N // tn),
        in_specs=[
            pl.BlockSpec((8, D), lambda l, j: (0, 0)),
            pl.BlockSpec((None, D, tn), lambda l, j: (l, 0, j)),
            pl.BlockSpec((None, 1, tn), lambda l, j: (l, 0, j)),
        ],
        out_specs=pl.BlockSpec((None, 8, tn), lambda l, j: (l, 0, j)),
        out_shape=jax.ShapeDtypeStruct((L, 8, N), F32),
        compiler_params=_cparams(("arbitrary", "arbitrary")),
        name="ada",
    )(c8, w_ada, b_ada.reshape(L, 1, N))


def _inproj_kernel(x_ref, sc_ref, sh_ref, w_ref, o_ref, u_ref):
    @pl.when(pl.program_id(1) == 0)
    def _():
        u_ref[...] = (x_ref[...] * (1.0 + sc_ref[...]) + sh_ref[...]).astype(BF16)

    o_ref[...] = _dot(u_ref[...], w_ref[...]).astype(o_ref.dtype)


def in_projection(x2d, sc, sh, w_bf16, layer, seq, tm=1024, tn=1536):
    M, D = x2d.shape
    N = w_bf16.shape[2]
    tm = min(tm, seq)
    tn = min(tn, N)
    bpb = seq // tm
    return pl.pallas_call(
        _inproj_kernel,
        grid=(M // tm, N // tn),
        in_specs=[
            pl.BlockSpec((tm, D), lambda i, j: (i, 0)),
            pl.BlockSpec((None, 1, D), lambda i, j: (i // bpb, 0, 0)),
            pl.BlockSpec((None, 1, D), lambda i, j: (i // bpb, 0, 0)),
            pl.BlockSpec((None, D, tn), lambda i, j: (layer, 0, j)),
        ],
        out_specs=pl.BlockSpec((tm, tn), lambda i, j: (i, j)),
        out_shape=jax.ShapeDtypeStruct((M, N), BF16),
        scratch_shapes=[pltpu.VMEM((tm, D), BF16)],
        compiler_params=_cparams(("arbitrary", "arbitrary")),
        name="inproj",
    )(x2d, sc, sh, w_bf16)


def _t5_bucket_np(rel):
    n = np.maximum(rel, 0)
    max_exact = NUM_BUCKETS // 2
    ratio = np.maximum(n, 1).astype(np.float32) / np.float32(max_exact)
    large = max_exact + (np.log(ratio).astype(np.float32) / np.float32(math.log(MAX_DISTANCE / max_exact))
                         * np.float32(NUM_BUCKETS - max_exact)).astype(np.int32)
    large = np.minimum(large, NUM_BUCKETS - 1)
    return np.where(n < max_exact, n, large)


def _bucket_thresholds():
    buckets = _t5_bucket_np(np.arange(2 * MAX_DISTANCE))
    assert np.all(np.diff(buckets) >= 0) and buckets[MAX_DISTANCE] == NUM_BUCKETS - 1
    return [int(np.argmax(buckets >= k)) for k in range(NUM_BUCKETS)]


def _relbias_kernel(tab_ref, o_ref, *, t, n_heads, thresholds):
    h = pl.program_id(0)
    row = lax.broadcasted_iota(jnp.int32, (t, t), 0)
    col = lax.broadcasted_iota(jnp.int32, (t, t), 1)
    for blk in range(2):
        rel = row - col + blk * t
        for mp in range(2):
            entry = lambda k: tab_ref[(k * n_heads + h) * 2 + mp]
            far = entry(NUM_BUCKETS - 1)
            val = jnp.full((t, t), entry(0) - far, F32)
            for k in range(1, NUM_BUCKETS):
                val = jnp.where(rel >= thresholds[k], entry(k) - far, val)
            if blk == 0:
                val = jnp.where(rel < 0, MASK_VALUE, val)
            o_ref[blk, mp * t:(mp + 1) * t, :] = val


def rel_bias_tiles(rel_bias, t):
    assert t >= MAX_DISTANCE
    n_heads = rel_bias.shape[1]
    kern = functools.partial(_relbias_kernel, t=t, n_heads=n_heads, thresholds=_bucket_thresholds())
    grid_spec = pltpu.PrefetchScalarGridSpec(
        num_scalar_prefetch=1,
        grid=(n_heads,),
        in_specs=[],
        out_specs=pl.BlockSpec((None, 2, 2 * t, t), lambda h, tab: (h, 0, 0, 0)),
    )
    return pl.pallas_call(
        kern,
        grid_spec=grid_spec,
        out_shape=jax.ShapeDtypeStruct((n_heads, 2, 2 * t, t), F32),
        compiler_params=_cparams(("arbitrary",)),
        name="relbias",
    )(rel_bias.reshape(-1).astype(F32))


def _diff_kernel(lam_ref, q_ref, k_ref, v_ref, bias_ref, g_ref, o_ref, qs_ref, s_ref, m_ref, acc_ref,
                 *, t, rc, out_scale):
    qi = pl.program_id(2)
    q = q_ref[...] * BF16(DIFF_QK_DIM ** -0.5)
    lane = lax.broadcasted_iota(jnp.int32, q.shape, 1)
    zero = jnp.zeros_like(q)
    qs_ref[0:t, :] = jnp.where(lane < DIFF_QK_DIM, q, zero)
    qs_ref[t:2 * t, :] = jnp.where(lane >= DIFF_QK_DIM, q, zero)
    m_ref[...] = jnp.full(m_ref.shape, MASK_VALUE, F32)
    acc_ref[...] = jnp.zeros(acc_ref.shape, F32)
    ones = jnp.ones((t, HEAD_DIM), BF16)

    def keys(j):
        return pl.ds(pl.multiple_of(j * t, t), t)

    def logits(j):
        kb = k_ref[keys(j), :]
        for r in range(2 * t // rc):
            rows = pl.ds(r * rc, rc)
            s_ref[rows, :] = _dot_nt(qs_ref[rows, :], kb)

    def step(j, bias_blk, produce_next):
        vb = v_ref[keys(j), :]
        vb1 = jnp.concatenate([vb, ones], axis=1)
        if produce_next:
            kb_next = k_ref[keys(j + 1), :]
        diagonal = bias_blk == 0
        for r in range(2 * t // rc):
            rows = pl.ds(r * rc, rc)
            nc = (r % (t // rc) + 1) * rc if diagonal else t
            s = s_ref[rows, :nc]
            if produce_next:
                s_ref[rows, :] = _dot_nt(qs_ref[rows, :], kb_next)
            if bias_blk is not None:
                s = s + bias_ref[bias_blk, rows, :nc]
            m_old = m_ref[rows, :]
            m_new = jnp.maximum(m_old, jnp.max(s, axis=1, keepdims=True))
            p = jnp.exp(s - jnp.concatenate([m_new] * (nc // LANES), axis=1)).astype(BF16)
            alpha = jnp.exp(m_old - m_new)
            acc_ref[rows, :] = jnp.concatenate([alpha, alpha], axis=1) * acc_ref[rows, :] + _dot(p, vb1[:nc])
            m_ref[rows, :] = m_new

    logits(0)

    def far_body(j, carry):
        step(j, None, True)
        return carry

    lax.fori_loop(0, jnp.maximum(qi - 1, 0), far_body, 0)

    @pl.when(qi >= 1)
    def _():
        step(qi - 1, 1, True)

    step(qi, 0, False)

    acc = acc_ref[...]
    r = acc[:, :HEAD_DIM] / acc[:, HEAD_DIM:]
    o = r[:t] - lam_ref[0] * r[t:]
    o = o * lax.rsqrt(jnp.mean(o * o, axis=-1, keepdims=True) + SUBLN_EPS)
    o_ref[...] = (o * g_ref[...] * out_scale).astype(o_ref.dtype)


def diff_attention(proj, lam, bias_tiles, subln_g, batch, seq, lam_init, t):
    M = proj.shape[0]
    H = N_DIFF_HEADS
    nq = seq // t
    kern = functools.partial(_diff_kernel, t=t, rc=min(DIFF_ROW_CHUNK, t), out_scale=1.0 - lam_init)
    grid_spec = pltpu.PrefetchScalarGridSpec(
        num_scalar_prefetch=1,
        grid=(H, batch, nq),
        in_specs=[
            pl.BlockSpec((t, HEAD_DIM), lambda h, b, qi, s: (b * nq + qi, h)),
            pl.BlockSpec((seq, HEAD_DIM), lambda h, b, qi, s: (b, H + h)),
            pl.BlockSpec((seq, HEAD_DIM), lambda h, b, qi, s: (b, 2 * H + h)),
            pl.BlockSpec((None, 2, 2 * t, t), lambda h, b, qi, s: (h, 0, 0, 0)),
            pl.BlockSpec((1, HEAD_DIM), lambda h, b, qi, s: (0, 0)),
        ],
        out_specs=pl.BlockSpec((t, HEAD_DIM), lambda h, b, qi, s: (b * nq + qi, h)),
        scratch_shapes=[
            pltpu.VMEM((2 * t, HEAD_DIM), BF16),
            pltpu.VMEM((2 * t, t), F32),
            pltpu.VMEM((2 * t, LANES), F32),
            pltpu.VMEM((2 * t, 2 * HEAD_DIM), F32),
        ],
    )
    return pl.pallas_call(
        kern,
        grid_spec=grid_spec,
        out_shape=jax.ShapeDtypeStruct((M, H * HEAD_DIM), BF16),
        compiler_params=_cparams(("arbitrary", "arbitrary", "arbitrary")),
        name="diffattn",
    )(lam.reshape(1).astype(F32), proj, proj, proj, bias_tiles, subln_g.reshape(1, HEAD_DIM))


def _sb_kernel(q_ref, k_ref, v_ref, o_ref, carry_ref, acc_ref, *, t, hp, scale):
    qi = pl.program_id(2)

    def keys(j):
        return pl.ds(pl.multiple_of(j * t, t), t)

    def upper_ones(n):
        row = lax.broadcasted_iota(jnp.int32, (n, n), 0)
        col = lax.broadcasted_iota(jnp.int32, (n, n), 1)
        return jnp.where(row > col, 1.0, 0.0).astype(BF16)

    def sticks(z, valid):
        lp = jnp.log(1.0 + jnp.exp(-jnp.abs(z)))
        log_keep = jnp.minimum(-z, 0.0) - lp
        log_beta = jnp.minimum(z, 0.0) - lp
        if valid is not None:
            log_keep = jnp.where(valid, log_keep, 0.0)
        return log_keep, log_beta

    prev = jnp.maximum(qi - 1, 0)
    qrow = lax.broadcasted_iota(jnp.int32, (t, 2 * t), 0)
    kcol = lax.broadcasted_iota(jnp.int32, (t, 2 * t), 1)
    n_prev = jnp.where(qi > 0, t, 0)
    valid = (kcol < n_prev) | ((kcol >= t) & (kcol - t < qrow))
    upper2 = upper_ones(2 * t)
    for hh in range(hp):
        cols = slice(hh * HEAD_DIM, (hh + 1) * HEAD_DIM)
        kw = jnp.concatenate([k_ref[keys(prev), cols], k_ref[keys(qi), cols]], axis=0)
        vw = jnp.concatenate([v_ref[keys(prev), cols], v_ref[keys(qi), cols]], axis=0)
        log_keep, log_beta = sticks(_dot_nt(q_ref[:, cols], kw) * scale, valid)
        hi, lo = _split(log_keep)
        suffix = _dot(hi, upper2) + _dot(lo, upper2)
        a = jnp.where(valid, jnp.exp(log_beta + suffix), 0.0)
        acc_ref[:, cols] = _dot(a.astype(BF16), vw)
        carry_ref[hh] = jnp.broadcast_to(jnp.sum(log_keep, axis=1, keepdims=True), (t, LANES))

    upper = upper_ones(t)

    def step(j):
        for hh in range(hp):
            cols = slice(hh * HEAD_DIM, (hh + 1) * HEAD_DIM)
            log_keep, log_beta = sticks(_dot_nt(q_ref[:, cols], k_ref[keys(j), cols]) * scale, None)
            hi, lo = _split(log_keep)
            suffix = _dot(hi, upper) + _dot(lo, upper)
            carry = carry_ref[hh]
            a = jnp.exp(log_beta + suffix + jnp.concatenate([carry] * (t // LANES), axis=1))
            acc_ref[:, cols] += _dot(a.astype(BF16), v_ref[keys(j), cols])
            carry_ref[hh] = carry + jnp.sum(log_keep, axis=1, keepdims=True)

    def cond(state):
        n, live = state
        return jnp.logical_and(n < qi - 1, live > SB_EXIT)

    def body(state):
        n, _ = state
        step(qi - 2 - n)
        return n + 1, jnp.max(carry_ref[...])

    lax.while_loop(cond, body, (jnp.int32(0), jnp.max(carry_ref[...])))
    o_ref[...] = acc_ref[...].astype(o_ref.dtype)


def sb_attention(proj, batch, seq, t, hp=SB_HEADS_PER_STEP):
    M = proj.shape[0]
    H = N_SB_HEADS
    base = 3 * N_DIFF_HEADS // hp
    nq = seq // t
    w = hp * HEAD_DIM
    kern = functools.partial(_sb_kernel, t=t, hp=hp, scale=HEAD_DIM ** -0.5)
    return pl.pallas_call(
        kern,
        grid=(H // hp, batch, nq),
        in_specs=[
            pl.BlockSpec((t, w), lambda h, b, qi: (b * nq + qi, base + h)),
            pl.BlockSpec((seq, w), lambda h, b, qi: (b, base + H // hp + h)),
            pl.BlockSpec((seq, w), lambda h, b, qi: (b, base + 2 * (H // hp) + h)),
        ],
        out_specs=pl.BlockSpec((t, w), lambda h, b, qi: (b * nq + qi, h)),
        out_shape=jax.ShapeDtypeStruct((M, H * HEAD_DIM), BF16),
        scratch_shapes=[pltpu.VMEM((hp, t, LANES), F32), pltpu.VMEM((t, w), F32)],
        compiler_params=_cparams(("arbitrary", "arbitrary", "arbitrary")),
        name="sbattn",
    )(proj, proj, proj)


ROUTE_E1, ROUTE_E2, ROUTE_W1, ROUTE_W2, ROUTE_R1, ROUTE_R2 = range(6)


def _mix_kernel(a_ref, b_ref, wo_ref, x_ref, g1_ref, sc2_ref, sh2_ref, lng_ref, lnb_ref, wrt_ref, brt_ref,
                x1_ref, u2_ref, route_ref, cnt_ref, carry_ref, *, alpha, half):
    i = pl.program_id(0)

    @pl.when(i == 0)
    def _():
        carry_ref[...] = jnp.zeros(carry_ref.shape, F32)

    mix = _dot(a_ref[...], wo_ref[pl.ds(0, half), :]) + _dot(b_ref[...], wo_ref[pl.ds(half, half), :])
    h = alpha * x_ref[...] + (1.0 + g1_ref[...]) * mix
    x1 = _layer_norm(h, lng_ref[...], lnb_ref[...])
    x1_ref[...] = x1
    u2 = x1 * (1.0 + sc2_ref[...]) + sh2_ref[...]
    nseg = u2.shape[1] // (2 * LANES)
    for s in range(nseg):
        lo = u2[:, 2 * s * LANES:(2 * s + 1) * LANES].astype(BF16).astype(F32)
        hi = u2[:, (2 * s + 1) * LANES:(2 * s + 2) * LANES].astype(BF16).astype(F32)
        lo_bits = lax.shift_right_logical(lax.bitcast_convert_type(lo, jnp.uint32), jnp.uint32(16))
        hi_bits = lax.bitcast_convert_type(hi, jnp.uint32) & jnp.uint32(HI_HALF)
        u2_ref[pl.ds(s, u2.shape[0], stride=nseg), :] = lo_bits | hi_bits

    logits = _dot3(u2, wrt_ref[...]) + brt_ref[...]
    tm = logits.shape[0]
    lane = lax.broadcasted_iota(jnp.int32, (tm, LANES), 1).astype(F32)
    gmask = lane < N_GROUPS
    gl = jnp.where(gmask, logits, MASK_VALUE)
    gmax = jnp.max(gl, axis=1, keepdims=True)
    gidx = jnp.min(jnp.where(gmask & (gl == gmax), lane, float(LANES)), axis=1, keepdims=True)
    gsum = jnp.sum(jnp.where(gmask, jnp.exp(gl - gmax), 0.0), axis=1, keepdims=True)
    gp = 1.0 / gsum

    lo = N_GROUPS + gidx * EXPERTS_PER_GROUP
    emask = (lane >= lo) & (lane < lo + EXPERTS_PER_GROUP)
    el = jnp.where(emask, logits, MASK_VALUE)
    v1 = jnp.max(el, axis=1, keepdims=True)
    i1 = jnp.min(jnp.where(emask & (el == v1), lane, float(LANES)), axis=1, keepdims=True)
    emask2 = emask & (lane != i1)
    el2 = jnp.where(emask2, logits, MASK_VALUE)
    v2 = jnp.max(el2, axis=1, keepdims=True)
    i2 = jnp.min(jnp.where(emask2 & (el2 == v2), lane, float(LANES)), axis=1, keepdims=True)
    tt = jnp.exp(v2 - v1)
    w1 = gp / (1.0 + tt)
    w2 = gp * tt / (1.0 + tt)
    e1 = i1 - N_GROUPS
    e2 = i2 - N_GROUPS

    oh1 = lane == e1
    oh2 = lane == e2
    oh1b = jnp.where(oh1, 1.0, 0.0).astype(BF16)
    oh2b = jnp.where(oh2, 1.0, 0.0).astype(BF16)
    row = lax.broadcasted_iota(jnp.int32, (tm, tm), 0)
    col = lax.broadcasted_iota(jnp.int32, (tm, tm), 1)
    lower = jnp.where(col < row, 1.0, 0.0).astype(BF16)
    c1 = jnp.sum(jnp.where(oh1, 1.0, 0.0), axis=0, keepdims=True)
    c2 = jnp.sum(jnp.where(oh2, 1.0, 0.0), axis=0, keepdims=True)
    carry = carry_ref[...]
    r1 = _dot(lower, oh1b) + carry
    r2 = _dot(lower, oh2b) + (carry + c1)
    rank1 = jnp.sum(jnp.where(oh1, r1, 0.0), axis=1, keepdims=True)
    rank2 = jnp.sum(jnp.where(oh2, r2, 0.0), axis=1, keepdims=True)
    carry = carry + c1 + c2
    carry_ref[...] = carry
    cnt_ref[...] = jnp.broadcast_to(carry, cnt_ref.shape)

    out = jnp.zeros((tm, LANES), F32)
    for ln, val in ((ROUTE_E1, e1), (ROUTE_E2, e2), (ROUTE_W1, w1), (ROUTE_W2, w2),
                    (ROUTE_R1, rank1), (ROUTE_R2, rank2)):
        out = jnp.where(lane == float(ln), val, out)
    route_ref[...] = out


def mix_ln_route(a_out, b_out, wo_bf16, layer, x2d, g1, sc2, sh2, ln_g, ln_b, w_rt, b_rt, seq, alpha, tm=256):
    M, D = x2d.shape
    half = a_out.shape[1]
    bpb = seq // tm
    nseg = D // (2 * LANES)
    row = lambda i: (i // bpb, 0, 0)
    full = lambda i: (0, 0)
    kern = functools.partial(_mix_kernel, alpha=alpha, half=half)
    return pl.pallas_call(
        kern,
        grid=(M // tm,),
        in_specs=[
            pl.BlockSpec((tm, half), lambda i: (i, 0)),
            pl.BlockSpec((tm, half), lambda i: (i, 0)),
            pl.BlockSpec((None, 2 * half, D), lambda i: (layer, 0, 0)),
            pl.BlockSpec((tm, D), lambda i: (i, 0)),
            pl.BlockSpec((None, 1, D), row),
            pl.BlockSpec((None, 1, D), row),
            pl.BlockSpec((None, 1, D), row),
            pl.BlockSpec((1, D), full),
            pl.BlockSpec((1, D), full),
            pl.BlockSpec((D, LANES), full),
            pl.BlockSpec((1, LANES), full),
        ],
        out_specs=[
            pl.BlockSpec((tm, D), lambda i: (i, 0)),
            pl.BlockSpec((tm * nseg, LANES), lambda i: (i, 0)),
            pl.BlockSpec((tm, LANES), lambda i: (i, 0)),
            pl.BlockSpec((8, LANES), full),
        ],
        out_shape=[
            jax.ShapeDtypeStruct((M, D), F32),
            jax.ShapeDtypeStruct((M * nseg, LANES), jnp.uint32),
            jax.ShapeDtypeStruct((M, LANES), F32),
            jax.ShapeDtypeStruct((8, LANES), F32),
        ],
        scratch_shapes=[pltpu.VMEM((1, LANES), F32)],
        compiler_params=_cparams(("arbitrary",)),
        name="mix",
    )(a_out, b_out, wo_bf16, x2d, g1, sc2, sh2, ln_g.reshape(1, D), ln_b.reshape(1, D), w_rt, b_rt)


def _item_copy(src_hbm, dst_vmem, src_item, dst_item, rpi, sem):
    src = src_hbm.at[pl.ds(pl.multiple_of(src_item * rpi, rpi), rpi), :]
    dst = dst_vmem.at[pl.ds(pl.multiple_of(dst_item * rpi, rpi), rpi), :]
    return pltpu.make_async_copy(src, dst, sem)


def _start_items(idx_ref, base, n_items, rpi, src_hbm, dst_vmem, sem, unroll=False):
    def issue(r, c):
        _item_copy(src_hbm, dst_vmem, idx_ref[base + r], r, rpi, sem).start()
        return c

    lax.fori_loop(0, n_items, issue, 0, unroll=unroll)


def _wait_items(n_items, rpi, src_hbm, dst_vmem, sem):
    pltpu.make_async_copy(src_hbm.at[pl.ds(0, n_items * rpi), :], dst_vmem, sem).wait()


def _expert_kernel(tok_ref, be_ref, nused_ref, u_hbm, wg_ref, wu_ref, wd_ref, o_ref, xbuf_ref, xs_ref, sems,
                   *, tb, nseg, nrow):
    i = pl.program_id(0)
    n_used = nused_ref[0]
    slot = i % GATHER_SLOTS

    def start(k, unroll):
        s = k % GATHER_SLOTS
        tile = jnp.minimum(k, n_used - 1)
        _start_items(tok_ref, tile * tb, tb, nseg, u_hbm, xbuf_ref.at[s], sems.at[s], unroll)

    def wait(k):
        s = k % GATHER_SLOTS
        _wait_items(tb, nseg, u_hbm, xbuf_ref.at[s], sems.at[s])

    @pl.when(i == 0)
    def _():
        for k in range(GATHER_SLOTS - 1):
            start(k, False)

    @pl.when(i < n_used)
    def _():
        wait(i)
        start(i + GATHER_SLOTS - 1, True)
        for s in range(nseg):
            w = xbuf_ref[slot, pl.ds(s, tb, stride=nseg), :]
            lo = lax.bitcast_convert_type(lax.shift_left(w, jnp.uint32(16)), F32)
            hi = lax.bitcast_convert_type(w & jnp.uint32(HI_HALF), F32)
            xs_ref[:, 2 * s * LANES:(2 * s + 1) * LANES] = lo.astype(BF16)
            xs_ref[:, (2 * s + 1) * LANES:(2 * s + 2) * LANES] = hi.astype(BF16)
        x = xs_ref[...]
        g = _dot(x, wg_ref[...])
        u = _dot(x, wu_ref[...])
        h = (g * jax.nn.sigmoid(g) * u).astype(BF16)
        y = _dot(h, wd_ref[...])
        for s in range(nrow):
            o_ref[pl.ds(s, tb, stride=nrow), :] = y[:, s * LANES:(s + 1) * LANES]

    @pl.when(i == n_used - 1)
    def _():
        for k in range(1, GATHER_SLOTS):
            wait(i + k)

    @pl.when(i >= n_used)
    def _():
        o_ref[...] = jnp.zeros(o_ref.shape, o_ref.dtype)


def expert_mlp(pad_tok, block_e, n_used, u2p, wg, wu, wd, layer, tb):
    P = pad_tok.shape[0]
    D = wg.shape[2]
    F = wg.shape[3]
    nseg = D // (2 * LANES)
    nrow = D // LANES

    def wmap(i, tok, be, nu):
        return (layer, be[jnp.minimum(i, jnp.maximum(nu[0] - 1, 0))], 0, 0)

    grid_spec = pltpu.PrefetchScalarGridSpec(
        num_scalar_prefetch=3,
        grid=(P // tb,),
        in_specs=[
            pl.BlockSpec(memory_space=pl.ANY),
            pl.BlockSpec((None, None, D, F), wmap),
            pl.BlockSpec((None, None, D, F), wmap),
            pl.BlockSpec((None, None, F, D), wmap),
        ],
        out_specs=pl.BlockSpec((tb * nrow, LANES), lambda i, tok, be, nu: (i, 0)),
        scratch_shapes=[pltpu.VMEM((GATHER_SLOTS, tb * nseg, LANES), jnp.uint32), pltpu.VMEM((tb, D), BF16),
                        pltpu.SemaphoreType.DMA((GATHER_SLOTS,))],
    )
    return pl.pallas_call(
        functools.partial(_expert_kernel, tb=tb, nseg=nseg, nrow=nrow),
        grid_spec=grid_spec,
        out_shape=jax.ShapeDtypeStruct((P * nrow, LANES), F32),
        compiler_params=_cparams(("arbitrary",)),
        name="expert",
    )(pad_tok, block_e, n_used, u2p, wg, wu, wd)


def _final_kernel(dest_ref, x1_ref, route_ref, g2_ref, lng_ref, lnb_ref, yb_hbm, o_ref, buf_ref, sems,
                  *, alpha, tm, m, nrow):
    i = pl.program_id(0)
    slot = i % GATHER_SLOTS
    last = pl.num_programs(0) - 1

    def start(n, unroll):
        s = n % GATHER_SLOTS
        tile = jnp.minimum(n, last)
        for k in range(TOP_K):
            _start_items(dest_ref, k * m + tile * tm, tm, nrow, yb_hbm, buf_ref.at[s, k], sems.at[s], unroll)

    def wait(n):
        s = n % GATHER_SLOTS
        for k in range(TOP_K):
            _wait_items(tm, nrow, yb_hbm, buf_ref.at[s, k], sems.at[s])

    @pl.when(i == 0)
    def _():
        for n in range(GATHER_SLOTS - 1):
            start(n, False)

    wait(i)
    start(i + GATHER_SLOTS - 1, True)

    def expert_out(k):
        chunks = [buf_ref[slot, k, pl.ds(s, tm, stride=nrow), :] for s in range(nrow)]
        return jnp.concatenate(chunks, axis=1)

    route = route_ref[...]
    w1 = route[:, ROUTE_W1:ROUTE_W1 + 1]
    w2 = route[:, ROUTE_W2:ROUTE_W2 + 1]
    y = w1 * expert_out(0) + w2 * expert_out(1)
    h = alpha * x1_ref[...] + (1.0 + g2_ref[...]) * y
    o_ref[...] = _layer_norm(h, lng_ref[...], lnb_ref[...])

    @pl.when(i == last)
    def _():
        for n in range(1, GATHER_SLOTS):
            wait(i + n)


def combine_ln(dest_flat, x1, route, g2, ln_g, ln_b, yb, seq, alpha, tm=256):
    M, D = x1.shape
    bpb = seq // tm
    nrow = D // LANES
    grid_spec = pltpu.PrefetchScalarGridSpec(
        num_scalar_prefetch=1,
        grid=(M // tm,),
        in_specs=[
            pl.BlockSpec((tm, D), lambda i, d: (i, 0)),
            pl.BlockSpec((tm, LANES), lambda i, d: (i, 0)),
            pl.BlockSpec((None, 1, D), lambda i, d: (i // bpb, 0, 0)),
            pl.BlockSpec((1, D), lambda i, d: (0, 0)),
            pl.BlockSpec((1, D), lambda i, d: (0, 0)),
            pl.BlockSpec(memory_space=pl.ANY),
        ],
        out_specs=pl.BlockSpec((tm, D), lambda i, d: (i, 0)),
        scratch_shapes=[pltpu.VMEM((GATHER_SLOTS, TOP_K, tm * nrow, LANES), F32),
                        pltpu.SemaphoreType.DMA((GATHER_SLOTS,))],
    )
    return pl.pallas_call(
        functools.partial(_final_kernel, alpha=alpha, tm=tm, m=M, nrow=nrow),
        grid_spec=grid_spec,
        out_shape=jax.ShapeDtypeStruct((M, D), F32),
        compiler_params=_cparams(("arbitrary",)),
        name="final",
    )(dest_flat, x1, route, g2, ln_g.reshape(1, D), ln_b.reshape(1, D), yb)


def moe_layout(route, cnt, tb):
    M = route.shape[0]
    n_assign = M * TOP_K
    counts = cnt[0, :N_EXPERTS].astype(jnp.int32)
    padded = ((counts + tb - 1) // tb) * tb
    pends = jnp.cumsum(padded)
    pstarts = pends - padded
    n_tiles = (n_assign + N_EXPERTS * (tb - 1) + tb - 1) // tb
    e = route[:, ROUTE_E1:ROUTE_E2 + 1].astype(jnp.int32)
    rank = route[:, ROUTE_R1:ROUTE_R2 + 1].astype(jnp.int32)
    dest = pstarts[e] + rank
    dest_flat = dest.T.reshape(-1)
    tok = jnp.tile(jnp.arange(M, dtype=jnp.int32), TOP_K)
    pad_tok = jnp.zeros((n_tiles * tb,), jnp.int32).at[dest_flat].set(tok)
    tile_start = jnp.arange(n_tiles, dtype=jnp.int32) * tb
    block_e = jnp.sum((pends[None, :] <= tile_start[:, None]).astype(jnp.int32), axis=1)
    block_e = jnp.minimum(block_e, N_EXPERTS - 1)
    n_used = (pends[-1] // tb).astype(jnp.int32).reshape(1)
    return dest_flat, pad_tok, block_e, n_used


def _forward(x, c, w_ada, b_ada, w_in, diff_lambda, diff_subln_g, rel_bias, w_o, ln_g, ln_b, w_group, b_group,
             w_router, b_router, w_gate, w_up, w_down, *, diff_tile, sb_tile, moe_tile, ada_tn):
    B, S, D = x.shape
    depth = w_ada.shape[0]
    M = B * S
    alpha = (2 * depth) ** 0.25

    c8 = jnp.zeros((8, D), F32).at[:B].set(c)
    mod = ada_modulation(c8, w_ada, b_ada, ada_tn)[:, :B]
    mod = mod.reshape(depth, B, 6, 1, D)
    bias_tiles = rel_bias_tiles(rel_bias, diff_tile)
    w_in_b, w_o_b = w_in.astype(BF16), w_o.astype(BF16)
    w_gate_b, w_up_b, w_down_b = w_gate.astype(BF16), w_up.astype(BF16), w_down.astype(BF16)

    x2d = x.reshape(M, D)
    for l in range(depth):
        sh1, sc1, g1, sh2, sc2, g2 = (mod[l, :, n] for n in range(6))
        proj = in_projection(x2d, sc1, sh1, w_in_b, l, S)

        lam_init = 0.8 - 0.6 * math.exp(-0.3 * l)
        lp = diff_lambda[l].astype(F32)
        lam = jnp.exp(jnp.sum(lp[0] * lp[1])) - jnp.exp(jnp.sum(lp[2] * lp[3])) + lam_init
        a_out = diff_attention(proj, lam, bias_tiles, diff_subln_g[l], B, S, lam_init, diff_tile)
        b_out = sb_attention(proj, B, S, sb_tile)

        w_rt = jnp.zeros((D, LANES), F32).at[:, :N_GROUPS].set(w_group[l])
        w_rt = w_rt.at[:, N_GROUPS:N_GROUPS + N_EXPERTS].set(w_router[l])
        b_rt = jnp.zeros((1, LANES), F32).at[0, :N_GROUPS].set(b_group[l])
        b_rt = b_rt.at[0, N_GROUPS:N_GROUPS + N_EXPERTS].set(b_router[l])
        x1, u2, route, cnt = mix_ln_route(a_out, b_out, w_o_b, l, x2d, g1, sc2, sh2,
                                          ln_g[l, 0], ln_b[l, 0], w_rt, b_rt, S, alpha)

        dest_flat, pad_tok, block_e, n_used = moe_layout(route, cnt, moe_tile)
        yb = expert_mlp(pad_tok, block_e, n_used, u2, w_gate_b, w_up_b, w_down_b, l, moe_tile)
        x2d = combine_ln(dest_flat, x1, route, g2, ln_g[l, 1], ln_b[l, 1], yb, S, alpha)
    return x2d.reshape(B, S, D)


def kernel(x, c, w_ada, b_ada, w_in, diff_lambda, diff_subln_g, rel_bias, w_o, ln_g, ln_b, w_group, b_group,
           w_router, b_router, w_gate, w_up, w_down):
    return _forward(x, c, w_ada, b_ada, w_in, diff_lambda, diff_subln_g, rel_bias, w_o, ln_g, ln_b, w_group,
                    b_group, w_router, b_router, w_gate, w_up, w_down,
                    diff_tile=512, sb_tile=256, moe_tile=256, ada_tn=1024)
```

```python
import functools
import math

import jax
import jax.numpy as jnp
import numpy as np
from jax import lax
from jax.experimental import pallas as pl
from jax.experimental.pallas import tpu as pltpu

F32 = jnp.float32
BF16 = jnp.bfloat16

N_DIFF_HEADS = 8
N_SB_HEADS = 8
HEAD_DIM = 128
DIFF_QK_DIM = 64
NUM_BUCKETS = 32
MAX_DISTANCE = 128
N_GROUPS = 4
EXPERTS_PER_GROUP = 8
N_EXPERTS = N_GROUPS * EXPERTS_PER_GROUP
TOP_K = 2
LN_EPS = 1e-5
SUBLN_EPS = 1e-5

LANES = 128
MASK_VALUE = -1e30
SB_EXIT = -104.0
DIFF_ROW_CHUNK = 128
SB_HEADS_PER_STEP = 2
HI_HALF = 0xFFFF0000
GATHER_SLOTS = 3
VMEM_LIMIT = 56 * 1024 * 1024


def _cparams(sem):
    return pltpu.CompilerParams(dimension_semantics=sem, vmem_limit_bytes=VMEM_LIMIT)


def _dot(a, b):
    return jnp.dot(a, b, preferred_element_type=F32)


def _dot_nt(a, b):
    return lax.dot_general(a, b, (((1,), (1,)), ((), ())), preferred_element_type=F32)


def _split(a):
    hi = a.astype(BF16)
    lo = (a - hi.astype(F32)).astype(BF16)
    return hi, lo


def _dot3(a, b):
    a_hi, a_lo = _split(a)
    b_hi, b_lo = _split(b)
    return _dot(a_hi, b_hi) + (_dot(a_hi, b_lo) + _dot(a_lo, b_hi))


def _layer_norm(h, g, b):
    mu = jnp.mean(h, axis=-1, keepdims=True)
    d = h - mu
    var = jnp.mean(d * d, axis=-1, keepdims=True)
    return d * lax.rsqrt(var + LN_EPS) * g + b


def _ada_kernel(c_ref, w_ref, b_ref, o_ref):
    c = c_ref[...]
    s = c * jax.nn.sigmoid(c)
    o_ref[...] = _dot3(s, w_ref[...]) + b_ref[...]


def ada_modulation(c8, w_ada, b_ada, tn=1024):
    L, D, N = w_ada.shape
    return pl.pallas_call(
        _ada_kernel,
        grid=(L, N // tn),
        in_specs=[
            pl.BlockSpec((8, D), lambda l, j: (0, 0)),
            pl.BlockSpec((None, D, tn), lambda l, j: (l, 0, j)),
            pl.BlockSpec((None, 1, tn), lambda l, j: (l, 0, j)),
        ],
        out_specs=pl.BlockSpec((None, 8, tn), lambda l, j: (l, 0, j)),
        out_shape=jax.ShapeDtypeStruct((L, 8, N), F32),
        compiler_params=_cparams(("arbitrary", "arbitrary")),
        name="ada",
    )(c8, w_ada, b_ada.reshape(L, 1, N))


def _inproj_kernel(x_ref, sc_ref, sh_ref, w_ref, o_ref, u_ref):
    @pl.when(pl.program_id(1) == 0)
    def _():
        u_ref[...] = (x_ref[...] * (1.0 + sc_ref[...]) + sh_ref[...]).astype(BF16)

    o_ref[...] = _dot(u_ref[...], w_ref[...]).astype(o_ref.dtype)


def in_projection(x2d, sc, sh, w_bf16, layer, seq, tm=1024, tn=1536):
    M, D = x2d.shape
    N = w_bf16.shape[2]
    tm = min(tm, seq)
    tn = min(tn, N)
    bpb = seq // tm
    return pl.pallas_call(
        _inproj_kernel,
        grid=(M // tm, N // tn),
        in_specs=[
            pl.BlockSpec((tm, D), lambda i, j: (i, 0)),
            pl.BlockSpec((None, 1, D), lambda i, j: (i // bpb, 0, 0)),
            pl.BlockSpec((None, 1, D), lambda i, j: (i // bpb, 0, 0)),
            pl.BlockSpec((None, D, tn), lambda i, j: (layer, 0, j)),
        ],
        out_specs=pl.BlockSpec((tm, tn), lambda i, j: (i, j)),
        out_shape=jax.ShapeDtypeStruct((M, N), BF16),
        scratch_shapes=[pltpu.VMEM((tm, D), BF16)],
        compiler_params=_cparams(("arbitrary", "arbitrary")),
        name="inproj",
    )(x2d, sc, sh, w_bf16)


def _t5_bucket_np(rel):
    n = np.maximum(rel, 0)
    max_exact = NUM_BUCKETS // 2
    ratio = np.maximum(n, 1).astype(np.float32) / np.float32(max_exact)
    large = max_exact + (np.log(ratio).astype(np.float32) / np.float32(math.log(MAX_DISTANCE / max_exact))
                         * np.float32(NUM_BUCKETS - max_exact)).astype(np.int32)
    large = np.minimum(large, NUM_BUCKETS - 1)
    return np.where(n < max_exact, n, large)


def _bucket_thresholds():
    buckets = _t5_bucket_np(np.arange(2 * MAX_DISTANCE))
    assert np.all(np.diff(buckets) >= 0) and buckets[MAX_DISTANCE] == NUM_BUCKETS - 1
    return [int(np.argmax(buckets >= k)) for k in range(NUM_BUCKETS)]


def _relbias_kernel(tab_ref, o_ref, *, t, n_heads, thresholds):
    h = pl.program_id(0)
    row = lax.broadcasted_iota(jnp.int32, (t, t), 0)
    col = lax.broadcasted_iota(jnp.int32, (t, t), 1)
    for blk in range(2):
        rel = row - col + blk * t
        for mp in range(2):
            entry = lambda k: tab_ref[(k * n_heads + h) * 2 + mp]
            far = entry(NUM_BUCKETS - 1)
            val = jnp.full((t, t), entry(0) - far, F32)
            for k in range(1, NUM_BUCKETS):
                val = jnp.where(rel >= thresholds[k], entry(k) - far, val)
            if blk == 0:
                val = jnp.where(rel < 0, MASK_VALUE, val)
            o_ref[blk, mp * t:(mp + 1) * t, :] = val


def rel_bias_tiles(rel_bias, t):
    assert t >= MAX_DISTANCE
    n_heads = rel_bias.shape[1]
    kern = functools.partial(_relbias_kernel, t=t, n_heads=n_heads, thresholds=_bucket_thresholds())
    grid_spec = pltpu.PrefetchScalarGridSpec(
        num_scalar_prefetch=1,
        grid=(n_heads,),
        in_specs=[],
        out_specs=pl.BlockSpec((None, 2, 2 * t, t), lambda h, tab: (h, 0, 0, 0)),
    )
    return pl.pallas_call(
        kern,
        grid_spec=grid_spec,
        out_shape=jax.ShapeDtypeStruct((n_heads, 2, 2 * t, t), F32),
        compiler_params=_cparams(("arbitrary",)),
        name="relbias",
    )(rel_bias.reshape(-1).astype(F32))


def _diff_kernel(lam_ref, q_ref, k_ref, v_ref, bias_ref, g_ref, o_ref, qs_ref, qn_ref, s_ref, m_ref, acc_ref,
                 *, t, rc, nq, out_scale):
    ones = jnp.ones((t, HEAD_DIM), BF16)

    def keys(j):
        return pl.ds(pl.multiple_of(j * t, t), t)

    def stack_queries(qi, dst_ref):
        q = q_ref[keys(qi), :] * BF16(DIFF_QK_DIM ** -0.5)
        lane = lax.broadcasted_iota(jnp.int32, q.shape, 1)
        zero = jnp.zeros_like(q)
        dst_ref[0:t, :] = jnp.where(lane < DIFF_QK_DIM, q, zero)
        dst_ref[t:2 * t, :] = jnp.where(lane >= DIFF_QK_DIM, q, zero)

    def step(j, bias_blk, next_queries_ref, next_j):
        vb = v_ref[keys(j), :]
        vb1 = jnp.concatenate([vb, ones], axis=1)
        kb_next = k_ref[keys(next_j), :]
        diagonal = bias_blk == 0
        for r in range(2 * t // rc):
            rows = pl.ds(r * rc, rc)
            nc = (r % (t // rc) + 1) * rc if diagonal else t
            s = s_ref[rows, :nc]
            s_ref[rows, :] = _dot_nt(next_queries_ref[rows, :], kb_next)
            if bias_blk is not None:
                s = s + bias_ref[bias_blk, rows, :nc]
            m_old = m_ref[rows, :]
            m_new = jnp.maximum(m_old, jnp.max(s, axis=1, keepdims=True))
            p = jnp.exp(s - jnp.concatenate([m_new] * (nc // LANES), axis=1)).astype(BF16)
            alpha = jnp.exp(m_old - m_new)
            acc_ref[rows, :] = jnp.concatenate([alpha, alpha], axis=1) * acc_ref[rows, :] + _dot(p, vb1[:nc])
            m_ref[rows, :] = m_new

    stack_queries(0, qs_ref)
    kb0 = k_ref[keys(0), :]
    for r in range(2 * t // rc):
        rows = pl.ds(r * rc, rc)
        s_ref[rows, :] = _dot_nt(qs_ref[rows, :], kb0)

    def tile_body(qi, carry):
        m_ref[...] = jnp.full(m_ref.shape, MASK_VALUE, F32)
        acc_ref[...] = jnp.zeros(acc_ref.shape, F32)

        def far_body(j, c):
            step(j, None, qs_ref, j + 1)
            return c

        lax.fori_loop(0, jnp.maximum(qi - 1, 0), far_body, 0)

        @pl.when(qi >= 1)
        def _():
            step(qi - 1, 1, qs_ref, qi)

        stack_queries(jnp.minimum(qi + 1, nq - 1), qn_ref)
        step(qi, 0, qn_ref, 0)

        acc = acc_ref[...]
        ratio = acc[:, :HEAD_DIM] / acc[:, HEAD_DIM:]
        o = ratio[:t] - lam_ref[0] * ratio[t:]
        o = o * lax.rsqrt(jnp.mean(o * o, axis=-1, keepdims=True) + SUBLN_EPS)
        o_ref[keys(qi), :] = (o * g_ref[...] * out_scale).astype(o_ref.dtype)
        qs_ref[...] = qn_ref[...]
        return carry

    lax.fori_loop(0, nq, tile_body, 0)


def diff_attention(proj, lam, bias_tiles, subln_g, batch, seq, lam_init, t):
    M = proj.shape[0]
    H = N_DIFF_HEADS
    nq = seq // t
    kern = functools.partial(_diff_kernel, t=t, rc=min(DIFF_ROW_CHUNK, t), nq=nq, out_scale=1.0 - lam_init)
    grid_spec = pltpu.PrefetchScalarGridSpec(
        num_scalar_prefetch=1,
        grid=(H, batch),
        in_specs=[
            pl.BlockSpec((seq, HEAD_DIM), lambda h, b, s: (b, h)),
            pl.BlockSpec((seq, HEAD_DIM), lambda h, b, s: (b, H + h)),
            pl.BlockSpec((seq, HEAD_DIM), lambda h, b, s: (b, 2 * H + h)),
            pl.BlockSpec((None, 2, 2 * t, t), lambda h, b, s: (h, 0, 0, 0)),
            pl.BlockSpec((1, HEAD_DIM), lambda h, b, s: (0, 0)),
        ],
        out_specs=pl.BlockSpec((seq, HEAD_DIM), lambda h, b, s: (b, h)),
        scratch_shapes=[
            pltpu.VMEM((2 * t, HEAD_DIM), BF16),
            pltpu.VMEM((2 * t, HEAD_DIM), BF16),
            pltpu.VMEM((2 * t, t), F32),
            pltpu.VMEM((2 * t, LANES), F32),
            pltpu.VMEM((2 * t, 2 * HEAD_DIM), F32),
        ],
    )
    return pl.pallas_call(
        kern,
        grid_spec=grid_spec,
        out_shape=jax.ShapeDtypeStruct((M, H * HEAD_DIM), BF16),
        compiler_params=_cparams(("arbitrary", "arbitrary")),
        name="diffattn",
    )(lam.reshape(1).astype(F32), proj, proj, proj, bias_tiles, subln_g.reshape(1, HEAD_DIM))


def _sb_kernel(q_ref, k_ref, v_ref, o_ref, carry_ref, acc_ref, *, t, hp, scale):
    qi = pl.program_id(2)

    def keys(j):
        return pl.ds(pl.multiple_of(j * t, t), t)

    def upper_ones(n):
        row = lax.broadcasted_iota(jnp.int32, (n, n), 0)
        col = lax.broadcasted_iota(jnp.int32, (n, n), 1)
        return jnp.where(row > col, 1.0, 0.0).astype(BF16)

    def sticks(z, valid):
        lp = jnp.log(1.0 + jnp.exp(-jnp.abs(z)))
        log_keep = jnp.minimum(-z, 0.0) - lp
        log_beta = jnp.minimum(z, 0.0) - lp
        if valid is not None:
            log_keep = jnp.where(valid, log_keep, 0.0)
        return log_keep, log_beta

    prev = jnp.maximum(qi - 1, 0)
    qrow = lax.broadcasted_iota(jnp.int32, (t, 2 * t), 0)
    kcol = lax.broadcasted_iota(jnp.int32, (t, 2 * t), 1)
    n_prev = jnp.where(qi > 0, t, 0)
    valid = (kcol < n_prev) | ((kcol >= t) & (kcol - t < qrow))
    upper2 = upper_ones(2 * t)
    for hh in range(hp):
        cols = slice(hh * HEAD_DIM, (hh + 1) * HEAD_DIM)
        kw = jnp.concatenate([k_ref[keys(prev), cols], k_ref[keys(qi), cols]], axis=0)
        vw = jnp.concatenate([v_ref[keys(prev), cols], v_ref[keys(qi), cols]], axis=0)
        log_keep, log_beta = sticks(_dot_nt(q_ref[:, cols], kw) * scale, valid)
        hi, lo = _split(log_keep)
        suffix = _dot(hi, upper2) + _dot(lo, upper2)
        a = jnp.where(valid, jnp.exp(log_beta + suffix), 0.0)
        acc_ref[:, cols] = _dot(a.astype(BF16), vw)
        carry_ref[hh] = jnp.broadcast_to(jnp.sum(log_keep, axis=1, keepdims=True), (t, LANES))

    upper = upper_ones(t)

    def step(j):
        for hh in range(hp):
            cols = slice(hh * HEAD_DIM, (hh + 1) * HEAD_DIM)
            log_keep, log_beta = sticks(_dot_nt(q_ref[:, cols], k_ref[keys(j), cols]) * scale, None)
            hi, lo = _split(log_keep)
            suffix = _dot(hi, upper) + _dot(lo, upper)
            carry = carry_ref[hh]
            a = jnp.exp(log_beta + suffix + jnp.concatenate([carry] * (t // LANES), axis=1))
            acc_ref[:, cols] += _dot(a.astype(BF16), v_ref[keys(j), cols])
            carry_ref[hh] = carry + jnp.sum(log_keep, axis=1, keepdims=True)

    def cond(state):
        n, live = state
        return jnp.logical_and(n < qi - 1, live > SB_EXIT)

    def body(state):
        n, _ = state
        step(qi - 2 - n)
        return n + 1, jnp.max(carry_ref[...])

    lax.while_loop(cond, body, (jnp.int32(0), jnp.max(carry_ref[...])))
    o_ref[...] = acc_ref[...].astype(o_ref.dtype)


def sb_attention(proj, batch, seq, t, hp=SB_HEADS_PER_STEP):
    M = proj.shape[0]
    H = N_SB_HEADS
    base = 3 * N_DIFF_HEADS // hp
    nq = seq // t
    w = hp * HEAD_DIM
    kern = functools.partial(_sb_kernel, t=t, hp=hp, scale=HEAD_DIM ** -0.5)
    return pl.pallas_call(
        kern,
        grid=(H // hp, batch, nq),
        in_specs=[
            pl.BlockSpec((t, w), lambda h, b, qi: (b * nq + qi, base + h)),
            pl.BlockSpec((seq, w), lambda h, b, qi: (b, base + H // hp + h)),
            pl.BlockSpec((seq, w), lambda h, b, qi: (b, base + 2 * (H // hp) + h)),
        ],
        out_specs=pl.BlockSpec((t, w), lambda h, b, qi: (b * nq + qi, h)),
        out_shape=jax.ShapeDtypeStruct((M, H * HEAD_DIM), BF16),
        scratch_shapes=[pltpu.VMEM((hp, t, LANES), F32), pltpu.VMEM((t, w), F32)],
        compiler_params=_cparams(("arbitrary", "arbitrary", "arbitrary")),
        name="sbattn",
    )(proj, proj, proj)


ROUTE_E1, ROUTE_E2, ROUTE_W1, ROUTE_W2, ROUTE_R1, ROUTE_R2 = range(6)


def _mix_kernel(a_ref, b_ref, wo_ref, x_ref, g1_ref, sc2_ref, sh2_ref, lng_ref, lnb_ref, wrt_ref, brt_ref,
                x1_ref, u2_ref, route_ref, cnt_ref, carry_ref, *, alpha, half):
    i = pl.program_id(0)

    @pl.when(i == 0)
    def _():
        carry_ref[...] = jnp.zeros(carry_ref.shape, F32)

    mix = _dot(a_ref[...], wo_ref[pl.ds(0, half), :]) + _dot(b_ref[...], wo_ref[pl.ds(half, half), :])
    h = alpha * x_ref[...] + (1.0 + g1_ref[...]) * mix
    x1 = _layer_norm(h, lng_ref[...], lnb_ref[...])
    x1_ref[...] = x1
    u2 = x1 * (1.0 + sc2_ref[...]) + sh2_ref[...]
    nseg = u2.shape[1] // (2 * LANES)
    for s in range(nseg):
        lo = u2[:, 2 * s * LANES:(2 * s + 1) * LANES].astype(BF16).astype(F32)
        hi = u2[:, (2 * s + 1) * LANES:(2 * s + 2) * LANES].astype(BF16).astype(F32)
        lo_bits = lax.shift_right_logical(lax.bitcast_convert_type(lo, jnp.uint32), jnp.uint32(16))
        hi_bits = lax.bitcast_convert_type(hi, jnp.uint32) & jnp.uint32(HI_HALF)
        u2_ref[pl.ds(s, u2.shape[0], stride=nseg), :] = lo_bits | hi_bits

    logits = _dot3(u2, wrt_ref[...]) + brt_ref[...]
    tm = logits.shape[0]
    lane = lax.broadcasted_iota(jnp.int32, (tm, LANES), 1).astype(F32)
    gmask = lane < N_GROUPS
    gl = jnp.where(gmask, logits, MASK_VALUE)
    gmax = jnp.max(gl, axis=1, keepdims=True)
    gidx = jnp.min(jnp.where(gmask & (gl == gmax), lane, float(LANES)), axis=1, keepdims=True)
    gsum = jnp.sum(jnp.where(gmask, jnp.exp(gl - gmax), 0.0), axis=1, keepdims=True)
    gp = 1.0 / gsum

    lo = N_GROUPS + gidx * EXPERTS_PER_GROUP
    emask = (lane >= lo) & (lane < lo + EXPERTS_PER_GROUP)
    el = jnp.where(emask, logits, MASK_VALUE)
    v1 = jnp.max(el, axis=1, keepdims=True)
    i1 = jnp.min(jnp.where(emask & (el == v1), lane, float(LANES)), axis=1, keepdims=True)
    emask2 = emask & (lane != i1)
    el2 = jnp.where(emask2, logits, MASK_VALUE)
    v2 = jnp.max(el2, axis=1, keepdims=True)
    i2 = jnp.min(jnp.where(emask2 & (el2 == v2), lane, float(LANES)), axis=1, keepdims=True)
    tt = jnp.exp(v2 - v1)
    w1 = gp / (1.0 + tt)
    w2 = gp * tt / (1.0 + tt)
    e1 = i1 - N_GROUPS
    e2 = i2 - N_GROUPS

    oh1 = lane == e1
    oh2 = lane == e2
    oh1b = jnp.where(oh1, 1.0, 0.0).astype(BF16)
    oh2b = jnp.where(oh2, 1.0, 0.0).astype(BF16)
    row = lax.broadcasted_iota(jnp.int32, (tm, tm), 0)
    col = lax.broadcasted_iota(jnp.int32, (tm, tm), 1)
    lower = jnp.where(col < row, 1.0, 0.0).astype(BF16)
    c1 = jnp.sum(jnp.where(oh1, 1.0, 0.0), axis=0, keepdims=True)
    c2 = jnp.sum(jnp.where(oh2, 1.0, 0.0), axis=0, keepdims=True)
    carry = carry_ref[...]
    r1 = _dot(lower, oh1b) + carry
    r2 = _dot(lower, oh2b) + (carry + c1)
    rank1 = jnp.sum(jnp.where(oh1, r1, 0.0), axis=1, keepdims=True)
    rank2 = jnp.sum(jnp.where(oh2, r2, 0.0), axis=1, keepdims=True)
    carry = carry + c1 + c2
    carry_ref[...] = carry
    cnt_ref[...] = jnp.broadcast_to(carry, cnt_ref.shape)

    out = jnp.zeros((tm, LANES), F32)
    for ln, val in ((ROUTE_E1, e1), (ROUTE_E2, e2), (ROUTE_W1, w1), (ROUTE_W2, w2),
                    (ROUTE_R1, rank1), (ROUTE_R2, rank2)):
        out = jnp.where(lane == float(ln), val, out)
    route_ref[...] = out


def mix_ln_route(a_out, b_out, wo_bf16, layer, x2d, g1, sc2, sh2, ln_g, ln_b, w_rt, b_rt, seq, alpha, tm=256):
    M, D = x2d.shape
    half = a_out.shape[1]
    bpb = seq // tm
    nseg = D // (2 * LANES)
    row = lambda i: (i // bpb, 0, 0)
    full = lambda i: (0, 0)
    kern = functools.partial(_mix_kernel, alpha=alpha, half=half)
    return pl.pallas_call(
        kern,
        grid=(M // tm,),
        in_specs=[
            pl.BlockSpec((tm, half), lambda i: (i, 0)),
            pl.BlockSpec((tm, half), lambda i: (i, 0)),
            pl.BlockSpec((None, 2 * half, D), lambda i: (layer, 0, 0)),
            pl.BlockSpec((tm, D), lambda i: (i, 0)),
            pl.BlockSpec((None, 1, D), row),
            pl.BlockSpec((None, 1, D), row),
            pl.BlockSpec((None, 1, D), row),
            pl.BlockSpec((1, D), full),
            pl.BlockSpec((1, D), full),
            pl.BlockSpec((D, LANES), full),
            pl.BlockSpec((1, LANES), full),
        ],
        out_specs=[
            pl.BlockSpec((tm, D), lambda i: (i, 0)),
            pl.BlockSpec((tm * nseg, LANES), lambda i: (i, 0)),
            pl.BlockSpec((tm, LANES), lambda i: (i, 0)),
            pl.BlockSpec((8, LANES), full),
        ],
        out_shape=[
            jax.ShapeDtypeStruct((M, D), F32),
            jax.ShapeDtypeStruct((M * nseg, LANES), jnp.uint32),
            jax.ShapeDtypeStruct((M, LANES), F32),
            jax.ShapeDtypeStruct((8, LANES), F32),
        ],
        scratch_shapes=[pltpu.VMEM((1, LANES), F32)],
        compiler_params=_cparams(("arbitrary",)),
        name="mix",
    )(a_out, b_out, wo_bf16, x2d, g1, sc2, sh2, ln_g.reshape(1, D), ln_b.reshape(1, D), w_rt, b_rt)


def _item_copy(src_hbm, dst_vmem, src_item, dst_item, rpi, sem):
    src = src_hbm.at[pl.ds(pl.multiple_of(src_item * rpi, rpi), rpi), :]
    dst = dst_vmem.at[pl.ds(pl.multiple_of(dst_item * rpi, rpi), rpi), :]
    return pltpu.make_async_copy(src, dst, sem)


def _start_items(idx_ref, base, n_items, rpi, src_hbm, dst_vmem, sem, unroll=False):
    def issue(r, c):
        _item_copy(src_hbm, dst_vmem, idx_ref[base + r], r, rpi, sem).start()
        return c

    lax.fori_loop(0, n_items, issue, 0, unroll=unroll)


def _wait_items(n_items, rpi, src_hbm, dst_vmem, sem):
    pltpu.make_async_copy(src_hbm.at[pl.ds(0, n_items * rpi), :], dst_vmem, sem).wait()


def _expert_kernel(tok_ref, be_ref, nused_ref, u_hbm, wg_ref, wu_ref, wd_ref, o_ref, xbuf_ref, xs_ref, sems,
                   *, tb, nseg, nrow):
    i = pl.program_id(0)
    n_used = nused_ref[0]
    slot = i % GATHER_SLOTS

    def start(k, unroll):
        s = k % GATHER_SLOTS
        tile = jnp.minimum(k, n_used - 1)
        _start_items(tok_ref, tile * tb, tb, nseg, u_hbm, xbuf_ref.at[s], sems.at[s], unroll)

    def wait(k):
        s = k % GATHER_SLOTS
        _wait_items(tb, nseg, u_hbm, xbuf_ref.at[s], sems.at[s])

    @pl.when(i == 0)
    def _():
        for k in range(GATHER_SLOTS - 1):
            start(k, False)

    @pl.when(i < n_used)
    def _():
        wait(i)
        start(i + GATHER_SLOTS - 1, True)
        for s in range(nseg):
            w = xbuf_ref[slot, pl.ds(s, tb, stride=nseg), :]
            lo = lax.bitcast_convert_type(lax.shift_left(w, jnp.uint32(16)), F32)
            hi = lax.bitcast_convert_type(w & jnp.uint32(HI_HALF), F32)
            xs_ref[:, 2 * s * LANES:(2 * s + 1) * LANES] = lo.astype(BF16)
            xs_ref[:, (2 * s + 1) * LANES:(2 * s + 2) * LANES] = hi.astype(BF16)
        x = xs_ref[...]
        g = _dot(x, wg_ref[...])
        u = _dot(x, wu_ref[...])
        h = (g * jax.nn.sigmoid(g) * u).astype(BF16)
        y = _dot(h, wd_ref[...])
        for s in range(nrow):
            o_ref[pl.ds(s, tb, stride=nrow), :] = y[:, s * LANES:(s + 1) * LANES]

    @pl.when(i == n_used - 1)
    def _():
        for k in range(1, GATHER_SLOTS):
            wait(i + k)

    @pl.when(i >= n_used)
    def _():
        o_ref[...] = jnp.zeros(o_ref.shape, o_ref.dtype)


def expert_mlp(pad_tok, block_e, n_used, u2p, wg, wu, wd, layer, tb):
    P = pad_tok.shape[0]
    D = wg.shape[2]
    F = wg.shape[3]
    nseg = D // (2 * LANES)
    nrow = D // LANES

    def wmap(i, tok, be, nu):
        return (layer, be[jnp.minimum(i, jnp.maximum(nu[0] - 1, 0))], 0, 0)

    grid_spec = pltpu.PrefetchScalarGridSpec(
        num_scalar_prefetch=3,
        grid=(P // tb,),
        in_specs=[
            pl.BlockSpec(memory_space=pl.ANY),
            pl.BlockSpec((None, None, D, F), wmap),
            pl.BlockSpec((None, None, D, F), wmap),
            pl.BlockSpec((None, None, F, D), wmap),
        ],
        out_specs=pl.BlockSpec((tb * nrow, LANES), lambda i, tok, be, nu: (i, 0)),
        scratch_shapes=[pltpu.VMEM((GATHER_SLOTS, tb * nseg, LANES), jnp.uint32), pltpu.VMEM((tb, D), BF16),
                        pltpu.SemaphoreType.DMA((GATHER_SLOTS,))],
    )
    return pl.pallas_call(
        functools.partial(_expert_kernel, tb=tb, nseg=nseg, nrow=nrow),
        grid_spec=grid_spec,
        out_shape=jax.ShapeDtypeStruct((P * nrow, LANES), F32),
        compiler_params=_cparams(("arbitrary",)),
        name="expert",
    )(pad_tok, block_e, n_used, u2p, wg, wu, wd)


def _final_kernel(dest_ref, x1_ref, route_ref, g2_ref, lng_ref, lnb_ref, yb_hbm, o_ref, buf_ref, sems,
                  *, alpha, tm, m, nrow):
    i = pl.program_id(0)
    slot = i % GATHER_SLOTS
    last = pl.num_programs(0) - 1

    def start(n, unroll):
        s = n % GATHER_SLOTS
        tile = jnp.minimum(n, last)
        for k in range(TOP_K):
            _start_items(dest_ref, k * m + tile * tm, tm, nrow, yb_hbm, buf_ref.at[s, k], sems.at[s], unroll)

    def wait(n):
        s = n % GATHER_SLOTS
        for k in range(TOP_K):
            _wait_items(tm, nrow, yb_hbm, buf_ref.at[s, k], sems.at[s])

    @pl.when(i == 0)
    def _():
        for n in range(GATHER_SLOTS - 1):
            start(n, False)

    wait(i)
    start(i + GATHER_SLOTS - 1, True)

    def expert_out(k):
        chunks = [buf_ref[slot, k, pl.ds(s, tm, stride=nrow), :] for s in range(nrow)]
        return jnp.concatenate(chunks, axis=1)

    route = route_ref[...]
    w1 = route[:, ROUTE_W1:ROUTE_W1 + 1]
    w2 = route[:, ROUTE_W2:ROUTE_W2 + 1]
    y = w1 * expert_out(0) + w2 * expert_out(1)
    h = alpha * x1_ref[...] + (1.0 + g2_ref[...]) * y
    o_ref[...] = _layer_norm(h, lng_ref[...], lnb_ref[...])

    @pl.when(i == last)
    def _():
        for n in range(1, GATHER_SLOTS):
            wait(i + n)


def combine_ln(dest_flat, x1, route, g2, ln_g, ln_b, yb, seq, alpha, tm=256):
    M, D = x1.shape
    bpb = seq // tm
    nrow = D // LANES
    grid_spec = pltpu.PrefetchScalarGridSpec(
        num_scalar_prefetch=1,
        grid=(M // tm,),
        in_specs=[
            pl.BlockSpec((tm, D), lambda i, d: (i, 0)),
            pl.BlockSpec((tm, LANES), lambda i, d: (i, 0)),
            pl.BlockSpec((None, 1, D), lambda i, d: (i // bpb, 0, 0)),
            pl.BlockSpec((1, D), lambda i, d: (0, 0)),
            pl.BlockSpec((1, D), lambda i, d: (0, 0)),
            pl.BlockSpec(memory_space=pl.ANY),
        ],
        out_specs=pl.BlockSpec((tm, D), lambda i, d: (i, 0)),
        scratch_shapes=[pltpu.VMEM((GATHER_SLOTS, TOP_K, tm * nrow, LANES), F32),
                        pltpu.SemaphoreType.DMA((GATHER_SLOTS,))],
    )
    return pl.pallas_call(
        functools.partial(_final_kernel, alpha=alpha, tm=tm, m=M, nrow=nrow),
        grid_spec=grid_spec,
        out_shape=jax.ShapeDtypeStruct((M, D), F32),
        compiler_params=_cparams(("arbitrary",)),
        name="final",
    )(dest_flat, x1, route, g2, ln_g.reshape(1, D), ln_b.reshape(1, D), yb)


def moe_layout(route, cnt, tb):
    M = route.shape[0]
    n_assign = M * TOP_K
    counts = cnt[0, :N_EXPERTS].astype(jnp.int32)
    padded = ((counts + tb - 1) // tb) * tb
    pends = jnp.cumsum(padded)
    pstarts = pends - padded
    n_tiles = (n_assign + N_EXPERTS * (tb - 1) + tb - 1) // tb
    e = route[:, ROUTE_E1:ROUTE_E2 + 1].astype(jnp.int32)
    rank = route[:, ROUTE_R1:ROUTE_R2 + 1].astype(jnp.int32)
    dest = pstarts[e] + rank
    dest_flat = dest.T.reshape(-1)
    tok = jnp.tile(jnp.arange(M, dtype=jnp.int32), TOP_K)
    pad_tok = jnp.zeros((n_tiles * tb,), jnp.int32).at[dest_flat].set(tok)
    tile_start = jnp.arange(n_tiles, dtype=jnp.int32) * tb
    block_e = jnp.sum((pends[None, :] <= tile_start[:, None]).astype(jnp.int32), axis=1)
    block_e = jnp.minimum(block_e, N_EXPERTS - 1)
    n_used = (pends[-1] // tb).astype(jnp.int32).reshape(1)
    return dest_flat, pad_tok, block_e, n_used


def _forward(x, c, w_ada, b_ada, w_in, diff_lambda, diff_subln_g, rel_bias, w_o, ln_g, ln_b, w_group, b_group,
             w_router, b_router, w_gate, w_up, w_down, *, diff_tile, sb_tile, moe_tile, ada_tn):
    B, S, D = x.shape
    depth = w_ada.shape[0]
    M = B * S
    alpha = (2 * depth) ** 0.25

    c8 = jnp.zeros((8, D), F32).at[:B].set(c)
    mod = ada_modulation(c8, w_ada, b_ada, ada_tn)[:, :B]
    mod = mod.reshape(depth, B, 6, 1, D)
    bias_tiles = rel_bias_tiles(rel_bias, diff_tile)
    w_in_b, w_o_b = w_in.astype(BF16), w_o.astype(BF16)
    w_gate_b, w_up_b, w_down_b = w_gate.astype(BF16), w_up.astype(BF16), w_down.astype(BF16)

    x2d = x.reshape(M, D)
    for l in range(depth):
        sh1, sc1, g1, sh2, sc2, g2 = (mod[l, :, n] for n in range(6))
        proj = in_projection(x2d, sc1, sh1, w_in_b, l, S)

        lam_init = 0.8 - 0.6 * math.exp(-0.3 * l)
        lp = diff_lambda[l].astype(F32)
        lam = jnp.exp(jnp.sum(lp[0] * lp[1])) - jnp.exp(jnp.sum(lp[2] * lp[3])) + lam_init
        a_out = diff_attention(proj, lam, bias_tiles, diff_subln_g[l], B, S, lam_init, diff_tile)
        b_out = sb_attention(proj, B, S, sb_tile)

        w_rt = jnp.zeros((D, LANES), F32).at[:, :N_GROUPS].set(w_group[l])
        w_rt = w_rt.at[:, N_GROUPS:N_GROUPS + N_EXPERTS].set(w_router[l])
        b_rt = jnp.zeros((1, LANES), F32).at[0, :N_GROUPS].set(b_group[l])
        b_rt = b_rt.at[0, N_GROUPS:N_GROUPS + N_EXPERTS].set(b_router[l])
        x1, u2, route, cnt = mix_ln_route(a_out, b_out, w_o_b, l, x2d, g1, sc2, sh2,
                                          ln_g[l, 0], ln_b[l, 0], w_rt, b_rt, S, alpha)

        dest_flat, pad_tok, block_e, n_used = moe_layout(route, cnt, moe_tile)
        yb = expert_mlp(pad_tok, block_e, n_used, u2, w_gate_b, w_up_b, w_down_b, l, moe_tile)
        x2d = combine_ln(dest_flat, x1, route, g2, ln_g[l, 1], ln_b[l, 1], yb, S, alpha)
    return x2d.reshape(B, S, D)


def kernel(x, c, w_ada, b_ada, w_in, diff_lambda, diff_subln_g, rel_bias, w_o, ln_g, ln_b, w_group, b_group,
           w_router, b_router, w_gate, w_up, w_down):
    return _forward(x, c, w_ada, b_ada, w_in, diff_lambda, diff_subln_g, rel_bias, w_o, ln_g, ln_b, w_group,
                    b_group, w_router, b_router, w_gate, w_up, w_down,
                    diff_tile=512, sb_tile=256, moe_tile=256, ada_tn=1024)
```

```python
import functools
import math

import jax
import jax.numpy as jnp
import numpy as np
from jax import lax
from jax.experimental import pallas as pl
from jax.experimental.pallas import tpu as pltpu

F32 = jnp.float32
BF16 = jnp.bfloat16

N_DIFF_HEADS = 8
N_SB_HEADS = 8
HEAD_DIM = 128
DIFF_QK_DIM = 64
NUM_BUCKETS = 32
MAX_DISTANCE = 128
N_GROUPS = 4
EXPERTS_PER_GROUP = 8
N_EXPERTS = N_GROUPS * EXPERTS_PER_GROUP
TOP_K = 2
LN_EPS = 1e-5
SUBLN_EPS = 1e-5

LANES = 128
MASK_VALUE = -1e30
SB_EXIT = -104.0
DIFF_ROW_CHUNK = 128
SB_HEADS_PER_STEP = 2
HI_HALF = 0xFFFF0000
GATHER_SLOTS = 3
WEIGHT_CHUNKS = 4
VMEM_LIMIT = 56 * 1024 * 1024


def _cparams(sem):
    return pltpu.CompilerParams(dimension_semantics=sem, vmem_limit_bytes=VMEM_LIMIT)


def _dot(a, b):
    return jnp.dot(a, b, preferred_element_type=F32)


def _dot_nt(a, b):
    return lax.dot_general(a, b, (((1,), (1,)), ((), ())), preferred_element_type=F32)


def _split(a):
    hi = a.astype(BF16)
    lo = (a - hi.astype(F32)).astype(BF16)
    return hi, lo


def _dot3(a, b):
    a_hi, a_lo = _split(a)
    b_hi, b_lo = _split(b)
    return _dot(a_hi, b_hi) + (_dot(a_hi, b_lo) + _dot(a_lo, b_hi))


def _layer_norm(h, g, b):
    mu = jnp.mean(h, axis=-1, keepdims=True)
    d = h - mu
    var = jnp.mean(d * d, axis=-1, keepdims=True)
    return d * lax.rsqrt(var + LN_EPS) * g + b


def _ada_kernel(c_ref, w_ref, b_ref, o_ref):
    c = c_ref[...]
    s = c * jax.nn.sigmoid(c)
    o_ref[...] = _dot3(s, w_ref[...]) + b_ref[...]


def ada_modulation(c8, w_ada, b_ada, tn=1024):
    L, D, N = w_ada.shape
    return pl.pallas_call(
        _ada_kernel,
        grid=(L, N // tn),
        in_specs=[
            pl.BlockSpec((8, D), lambda l, j: (0, 0)),
            pl.BlockSpec((None, D, tn), lambda l, j: (l, 0, j)),
            pl.BlockSpec((None, 1, tn), lambda l, j: (l, 0, j)),
        ],
        out_specs=pl.BlockSpec((None, 8, tn), lambda l, j: (l, 0, j)),
        out_shape=jax.ShapeDtypeStruct((L, 8, N), F32),
        compiler_params=_cparams(("arbitrary", "arbitrary")),
        name="ada",
    )(c8, w_ada, b_ada.reshape(L, 1, N))


def _inproj_kernel(x_ref, sc_ref, sh_ref, w_ref, o_ref, u_ref):
    @pl.when(pl.program_id(1) == 0)
    def _():
        u_ref[...] = (x_ref[...] * (1.0 + sc_ref[...]) + sh_ref[...]).astype(BF16)

    o_ref[...] = _dot(u_ref[...], w_ref[...]).astype(o_ref.dtype)


def in_projection(x2d, sc, sh, w_bf16, layer, seq, tm=1024, tn=1536):
    M, D = x2d.shape
    N = w_bf16.shape[2]
    tm = min(tm, seq)
    tn = min(tn, N)
    bpb = seq // tm
    return pl.pallas_call(
        _inproj_kernel,
        grid=(M // tm, N // tn),
        in_specs=[
            pl.BlockSpec((tm, D), lambda i, j: (i, 0)),
            pl.BlockSpec((None, 1, D), lambda i, j: (i // bpb, 0, 0)),
            pl.BlockSpec((None, 1, D), lambda i, j: (i // bpb, 0, 0)),
            pl.BlockSpec((None, D, tn), lambda i, j: (layer, 0, j)),
        ],
        out_specs=pl.BlockSpec((tm, tn), lambda i, j: (i, j)),
        out_shape=jax.ShapeDtypeStruct((M, N), BF16),
        scratch_shapes=[pltpu.VMEM((tm, D), BF16)],
        compiler_params=_cparams(("arbitrary", "arbitrary")),
        name="inproj",
    )(x2d, sc, sh, w_bf16)


def _t5_bucket_np(rel):
    n = np.maximum(rel, 0)
    max_exact = NUM_BUCKETS // 2
    ratio = np.maximum(n, 1).astype(np.float32) / np.float32(max_exact)
    large = max_exact + (np.log(ratio).astype(np.float32) / np.float32(math.log(MAX_DISTANCE / max_exact))
                         * np.float32(NUM_BUCKETS - max_exact)).astype(np.int32)
    large = np.minimum(large, NUM_BUCKETS - 1)
    return np.where(n < max_exact, n, large)


def _bucket_thresholds():
    buckets = _t5_bucket_np(np.arange(2 * MAX_DISTANCE))
    assert np.all(np.diff(buckets) >= 0) and buckets[MAX_DISTANCE] == NUM_BUCKETS - 1
    return [int(np.argmax(buckets >= k)) for k in range(NUM_BUCKETS)]


def _relbias_kernel(tab_ref, o_ref, *, t, n_heads, thresholds):
    h = pl.program_id(0)
    row = lax.broadcasted_iota(jnp.int32, (t, t), 0)
    col = lax.broadcasted_iota(jnp.int32, (t, t), 1)
    for blk in range(2):
        rel = row - col + blk * t
        for mp in range(2):
            entry = lambda k: tab_ref[(k * n_heads + h) * 2 + mp]
            far = entry(NUM_BUCKETS - 1)
            val = jnp.full((t, t), entry(0) - far, F32)
            for k in range(1, NUM_BUCKETS):
                val = jnp.where(rel >= thresholds[k], entry(k) - far, val)
            if blk == 0:
                val = jnp.where(rel < 0, MASK_VALUE, val)
            o_ref[blk, mp * t:(mp + 1) * t, :] = val


def rel_bias_tiles(rel_bias, t):
    assert t >= MAX_DISTANCE
    n_heads = rel_bias.shape[1]
    kern = functools.partial(_relbias_kernel, t=t, n_heads=n_heads, thresholds=_bucket_thresholds())
    grid_spec = pltpu.PrefetchScalarGridSpec(
        num_scalar_prefetch=1,
        grid=(n_heads,),
        in_specs=[],
        out_specs=pl.BlockSpec((None, 2, 2 * t, t), lambda h, tab: (h, 0, 0, 0)),
    )
    return pl.pallas_call(
        kern,
        grid_spec=grid_spec,
        out_shape=jax.ShapeDtypeStruct((n_heads, 2, 2 * t, t), F32),
        compiler_params=_cparams(("arbitrary",)),
        name="relbias",
    )(rel_bias.reshape(-1).astype(F32))


def _diff_kernel(lam_ref, q_ref, k_ref, v_ref, bias_ref, g_ref, o_ref, qs_ref, qn_ref, s_ref, m_ref, acc_ref,
                 *, t, rc, nq, out_scale):
    ones = jnp.ones((t, HEAD_DIM), BF16)

    def keys(j):
        return pl.ds(pl.multiple_of(j * t, t), t)

    def stack_queries(qi, dst_ref):
        q = q_ref[keys(qi), :] * BF16(DIFF_QK_DIM ** -0.5)
        lane = lax.broadcasted_iota(jnp.int32, q.shape, 1)
        zero = jnp.zeros_like(q)
        dst_ref[0:t, :] = jnp.where(lane < DIFF_QK_DIM, q, zero)
        dst_ref[t:2 * t, :] = jnp.where(lane >= DIFF_QK_DIM, q, zero)

    def step(j, bias_blk, next_queries_ref, next_j):
        vb = v_ref[keys(j), :]
        vb1 = jnp.concatenate([vb, ones], axis=1)
        kb_next = k_ref[keys(next_j), :]
        diagonal = bias_blk == 0
        for r in range(2 * t // rc):
            rows = pl.ds(r * rc, rc)
            nc = (r % (t // rc) + 1) * rc if diagonal else t
            s = s_ref[rows, :nc]
            s_ref[rows, :] = _dot_nt(next_queries_ref[rows, :], kb_next)
            if bias_blk is not None:
                s = s + bias_ref[bias_blk, rows, :nc]
            m_old = m_ref[rows, :]
            m_new = jnp.maximum(m_old, jnp.max(s, axis=1, keepdims=True))
            p = jnp.exp(s - jnp.concatenate([m_new] * (nc // LANES), axis=1)).astype(BF16)
            alpha = jnp.exp(m_old - m_new)
            acc_ref[rows, :] = jnp.concatenate([alpha, alpha], axis=1) * acc_ref[rows, :] + _dot(p, vb1[:nc])
            m_ref[rows, :] = m_new

    stack_queries(0, qs_ref)
    kb0 = k_ref[keys(0), :]
    for r in range(2 * t // rc):
        rows = pl.ds(r * rc, rc)
        s_ref[rows, :] = _dot_nt(qs_ref[rows, :], kb0)

    def tile_body(qi, carry):
        m_ref[...] = jnp.full(m_ref.shape, MASK_VALUE, F32)
        acc_ref[...] = jnp.zeros(acc_ref.shape, F32)

        def far_body(j, c):
            step(j, None, qs_ref, j + 1)
            return c

        lax.fori_loop(0, jnp.maximum(qi - 1, 0), far_body, 0)

        @pl.when(qi >= 1)
        def _():
            step(qi - 1, 1, qs_ref, qi)

        stack_queries(jnp.minimum(qi + 1, nq - 1), qn_ref)
        step(qi, 0, qn_ref, 0)

        acc = acc_ref[...]
        ratio = acc[:, :HEAD_DIM] / acc[:, HEAD_DIM:]
        o = ratio[:t] - lam_ref[0] * ratio[t:]
        o = o * lax.rsqrt(jnp.mean(o * o, axis=-1, keepdims=True) + SUBLN_EPS)
        o_ref[keys(qi), :] = (o * g_ref[...] * out_scale).astype(o_ref.dtype)
        qs_ref[...] = qn_ref[...]
        return carry

    lax.fori_loop(0, nq, tile_body, 0)


def diff_attention(proj, lam, bias_tiles, subln_g, batch, seq, lam_init, t):
    M = proj.shape[0]
    H = N_DIFF_HEADS
    nq = seq // t
    kern = functools.partial(_diff_kernel, t=t, rc=min(DIFF_ROW_CHUNK, t), nq=nq, out_scale=1.0 - lam_init)
    grid_spec = pltpu.PrefetchScalarGridSpec(
        num_scalar_prefetch=1,
        grid=(H, batch),
        in_specs=[
            pl.BlockSpec((seq, HEAD_DIM), lambda h, b, s: (b, h)),
            pl.BlockSpec((seq, HEAD_DIM), lambda h, b, s: (b, H + h)),
            pl.BlockSpec((seq, HEAD_DIM), lambda h, b, s: (b, 2 * H + h)),
            pl.BlockSpec((None, 2, 2 * t, t), lambda h, b, s: (h, 0, 0, 0)),
            pl.BlockSpec((1, HEAD_DIM), lambda h, b, s: (0, 0)),
        ],
        out_specs=pl.BlockSpec((seq, HEAD_DIM), lambda h, b, s: (b, h)),
        scratch_shapes=[
            pltpu.VMEM((2 * t, HEAD_DIM), BF16),
            pltpu.VMEM((2 * t, HEAD_DIM), BF16),
            pltpu.VMEM((2 * t, t), F32),
            pltpu.VMEM((2 * t, LANES), F32),
            pltpu.VMEM((2 * t, 2 * HEAD_DIM), F32),
        ],
    )
    return pl.pallas_call(
        kern,
        grid_spec=grid_spec,
        out_shape=jax.ShapeDtypeStruct((M, H * HEAD_DIM), BF16),
        compiler_params=_cparams(("arbitrary", "arbitrary")),
        name="diffattn",
    )(lam.reshape(1).astype(F32), proj, proj, proj, bias_tiles, subln_g.reshape(1, HEAD_DIM))


def _sb_kernel(q_ref, k_ref, v_ref, o_ref, carry_ref, acc_ref, *, t, hp, scale):
    qi = pl.program_id(2)

    def keys(j):
        return pl.ds(pl.multiple_of(j * t, t), t)

    def upper_ones(n):
        row = lax.broadcasted_iota(jnp.int32, (n, n), 0)
        col = lax.broadcasted_iota(jnp.int32, (n, n), 1)
        return jnp.where(row > col, 1.0, 0.0).astype(BF16)

    def sticks(z, valid):
        lp = jnp.log(1.0 + jnp.exp(-jnp.abs(z)))
        log_keep = jnp.minimum(-z, 0.0) - lp
        log_beta = jnp.minimum(z, 0.0) - lp
        if valid is not None:
            log_keep = jnp.where(valid, log_keep, 0.0)
        return log_keep, log_beta

    prev = jnp.maximum(qi - 1, 0)
    qrow = lax.broadcasted_iota(jnp.int32, (t, 2 * t), 0)
    kcol = lax.broadcasted_iota(jnp.int32, (t, 2 * t), 1)
    n_prev = jnp.where(qi > 0, t, 0)
    valid = (kcol < n_prev) | ((kcol >= t) & (kcol - t < qrow))
    upper = upper_ones(t)
    for hh in range(hp):
        cols = slice(hh * HEAD_DIM, (hh + 1) * HEAD_DIM)
        kw = jnp.concatenate([k_ref[keys(prev), cols], k_ref[keys(qi), cols]], axis=0)
        vw = jnp.concatenate([v_ref[keys(prev), cols], v_ref[keys(qi), cols]], axis=0)
        log_keep, log_beta = sticks(_dot_nt(q_ref[:, cols], kw) * scale, valid)
        hi, lo = _split(log_keep)
        stacked = jnp.concatenate([hi[:, :t], lo[:, :t], hi[:, t:], lo[:, t:]], axis=0)
        within = _dot(stacked, upper)
        diag_total = jnp.sum(log_keep[:, t:], axis=1, keepdims=True)
        suffix = jnp.concatenate([within[0:t] + within[t:2 * t] + diag_total,
                                  within[2 * t:3 * t] + within[3 * t:4 * t]], axis=1)
        a = jnp.where(valid, jnp.exp(log_beta + suffix), 0.0)
        acc_ref[:, cols] = _dot(a.astype(BF16), vw)
        prev_total = jnp.sum(log_keep[:, :t], axis=1, keepdims=True)
        carry_ref[hh] = jnp.broadcast_to(prev_total + diag_total, (t, LANES))

    def step(j):
        for hh in range(hp):
            cols = slice(hh * HEAD_DIM, (hh + 1) * HEAD_DIM)
            log_keep, log_beta = sticks(_dot_nt(q_ref[:, cols], k_ref[keys(j), cols]) * scale, None)
            hi, lo = _split(log_keep)
            suffix = _dot(hi, upper) + _dot(lo, upper)
            carry = carry_ref[hh]
            a = jnp.exp(log_beta + suffix + jnp.concatenate([carry] * (t // LANES), axis=1))
            acc_ref[:, cols] += _dot(a.astype(BF16), v_ref[keys(j), cols])
            carry_ref[hh] = carry + jnp.sum(log_keep, axis=1, keepdims=True)

    def cond(state):
        n, live = state
        return jnp.logical_and(n < qi - 1, live > SB_EXIT)

    def body(state):
        n, _ = state
        step(qi - 2 - n)
        return n + 1, jnp.max(carry_ref[...])

    lax.while_loop(cond, body, (jnp.int32(0), jnp.max(carry_ref[...])))
    o_ref[...] = acc_ref[...].astype(o_ref.dtype)


def sb_attention(proj, batch, seq, t, hp=SB_HEADS_PER_STEP):
    M = proj.shape[0]
    H = N_SB_HEADS
    base = 3 * N_DIFF_HEADS // hp
    nq = seq // t
    w = hp * HEAD_DIM
    kern = functools.partial(_sb_kernel, t=t, hp=hp, scale=HEAD_DIM ** -0.5)
    return pl.pallas_call(
        kern,
        grid=(H // hp, batch, nq),
        in_specs=[
            pl.BlockSpec((t, w), lambda h, b, qi: (b * nq + qi, base + h)),
            pl.BlockSpec((seq, w), lambda h, b, qi: (b, base + H // hp + h)),
            pl.BlockSpec((seq, w), lambda h, b, qi: (b, base + 2 * (H // hp) + h)),
        ],
        out_specs=pl.BlockSpec((t, w), lambda h, b, qi: (b * nq + qi, h)),
        out_shape=jax.ShapeDtypeStruct((M, H * HEAD_DIM), BF16),
        scratch_shapes=[pltpu.VMEM((hp, t, LANES), F32), pltpu.VMEM((t, w), F32)],
        compiler_params=_cparams(("arbitrary", "arbitrary", "arbitrary")),
        name="sbattn",
    )(proj, proj, proj)


ROUTE_E1, ROUTE_E2, ROUTE_W1, ROUTE_W2, ROUTE_R1, ROUTE_R2 = range(6)


def _mix_kernel(a_ref, b_ref, wo_ref, x_ref, g1_ref, sc2_ref, sh2_ref, lng_ref, lnb_ref, wrt_ref, brt_ref,
                x1_ref, u2_ref, route_ref, cnt_ref, carry_ref, wsplit_ref, *, alpha, half):
    i = pl.program_id(0)

    @pl.when(i == 0)
    def _():
        carry_ref[...] = jnp.zeros(carry_ref.shape, F32)
        w_hi, w_lo = _split(wrt_ref[...])
        wsplit_ref[...] = jnp.concatenate([w_hi, w_lo], axis=1)

    mix = _dot(a_ref[...], wo_ref[pl.ds(0, half), :]) + _dot(b_ref[...], wo_ref[pl.ds(half, half), :])
    h = alpha * x_ref[...] + (1.0 + g1_ref[...]) * mix
    x1 = _layer_norm(h, lng_ref[...], lnb_ref[...])
    x1_ref[...] = x1
    u2 = x1 * (1.0 + sc2_ref[...]) + sh2_ref[...]
    nseg = u2.shape[1] // (2 * LANES)
    for s in range(nseg):
        lo = u2[:, 2 * s * LANES:(2 * s + 1) * LANES].astype(BF16).astype(F32)
        hi = u2[:, (2 * s + 1) * LANES:(2 * s + 2) * LANES].astype(BF16).astype(F32)
        lo_bits = lax.shift_right_logical(lax.bitcast_convert_type(lo, jnp.uint32), jnp.uint32(16))
        hi_bits = lax.bitcast_convert_type(hi, jnp.uint32) & jnp.uint32(HI_HALF)
        u2_ref[pl.ds(s, u2.shape[0], stride=nseg), :] = lo_bits | hi_bits

    u_hi, u_lo = _split(u2)
    hh_hl = _dot(u_hi, wsplit_ref[...])
    lh = _dot(u_lo, wsplit_ref[:, :LANES])
    logits = hh_hl[:, :LANES] + (hh_hl[:, LANES:] + lh) + brt_ref[...]
    tm = logits.shape[0]
    lane = lax.broadcasted_iota(jnp.int32, (tm, LANES), 1).astype(F32)
    gmask = lane < N_GROUPS
    gl = jnp.where(gmask, logits, MASK_VALUE)
    gmax = jnp.max(gl, axis=1, keepdims=True)
    gidx = jnp.min(jnp.where(gmask & (gl == gmax), lane, float(LANES)), axis=1, keepdims=True)
    gsum = jnp.sum(jnp.where(gmask, jnp.exp(gl - gmax), 0.0), axis=1, keepdims=True)
    gp = 1.0 / gsum

    lo = N_GROUPS + gidx * EXPERTS_PER_GROUP
    emask = (lane >= lo) & (lane < lo + EXPERTS_PER_GROUP)
    el = jnp.where(emask, logits, MASK_VALUE)
    v1 = jnp.max(el, axis=1, keepdims=True)
    i1 = jnp.min(jnp.where(emask & (el == v1), lane, float(LANES)), axis=1, keepdims=True)
    emask2 = emask & (lane != i1)
    el2 = jnp.where(emask2, logits, MASK_VALUE)
    v2 = jnp.max(el2, axis=1, keepdims=True)
    i2 = jnp.min(jnp.where(emask2 & (el2 == v2), lane, float(LANES)), axis=1, keepdims=True)
    tt = jnp.exp(v2 - v1)
    w1 = gp / (1.0 + tt)
    w2 = gp * tt / (1.0 + tt)
    e1 = i1 - N_GROUPS
    e2 = i2 - N_GROUPS

    oh1 = lane == e1
    oh2 = lane == e2
    oh1b = jnp.where(oh1, 1.0, 0.0).astype(BF16)
    oh2b = jnp.where(oh2, 1.0, 0.0).astype(BF16)
    row = lax.broadcasted_iota(jnp.int32, (tm, tm), 0)
    col = lax.broadcasted_iota(jnp.int32, (tm, tm), 1)
    lower = jnp.where(col < row, 1.0, 0.0).astype(BF16)
    c1 = jnp.sum(jnp.where(oh1, 1.0, 0.0), axis=0, keepdims=True)
    c2 = jnp.sum(jnp.where(oh2, 1.0, 0.0), axis=0, keepdims=True)
    carry = carry_ref[...]
    r1 = _dot(lower, oh1b) + carry
    r2 = _dot(lower, oh2b) + (carry + c1)
    rank1 = jnp.sum(jnp.where(oh1, r1, 0.0), axis=1, keepdims=True)
    rank2 = jnp.sum(jnp.where(oh2, r2, 0.0), axis=1, keepdims=True)
    carry = carry + c1 + c2
    carry_ref[...] = carry
    cnt_ref[...] = jnp.broadcast_to(carry, cnt_ref.shape)

    out = jnp.zeros((tm, LANES), F32)
    for ln, val in ((ROUTE_E1, e1), (ROUTE_E2, e2), (ROUTE_W1, w1), (ROUTE_W2, w2),
                    (ROUTE_R1, rank1), (ROUTE_R2, rank2)):
        out = jnp.where(lane == float(ln), val, out)
    route_ref[...] = out


def mix_ln_route(a_out, b_out, wo_bf16, layer, x2d, g1, sc2, sh2, ln_g, ln_b, w_rt, b_rt, seq, alpha, tm=256):
    M, D = x2d.shape
    half = a_out.shape[1]
    bpb = seq // tm
    nseg = D // (2 * LANES)
    row = lambda i: (i // bpb, 0, 0)
    full = lambda i: (0, 0)
    kern = functools.partial(_mix_kernel, alpha=alpha, half=half)
    return pl.pallas_call(
        kern,
        grid=(M // tm,),
        in_specs=[
            pl.BlockSpec((tm, half), lambda i: (i, 0)),
            pl.BlockSpec((tm, half), lambda i: (i, 0)),
            pl.BlockSpec((None, 2 * half, D), lambda i: (layer, 0, 0)),
            pl.BlockSpec((tm, D), lambda i: (i, 0)),
            pl.BlockSpec((None, 1, D), row),
            pl.BlockSpec((None, 1, D), row),
            pl.BlockSpec((None, 1, D), row),
            pl.BlockSpec((1, D), full),
            pl.BlockSpec((1, D), full),
            pl.BlockSpec((D, LANES), full),
            pl.BlockSpec((1, LANES), full),
        ],
        out_specs=[
            pl.BlockSpec((tm, D), lambda i: (i, 0)),
            pl.BlockSpec((tm * nseg, LANES), lambda i: (i, 0)),
            pl.BlockSpec((tm, LANES), lambda i: (i, 0)),
            pl.BlockSpec((8, LANES), full),
        ],
        out_shape=[
            jax.ShapeDtypeStruct((M, D), F32),
            jax.ShapeDtypeStruct((M * nseg, LANES), jnp.uint32),
            jax.ShapeDtypeStruct((M, LANES), F32),
            jax.ShapeDtypeStruct((8, LANES), F32),
        ],
        scratch_shapes=[pltpu.VMEM((1, LANES), F32), pltpu.VMEM((D, 2 * LANES), BF16)],
        compiler_params=_cparams(("arbitrary",)),
        name="mix",
    )(a_out, b_out, wo_bf16, x2d, g1, sc2, sh2, ln_g.reshape(1, D), ln_b.reshape(1, D), w_rt, b_rt)


def _item_copy(src_hbm, dst_vmem, src_item, dst_item, rpi, sem):
    src = src_hbm.at[pl.ds(pl.multiple_of(src_item * rpi, rpi), rpi), :]
    dst = dst_vmem.at[pl.ds(pl.multiple_of(dst_item * rpi, rpi), rpi), :]
    return pltpu.make_async_copy(src, dst, sem)


def _start_items(idx_ref, base, n_items, rpi, src_hbm, dst_vmem, sem, unroll=False):
    def issue(r, c):
        _item_copy(src_hbm, dst_vmem, idx_ref[base + r], r, rpi, sem).start()
        return c

    lax.fori_loop(0, n_items, issue, 0, unroll=unroll)


def _wait_items(n_items, rpi, src_hbm, dst_vmem, sem):
    pltpu.make_async_copy(src_hbm.at[pl.ds(0, n_items * rpi), :], dst_vmem, sem).wait()


def _expert_kernel(tok_ref, first_ref, u_hbm, wg_hbm, wu_hbm, wd_hbm, y_hbm,
                   xbuf_ref, xs_ref, obuf_ref, stage_gu_ref, stage_d_ref, wg_ref, wu_ref, wd_ref,
                   gsems, osems, wsems, *, layer, tb, nseg, nrow, n_tiles):
    e = pl.program_id(0)
    n_exp = pl.num_programs(0)
    n_used = first_ref[n_exp]
    fc = wg_ref.shape[1] // WEIGHT_CHUNKS

    def weight_copies(expert, c):
        cols = pl.ds(c * fc, fc)
        return (pltpu.make_async_copy(wg_hbm.at[layer, expert, :, cols], stage_gu_ref.at[c, 0], wsems.at[c]),
                pltpu.make_async_copy(wu_hbm.at[layer, expert, :, cols], stage_gu_ref.at[c, 1], wsems.at[c]),
                pltpu.make_async_copy(wd_hbm.at[layer, expert, cols, :], stage_d_ref.at[c], wsems.at[c]))

    def start_gather(k, unroll):
        s = k % GATHER_SLOTS
        tile = jnp.minimum(k, n_used - 1)
        _start_items(tok_ref, tile * tb, tb, nseg, u_hbm, xbuf_ref.at[s], gsems.at[s], unroll)

    def wait_gather(k):
        s = k % GATHER_SLOTS
        _wait_items(tb, nseg, u_hbm, xbuf_ref.at[s], gsems.at[s])

    def out_copy(tile, s):
        rows = pl.ds(pl.multiple_of(tile * (tb * nrow), tb * nrow), tb * nrow)
        return pltpu.make_async_copy(obuf_ref.at[s], y_hbm.at[rows, :], osems.at[s])

    @pl.when(e == 0)
    def _():
        for c in range(WEIGHT_CHUNKS):
            for cp in weight_copies(0, c):
                cp.start()
        for k in range(GATHER_SLOTS - 1):
            start_gather(k, False)

    for c in range(WEIGHT_CHUNKS):
        for cp in weight_copies(e, c):
            cp.wait()
        wg_ref[:, c * fc:(c + 1) * fc] = stage_gu_ref[c, 0].astype(BF16)
        wu_ref[:, c * fc:(c + 1) * fc] = stage_gu_ref[c, 1].astype(BF16)
        wd_ref[c * fc:(c + 1) * fc, :] = stage_d_ref[c].astype(BF16)

        @pl.when(e + 1 < n_exp)
        def _():
            for cp in weight_copies(e + 1, c):
                cp.start()

    def tile_body(i, carry):
        slot = i % GATHER_SLOTS
        oslot = i % 2

        @pl.when(i >= 2)
        def _():
            out_copy(i - 2, oslot).wait()

        wait_gather(i)
        start_gather(i + GATHER_SLOTS - 1, True)
        for s in range(nseg):
            w = xbuf_ref[slot, pl.ds(s, tb, stride=nseg), :]
            lo = lax.bitcast_convert_type(lax.shift_left(w, jnp.uint32(16)), F32)
            hi = lax.bitcast_convert_type(w & jnp.uint32(HI_HALF), F32)
            xs_ref[:, 2 * s * LANES:(2 * s + 1) * LANES] = lo.astype(BF16)
            xs_ref[:, (2 * s + 1) * LANES:(2 * s + 2) * LANES] = hi.astype(BF16)
        x = xs_ref[...]
        g = _dot(x, wg_ref[...])
        u = _dot(x, wu_ref[...])
        h = (g * jax.nn.sigmoid(g) * u).astype(BF16)
        y = _dot(h, wd_ref[...])
        for s in range(nrow):
            obuf_ref[oslot, pl.ds(s, tb, stride=nrow), :] = y[:, s * LANES:(s + 1) * LANES]
        out_copy(i, oslot).start()
        return carry

    lax.fori_loop(first_ref[e], first_ref[e + 1], tile_body, 0)

    @pl.when(e == n_exp - 1)
    def _():
        for k in range(GATHER_SLOTS - 1):
            wait_gather(n_used + k)
        for back in (1, 2):

            @pl.when(n_used >= back)
            def _():
                out_copy(n_used - back, (n_used - back) % 2).wait()

        obuf_ref[0] = jnp.zeros(obuf_ref.shape[1:], obuf_ref.dtype)

        def zero_tile(i, carry):
            cp = out_copy(i, 0)
            cp.start()
            cp.wait()
            return carry

        lax.fori_loop(n_used, n_tiles, zero_tile, 0)


def expert_mlp(pad_tok, first_tile, u2p, wg, wu, wd, layer, tb):
    P = pad_tok.shape[0]
    E, D, F = wg.shape[1:]
    nseg = D // (2 * LANES)
    nrow = D // LANES
    fc = F // WEIGHT_CHUNKS
    any_spec = pl.BlockSpec(memory_space=pl.ANY)
    grid_spec = pltpu.PrefetchScalarGridSpec(
        num_scalar_prefetch=2,
        grid=(E,),
        in_specs=[any_spec, any_spec, any_spec, any_spec],
        out_specs=any_spec,
        scratch_shapes=[
            pltpu.VMEM((GATHER_SLOTS, tb * nseg, LANES), jnp.uint32),
            pltpu.VMEM((tb, D), BF16),
            pltpu.VMEM((2, tb * nrow, LANES), F32),
            pltpu.VMEM((WEIGHT_CHUNKS, 2, D, fc), F32),
            pltpu.VMEM((WEIGHT_CHUNKS, fc, D), F32),
            pltpu.VMEM((D, F), BF16),
            pltpu.VMEM((D, F), BF16),
            pltpu.VMEM((F, D), BF16),
            pltpu.SemaphoreType.DMA((GATHER_SLOTS,)),
            pltpu.SemaphoreType.DMA((2,)),
            pltpu.SemaphoreType.DMA((WEIGHT_CHUNKS,)),
        ],
    )
    return pl.pallas_call(
        functools.partial(_expert_kernel, layer=layer, tb=tb, nseg=nseg, nrow=nrow, n_tiles=P // tb),
        grid_spec=grid_spec,
        out_shape=jax.ShapeDtypeStruct((P * nrow, LANES), F32),
        compiler_params=_cparams(("arbitrary",)),
        name="expert",
    )(pad_tok, first_tile, u2p, wg, wu, wd)


def _final_kernel(dest_ref, x1_ref, route_ref, g2_ref, lng_ref, lnb_ref, yb_hbm, o_ref, buf_ref, sems,
                  *, alpha, tm, m, nrow):
    i = pl.program_id(0)
    slot = i % GATHER_SLOTS
    last = pl.num_programs(0) - 1

    def start(n, unroll):
        s = n % GATHER_SLOTS
        tile = jnp.minimum(n, last)
        for k in range(TOP_K):
            _start_items(dest_ref, k * m + tile * tm, tm, nrow, yb_hbm, buf_ref.at[s, k], sems.at[s], unroll)

    def wait(n):
        s = n % GATHER_SLOTS
        for k in range(TOP_K):
            _wait_items(tm, nrow, yb_hbm, buf_ref.at[s, k], sems.at[s])

    @pl.when(i == 0)
    def _():
        for n in range(GATHER_SLOTS - 1):
            start(n, False)

    wait(i)
    start(i + GATHER_SLOTS - 1, True)

    def expert_out(k):
        chunks = [buf_ref[slot, k, pl.ds(s, tm, stride=nrow), :] for s in range(nrow)]
        return jnp.concatenate(chunks, axis=1)

    route = route_ref[...]
    w1 = route[:, ROUTE_W1:ROUTE_W1 + 1]
    w2 = route[:, ROUTE_W2:ROUTE_W2 + 1]
    y = w1 * expert_out(0) + w2 * expert_out(1)
    h = alpha * x1_ref[...] + (1.0 + g2_ref[...]) * y
    o_ref[...] = _layer_norm(h, lng_ref[...], lnb_ref[...])

    @pl.when(i == last)
    def _():
        for n in range(1, GATHER_SLOTS):
            wait(i + n)


def combine_ln(dest_flat, x1, route, g2, ln_g, ln_b, yb, seq, alpha, tm=256):
    M, D = x1.shape
    bpb = seq // tm
    nrow = D // LANES
    grid_spec = pltpu.PrefetchScalarGridSpec(
        num_scalar_prefetch=1,
        grid=(M // tm,),
        in_specs=[
            pl.BlockSpec((tm, D), lambda i, d: (i, 0)),
            pl.BlockSpec((tm, LANES), lambda i, d: (i, 0)),
            pl.BlockSpec((None, 1, D), lambda i, d: (i // bpb, 0, 0)),
            pl.BlockSpec((1, D), lambda i, d: (0, 0)),
            pl.BlockSpec((1, D), lambda i, d: (0, 0)),
            pl.BlockSpec(memory_space=pl.ANY),
        ],
        out_specs=pl.BlockSpec((tm, D), lambda i, d: (i, 0)),
        scratch_shapes=[pltpu.VMEM((GATHER_SLOTS, TOP_K, tm * nrow, LANES), F32),
                        pltpu.SemaphoreType.DMA((GATHER_SLOTS,))],
    )
    return pl.pallas_call(
        functools.partial(_final_kernel, alpha=alpha, tm=tm, m=M, nrow=nrow),
        grid_spec=grid_spec,
        out_shape=jax.ShapeDtypeStruct((M, D), F32),
        compiler_params=_cparams(("arbitrary",)),
        name="final",
    )(dest_flat, x1, route, g2, ln_g.reshape(1, D), ln_b.reshape(1, D), yb)


def moe_layout(route, cnt, tb):
    M = route.shape[0]
    n_assign = M * TOP_K
    counts = cnt[0, :N_EXPERTS].astype(jnp.int32)
    padded = ((counts + tb - 1) // tb) * tb
    pends = jnp.cumsum(padded)
    pstarts = pends - padded
    n_tiles = (n_assign + N_EXPERTS * (tb - 1) + tb - 1) // tb
    e = route[:, ROUTE_E1:ROUTE_E2 + 1].astype(jnp.int32)
    rank = route[:, ROUTE_R1:ROUTE_R2 + 1].astype(jnp.int32)
    dest = pstarts[e] + rank
    dest_flat = dest.T.reshape(-1)
    tok = jnp.tile(jnp.arange(M, dtype=jnp.int32), TOP_K)
    pad_tok = jnp.zeros((n_tiles * tb,), jnp.int32).at[dest_flat].set(tok)
    first_tile = (jnp.concatenate([pstarts, pends[-1:]]) // tb).astype(jnp.int32)
    return dest_flat, pad_tok, first_tile


def _forward(x, c, w_ada, b_ada, w_in, diff_lambda, diff_subln_g, rel_bias, w_o, ln_g, ln_b, w_group, b_group,
             w_router, b_router, w_gate, w_up, w_down, *, diff_tile, sb_tile, moe_tile, ada_tn):
    B, S, D = x.shape
    depth = w_ada.shape[0]
    M = B * S
    alpha = (2 * depth) ** 0.25

    c8 = jnp.zeros((8, D), F32).at[:B].set(c)
    mod = ada_modulation(c8, w_ada, b_ada, ada_tn)[:, :B]
    mod = mod.reshape(depth, B, 6, 1, D)
    bias_tiles = rel_bias_tiles(rel_bias, diff_tile)
    w_in_b, w_o_b = w_in.astype(BF16), w_o.astype(BF16)

    x2d = x.reshape(M, D)
    for l in range(depth):
        sh1, sc1, g1, sh2, sc2, g2 = (mod[l, :, n] for n in range(6))
        proj = in_projection(x2d, sc1, sh1, w_in_b, l, S)

        lam_init = 0.8 - 0.6 * math.exp(-0.3 * l)
        lp = diff_lambda[l].astype(F32)
        lam = jnp.exp(jnp.sum(lp[0] * lp[1])) - jnp.exp(jnp.sum(lp[2] * lp[3])) + lam_init
        a_out = diff_attention(proj, lam, bias_tiles, diff_subln_g[l], B, S, lam_init, diff_tile)
        b_out = sb_attention(proj, B, S, sb_tile)

        w_rt = jnp.zeros((D, LANES), F32).at[:, :N_GROUPS].set(w_group[l])
        w_rt = w_rt.at[:, N_GROUPS:N_GROUPS + N_EXPERTS].set(w_router[l])
        b_rt = jnp.zeros((1, LANES), F32).at[0, :N_GROUPS].set(b_group[l])
        b_rt = b_rt.at[0, N_GROUPS:N_GROUPS + N_EXPERTS].set(b_router[l])
        x1, u2, route, cnt = mix_ln_route(a_out, b_out, w_o_b, l, x2d, g1, sc2, sh2,
                                          ln_g[l, 0], ln_b[l, 0], w_rt, b_rt, S, alpha)

        dest_flat, pad_tok, first_tile = moe_layout(route, cnt, moe_tile)
        yb = expert_mlp(pad_tok, first_tile, u2, w_gate, w_up, w_down, l, moe_tile)
        x2d = combine_ln(dest_flat, x1, route, g2, ln_g[l, 1], ln_b[l, 1], yb, S, alpha)
    return x2d.reshape(B, S, D)


def kernel(x, c, w_ada, b_ada, w_in, diff_lambda, diff_subln_g, rel_bias, w_o, ln_g, ln_b, w_group, b_group,
           w_router, b_router, w_gate, w_up, w_down):
    return _forward(x, c, w_ada, b_ada, w_in, diff_lambda, diff_subln_g, rel_bias, w_o, ln_g, ln_b, w_group,
                    b_group, w_router, b_router, w_gate, w_up, w_down,
                    diff_tile=512, sb_tile=256, moe_tile=256, ada_tn=1024)
```

```python
import functools
import math

import jax
import jax.numpy as jnp
import numpy as np
from jax import lax
from jax.experimental import pallas as pl
from jax.experimental.pallas import tpu as pltpu

F32 = jnp.float32
BF16 = jnp.bfloat16

N_DIFF_HEADS = 8
N_SB_HEADS = 8
HEAD_DIM = 128
DIFF_QK_DIM = 64
NUM_BUCKETS = 32
MAX_DISTANCE = 128
N_GROUPS = 4
EXPERTS_PER_GROUP = 8
N_EXPERTS = N_GROUPS * EXPERTS_PER_GROUP
TOP_K = 2
LN_EPS = 1e-5
SUBLN_EPS = 1e-5

LANES = 128
MASK_VALUE = -1e30
SB_EXIT = -104.0
DIFF_ROW_CHUNK = 128
SB_HEADS_PER_STEP = 4
HI_HALF = 0xFFFF0000
GATHER_SLOTS = 3
WEIGHT_CHUNKS = 4
VMEM_LIMIT = 56 * 1024 * 1024


def _cparams(sem):
    return pltpu.CompilerParams(dimension_semantics=sem, vmem_limit_bytes=VMEM_LIMIT)


def _dot(a, b):
    return jnp.dot(a, b, preferred_element_type=F32)


def _dot_nt(a, b):
    return lax.dot_general(a, b, (((1,), (1,)), ((), ())), preferred_element_type=F32)


def _split(a):
    hi = a.astype(BF16)
    lo = (a - hi.astype(F32)).astype(BF16)
    return hi, lo


def _dot3(a, b):
    a_hi, a_lo = _split(a)
    b_hi, b_lo = _split(b)
    return _dot(a_hi, b_hi) + (_dot(a_hi, b_lo) + _dot(a_lo, b_hi))


def _layer_norm(h, g, b):
    mu = jnp.mean(h, axis=-1, keepdims=True)
    d = h - mu
    var = jnp.mean(d * d, axis=-1, keepdims=True)
    return d * lax.rsqrt(var + LN_EPS) * g + b


def _ada_kernel(c_ref, w_ref, b_ref, o_ref):
    c = c_ref[...]
    s = c * jax.nn.sigmoid(c)
    o_ref[...] = _dot3(s, w_ref[...]) + b_ref[...]


def ada_modulation(c8, w_ada, b_ada, tn=1024):
    L, D, N = w_ada.shape
    return pl.pallas_call(
        _ada_kernel,
        grid=(L, N // tn),
        in_specs=[
            pl.BlockSpec((8, D), lambda l, j: (0, 0)),
            pl.BlockSpec((None, D, tn), lambda l, j: (l, 0, j)),
            pl.BlockSpec((None, 1, tn), lambda l, j: (l, 0, j)),
        ],
        out_specs=pl.BlockSpec((None, 8, tn), lambda l, j: (l, 0, j)),
        out_shape=jax.ShapeDtypeStruct((L, 8, N), F32),
        compiler_params=_cparams(("arbitrary", "arbitrary")),
        name="ada",
    )(c8, w_ada, b_ada.reshape(L, 1, N))


def _inproj_kernel(x_ref, sc_ref, sh_ref, w_ref, o_ref, u_ref):
    @pl.when(pl.program_id(1) == 0)
    def _():
        u_ref[...] = (x_ref[...] * (1.0 + sc_ref[...]) + sh_ref[...]).astype(BF16)

    o_ref[...] = _dot(u_ref[...], w_ref[...]).astype(o_ref.dtype)


def in_projection(x2d, sc, sh, w_bf16, layer, seq, tm=1024, tn=1536):
    M, D = x2d.shape
    N = w_bf16.shape[2]
    tm = min(tm, seq)
    tn = min(tn, N)
    bpb = seq // tm
    return pl.pallas_call(
        _inproj_kernel,
        grid=(M // tm, N // tn),
        in_specs=[
            pl.BlockSpec((tm, D), lambda i, j: (i, 0)),
            pl.BlockSpec((None, 1, D), lambda i, j: (i // bpb, 0, 0)),
            pl.BlockSpec((None, 1, D), lambda i, j: (i // bpb, 0, 0)),
            pl.BlockSpec((None, D, tn), lambda i, j: (layer, 0, j)),
        ],
        out_specs=pl.BlockSpec((tm, tn), lambda i, j: (i, j)),
        out_shape=jax.ShapeDtypeStruct((M, N), BF16),
        scratch_shapes=[pltpu.VMEM((tm, D), BF16)],
        compiler_params=_cparams(("arbitrary", "arbitrary")),
        name="inproj",
    )(x2d, sc, sh, w_bf16)


def _t5_bucket_np(rel):
    n = np.maximum(rel, 0)
    max_exact = NUM_BUCKETS // 2
    ratio = np.maximum(n, 1).astype(np.float32) / np.float32(max_exact)
    large = max_exact + (np.log(ratio).astype(np.float32) / np.float32(math.log(MAX_DISTANCE / max_exact))
                         * np.float32(NUM_BUCKETS - max_exact)).astype(np.int32)
    large = np.minimum(large, NUM_BUCKETS - 1)
    return np.where(n < max_exact, n, large)


def _bucket_thresholds():
    buckets = _t5_bucket_np(np.arange(2 * MAX_DISTANCE))
    assert np.all(np.diff(buckets) >= 0) and buckets[MAX_DISTANCE] == NUM_BUCKETS - 1
    return [int(np.argmax(buckets >= k)) for k in range(NUM_BUCKETS)]


def _relbias_kernel(tab_ref, o_ref, *, t, n_heads, thresholds):
    h = pl.program_id(0)
    row = lax.broadcasted_iota(jnp.int32, (t, t), 0)
    col = lax.broadcasted_iota(jnp.int32, (t, t), 1)
    for blk in range(2):
        rel = row - col + blk * t
        for mp in range(2):
            entry = lambda k: tab_ref[(k * n_heads + h) * 2 + mp]
            far = entry(NUM_BUCKETS - 1)
            val = jnp.full((t, t), entry(0) - far, F32)
            for k in range(1, NUM_BUCKETS):
                val = jnp.where(rel >= thresholds[k], entry(k) - far, val)
            if blk == 0:
                val = jnp.where(rel < 0, MASK_VALUE, val)
            o_ref[blk, mp * t:(mp + 1) * t, :] = val


def rel_bias_tiles(rel_bias, t):
    assert t >= MAX_DISTANCE
    n_heads = rel_bias.shape[1]
    kern = functools.partial(_relbias_kernel, t=t, n_heads=n_heads, thresholds=_bucket_thresholds())
    grid_spec = pltpu.PrefetchScalarGridSpec(
        num_scalar_prefetch=1,
        grid=(n_heads,),
        in_specs=[],
        out_specs=pl.BlockSpec((None, 2, 2 * t, t), lambda h, tab: (h, 0, 0, 0)),
    )
    return pl.pallas_call(
        kern,
        grid_spec=grid_spec,
        out_shape=jax.ShapeDtypeStruct((n_heads, 2, 2 * t, t), F32),
        compiler_params=_cparams(("arbitrary",)),
        name="relbias",
    )(rel_bias.reshape(-1).astype(F32))


def _diff_kernel(lam_ref, q_ref, k_ref, v_ref, bias_ref, g_ref, o_ref, qs_ref, qn_ref, s_ref, m_ref, acc_ref,
                 *, t, rc, nq, out_scale):
    ones = jnp.ones((t, HEAD_DIM), BF16)

    def keys(j):
        return pl.ds(pl.multiple_of(j * t, t), t)

    def stack_queries(qi, dst_ref):
        q = q_ref[keys(qi), :] * BF16(DIFF_QK_DIM ** -0.5)
        lane = lax.broadcasted_iota(jnp.int32, q.shape, 1)
        zero = jnp.zeros_like(q)
        dst_ref[0:t, :] = jnp.where(lane < DIFF_QK_DIM, q, zero)
        dst_ref[t:2 * t, :] = jnp.where(lane >= DIFF_QK_DIM, q, zero)

    def step(j, bias_blk, next_queries_ref, next_j):
        vb = v_ref[keys(j), :]
        vb1 = jnp.concatenate([vb, ones], axis=1)
        kb_next = k_ref[keys(next_j), :]
        diagonal = bias_blk == 0
        for r in range(2 * t // rc):
            rows = pl.ds(r * rc, rc)
            nc = (r % (t // rc) + 1) * rc if diagonal else t
            s = s_ref[rows, :nc]
            s_ref[rows, :] = _dot_nt(next_queries_ref[rows, :], kb_next)
            if bias_blk is not None:
                s = s + bias_ref[bias_blk, rows, :nc]
            m_old = m_ref[rows, :]
            m_new = jnp.maximum(m_old, jnp.max(s, axis=1, keepdims=True))
            p = jnp.exp(s - jnp.concatenate([m_new] * (nc // LANES), axis=1)).astype(BF16)
            alpha = jnp.exp(m_old - m_new)
            acc_ref[rows, :] = jnp.concatenate([alpha, alpha], axis=1) * acc_ref[rows, :] + _dot(p, vb1[:nc])
            m_ref[rows, :] = m_new

    stack_queries(0, qs_ref)
    kb0 = k_ref[keys(0), :]
    for r in range(2 * t // rc):
        rows = pl.ds(r * rc, rc)
        s_ref[rows, :] = _dot_nt(qs_ref[rows, :], kb0)

    def reset():
        m_ref[...] = jnp.full(m_ref.shape, MASK_VALUE, F32)
        acc_ref[...] = jnp.zeros(acc_ref.shape, F32)

    def diagonal_step_and_output(qi):
        stack_queries(jnp.minimum(qi + 1, nq - 1), qn_ref)
        step(qi, 0, qn_ref, 0)
        acc = acc_ref[...]
        ratio = acc[:, :HEAD_DIM] / acc[:, HEAD_DIM:]
        o = ratio[:t] - lam_ref[0] * ratio[t:]
        o = o * lax.rsqrt(jnp.mean(o * o, axis=-1, keepdims=True) + SUBLN_EPS)
        o_ref[keys(qi), :] = (o * g_ref[...] * out_scale).astype(o_ref.dtype)
        qs_ref[...] = qn_ref[...]

    reset()
    diagonal_step_and_output(0)

    def tile_body(qi, carry):
        reset()
        n_far = qi - 1

        def far_pair(p, c):
            step(2 * p, None, qs_ref, 2 * p + 1)
            step(2 * p + 1, None, qs_ref, 2 * p + 2)
            return c

        lax.fori_loop(0, n_far // 2, far_pair, 0)

        @pl.when(n_far % 2 == 1)
        def _():
            step(n_far - 1, None, qs_ref, n_far)

        step(qi - 1, 1, qs_ref, qi)
        diagonal_step_and_output(qi)
        return carry

    lax.fori_loop(1, nq, tile_body, 0)


def diff_attention(proj, lam, bias_tiles, subln_g, batch, seq, lam_init, t):
    M = proj.shape[0]
    H = N_DIFF_HEADS
    nq = seq // t
    kern = functools.partial(_diff_kernel, t=t, rc=min(DIFF_ROW_CHUNK, t), nq=nq, out_scale=1.0 - lam_init)
    grid_spec = pltpu.PrefetchScalarGridSpec(
        num_scalar_prefetch=1,
        grid=(H, batch),
        in_specs=[
            pl.BlockSpec((seq, HEAD_DIM), lambda h, b, s: (b, h)),
            pl.BlockSpec((seq, HEAD_DIM), lambda h, b, s: (b, H + h)),
            pl.BlockSpec((seq, HEAD_DIM), lambda h, b, s: (b, 2 * H + h)),
            pl.BlockSpec((None, 2, 2 * t, t), lambda h, b, s: (h, 0, 0, 0)),
            pl.BlockSpec((1, HEAD_DIM), lambda h, b, s: (0, 0)),
        ],
        out_specs=pl.BlockSpec((seq, HEAD_DIM), lambda h, b, s: (b, h)),
        scratch_shapes=[
            pltpu.VMEM((2 * t, HEAD_DIM), BF16),
            pltpu.VMEM((2 * t, HEAD_DIM), BF16),
            pltpu.VMEM((2 * t, t), F32),
            pltpu.VMEM((2 * t, LANES), F32),
            pltpu.VMEM((2 * t, 2 * HEAD_DIM), F32),
        ],
    )
    return pl.pallas_call(
        kern,
        grid_spec=grid_spec,
        out_shape=jax.ShapeDtypeStruct((M, H * HEAD_DIM), BF16),
        compiler_params=_cparams(("arbitrary", "arbitrary")),
        name="diffattn",
    )(lam.reshape(1).astype(F32), proj, proj, proj, bias_tiles, subln_g.reshape(1, HEAD_DIM))


def _sb_kernel(q_ref, k_ref, v_ref, o_ref, carry_ref, acc_ref, *, t, hp, scale):
    qi = pl.program_id(2)

    def keys(j):
        return pl.ds(pl.multiple_of(j * t, t), t)

    def upper_ones(n):
        row = lax.broadcasted_iota(jnp.int32, (n, n), 0)
        col = lax.broadcasted_iota(jnp.int32, (n, n), 1)
        return jnp.where(row > col, 1.0, 0.0).astype(BF16)

    def sticks(z, valid):
        lp = jnp.log(1.0 + jnp.exp(-jnp.abs(z)))
        log_keep = jnp.minimum(-z, 0.0) - lp
        log_beta = jnp.minimum(z, 0.0) - lp
        if valid is not None:
            log_keep = jnp.where(valid, log_keep, 0.0)
        return log_keep, log_beta

    prev = jnp.maximum(qi - 1, 0)
    qrow = lax.broadcasted_iota(jnp.int32, (t, 2 * t), 0)
    kcol = lax.broadcasted_iota(jnp.int32, (t, 2 * t), 1)
    n_prev = jnp.where(qi > 0, t, 0)
    valid = (kcol < n_prev) | ((kcol >= t) & (kcol - t < qrow))
    upper = upper_ones(t)
    for hh in range(hp):
        cols = slice(hh * HEAD_DIM, (hh + 1) * HEAD_DIM)
        kw = jnp.concatenate([k_ref[keys(prev), cols], k_ref[keys(qi), cols]], axis=0)
        vw = jnp.concatenate([v_ref[keys(prev), cols], v_ref[keys(qi), cols]], axis=0)
        log_keep, log_beta = sticks(_dot_nt(q_ref[:, cols], kw) * scale, valid)
        hi, lo = _split(log_keep)
        stacked = jnp.concatenate([hi[:, :t], lo[:, :t], hi[:, t:], lo[:, t:]], axis=0)
        within = _dot(stacked, upper)
        diag_total = jnp.sum(log_keep[:, t:], axis=1, keepdims=True)
        suffix = jnp.concatenate([within[0:t] + within[t:2 * t] + diag_total,
                                  within[2 * t:3 * t] + within[3 * t:4 * t]], axis=1)
        a = jnp.where(valid, jnp.exp(log_beta + suffix), 0.0)
        acc_ref[:, cols] = _dot(a.astype(BF16), vw)
        prev_total = jnp.sum(log_keep[:, :t], axis=1, keepdims=True)
        carry_ref[hh] = jnp.broadcast_to(prev_total + diag_total, (t, LANES))

    def step(j):
        for hh in range(hp):
            cols = slice(hh * HEAD_DIM, (hh + 1) * HEAD_DIM)
            log_keep, log_beta = sticks(_dot_nt(q_ref[:, cols], k_ref[keys(j), cols]) * scale, None)
            hi, lo = _split(log_keep)
            suffix = _dot(hi, upper) + _dot(lo, upper)
            carry = carry_ref[hh]
            a = jnp.exp(log_beta + suffix + jnp.concatenate([carry] * (t // LANES), axis=1))
            acc_ref[:, cols] += _dot(a.astype(BF16), v_ref[keys(j), cols])
            carry_ref[hh] = carry + jnp.sum(log_keep, axis=1, keepdims=True)

    def cond(state):
        n, live = state
        return jnp.logical_and(n < qi - 1, live > SB_EXIT)

    def body(state):
        n, _ = state
        step(qi - 2 - n)
        return n + 1, jnp.max(carry_ref[...])

    lax.while_loop(cond, body, (jnp.int32(0), jnp.max(carry_ref[...])))
    o_ref[...] = acc_ref[...].astype(o_ref.dtype)


def sb_attention(proj, batch, seq, t, hp=SB_HEADS_PER_STEP):
    M = proj.shape[0]
    H = N_SB_HEADS
    base = 3 * N_DIFF_HEADS // hp
    nq = seq // t
    w = hp * HEAD_DIM
    kern = functools.partial(_sb_kernel, t=t, hp=hp, scale=HEAD_DIM ** -0.5)
    return pl.pallas_call(
        kern,
        grid=(H // hp, batch, nq),
        in_specs=[
            pl.BlockSpec((t, w), lambda h, b, qi: (b * nq + qi, base + h)),
            pl.BlockSpec((seq, w), lambda h, b, qi: (b, base + H // hp + h)),
            pl.BlockSpec((seq, w), lambda h, b, qi: (b, base + 2 * (H // hp) + h)),
        ],
        out_specs=pl.BlockSpec((t, w), lambda h, b, qi: (b * nq + qi, h)),
        out_shape=jax.ShapeDtypeStruct((M, H * HEAD_DIM), BF16),
        scratch_shapes=[pltpu.VMEM((hp, t, LANES), F32), pltpu.VMEM((t, w), F32)],
        compiler_params=_cparams(("arbitrary", "arbitrary", "arbitrary")),
        name="sbattn",
    )(proj, proj, proj)


ROUTE_E1, ROUTE_E2, ROUTE_W1, ROUTE_W2, ROUTE_R1, ROUTE_R2 = range(6)


def _mix_kernel(a_ref, b_ref, wo_ref, x_ref, g1_ref, sc2_ref, sh2_ref, lng_ref, lnb_ref, wrt_ref, brt_ref,
                x1_ref, u2_ref, route_ref, cnt_ref, carry_ref, wsplit_ref, *, alpha, half):
    i = pl.program_id(0)

    @pl.when(i == 0)
    def _():
        carry_ref[...] = jnp.zeros(carry_ref.shape, F32)
        w_hi, w_lo = _split(wrt_ref[...])
        wsplit_ref[...] = jnp.concatenate([w_hi, w_lo], axis=1)

    mix = _dot(a_ref[...], wo_ref[pl.ds(0, half), :]) + _dot(b_ref[...], wo_ref[pl.ds(half, half), :])
    h = alpha * x_ref[...] + (1.0 + g1_ref[...]) * mix
    x1 = _layer_norm(h, lng_ref[...], lnb_ref[...])
    x1_ref[...] = x1
    u2 = x1 * (1.0 + sc2_ref[...]) + sh2_ref[...]
    nseg = u2.shape[1] // (2 * LANES)
    for s in range(nseg):
        lo = u2[:, 2 * s * LANES:(2 * s + 1) * LANES].astype(BF16).astype(F32)
        hi = u2[:, (2 * s + 1) * LANES:(2 * s + 2) * LANES].astype(BF16).astype(F32)
        lo_bits = lax.shift_right_logical(lax.bitcast_convert_type(lo, jnp.uint32), jnp.uint32(16))
        hi_bits = lax.bitcast_convert_type(hi, jnp.uint32) & jnp.uint32(HI_HALF)
        u2_ref[pl.ds(s, u2.shape[0], stride=nseg), :] = lo_bits | hi_bits

    u_hi, u_lo = _split(u2)
    hh_hl = _dot(u_hi, wsplit_ref[...])
    lh = _dot(u_lo, wsplit_ref[:, :LANES])
    logits = hh_hl[:, :LANES] + (hh_hl[:, LANES:] + lh) + brt_ref[...]
    tm = logits.shape[0]
    lane = lax.broadcasted_iota(jnp.int32, (tm, LANES), 1).astype(F32)
    gmask = lane < N_GROUPS
    gl = jnp.where(gmask, logits, MASK_VALUE)
    gmax = jnp.max(gl, axis=1, keepdims=True)
    gidx = jnp.min(jnp.where(gmask & (gl == gmax), lane, float(LANES)), axis=1, keepdims=True)
    gsum = jnp.sum(jnp.where(gmask, jnp.exp(gl - gmax), 0.0), axis=1, keepdims=True)
    gp = 1.0 / gsum

    lo = N_GROUPS + gidx * EXPERTS_PER_GROUP
    emask = (lane >= lo) & (lane < lo + EXPERTS_PER_GROUP)
    el = jnp.where(emask, logits, MASK_VALUE)
    v1 = jnp.max(el, axis=1, keepdims=True)
    i1 = jnp.min(jnp.where(emask & (el == v1), lane, float(LANES)), axis=1, keepdims=True)
    emask2 = emask & (lane != i1)
    el2 = jnp.where(emask2, logits, MASK_VALUE)
    v2 = jnp.max(el2, axis=1, keepdims=True)
    i2 = jnp.min(jnp.where(emask2 & (el2 == v2), lane, float(LANES)), axis=1, keepdims=True)
    tt = jnp.exp(v2 - v1)
    w1 = gp / (1.0 + tt)
    w2 = gp * tt / (1.0 + tt)
    e1 = i1 - N_GROUPS
    e2 = i2 - N_GROUPS

    oh1 = lane == e1
    oh2 = lane == e2
    oh1b = jnp.where(oh1, 1.0, 0.0).astype(BF16)
    oh2b = jnp.where(oh2, 1.0, 0.0).astype(BF16)
    row = lax.broadcasted_iota(jnp.int32, (tm, tm), 0)
    col = lax.broadcasted_iota(jnp.int32, (tm, tm), 1)
    lower = jnp.where(col < row, 1.0, 0.0).astype(BF16)
    c1 = jnp.sum(jnp.where(oh1, 1.0, 0.0), axis=0, keepdims=True)
    c2 = jnp.sum(jnp.where(oh2, 1.0, 0.0), axis=0, keepdims=True)
    carry = carry_ref[...]
    r1 = _dot(lower, oh1b) + carry
    r2 = _dot(lower, oh2b) + (carry + c1)
    rank1 = jnp.sum(jnp.where(oh1, r1, 0.0), axis=1, keepdims=True)
    rank2 = jnp.sum(jnp.where(oh2, r2, 0.0), axis=1, keepdims=True)
    carry = carry + c1 + c2
    carry_ref[...] = carry
    cnt_ref[...] = jnp.broadcast_to(carry, cnt_ref.shape)

    out = jnp.zeros((tm, LANES), F32)
    for ln, val in ((ROUTE_E1, e1), (ROUTE_E2, e2), (ROUTE_W1, w1), (ROUTE_W2, w2),
                    (ROUTE_R1, rank1), (ROUTE_R2, rank2)):
        out = jnp.where(lane == float(ln), val, out)
    route_ref[...] = out


def mix_ln_route(a_out, b_out, wo_bf16, layer, x2d, g1, sc2, sh2, ln_g, ln_b, w_rt, b_rt, seq, alpha, tm=256):
    M, D = x2d.shape
    half = a_out.shape[1]
    bpb = seq // tm
    nseg = D // (2 * LANES)
    row = lambda i: (i // bpb, 0, 0)
    full = lambda i: (0, 0)
    kern = functools.partial(_mix_kernel, alpha=alpha, half=half)
    return pl.pallas_call(
        kern,
        grid=(M // tm,),
        in_specs=[
            pl.BlockSpec((tm, half), lambda i: (i, 0)),
            pl.BlockSpec((tm, half), lambda i: (i, 0)),
            pl.BlockSpec((None, 2 * half, D), lambda i: (layer, 0, 0)),
            pl.BlockSpec((tm, D), lambda i: (i, 0)),
            pl.BlockSpec((None, 1, D), row),
            pl.BlockSpec((None, 1, D), row),
            pl.BlockSpec((None, 1, D), row),
            pl.BlockSpec((1, D), full),
            pl.BlockSpec((1, D), full),
            pl.BlockSpec((D, LANES), full),
            pl.BlockSpec((1, LANES), full),
        ],
        out_specs=[
            pl.BlockSpec((tm, D), lambda i: (i, 0)),
            pl.BlockSpec((tm * nseg, LANES), lambda i: (i, 0)),
            pl.BlockSpec((tm, LANES), lambda i: (i, 0)),
            pl.BlockSpec((8, LANES), full),
        ],
        out_shape=[
            jax.ShapeDtypeStruct((M, D), F32),
            jax.ShapeDtypeStruct((M * nseg, LANES), jnp.uint32),
            jax.ShapeDtypeStruct((M, LANES), F32),
            jax.ShapeDtypeStruct((8, LANES), F32),
        ],
        scratch_shapes=[pltpu.VMEM((1, LANES), F32), pltpu.VMEM((D, 2 * LANES), BF16)],
        compiler_params=_cparams(("arbitrary",)),
        name="mix",
    )(a_out, b_out, wo_bf16, x2d, g1, sc2, sh2, ln_g.reshape(1, D), ln_b.reshape(1, D), w_rt, b_rt)


def _item_copy(src_hbm, dst_vmem, src_item, dst_item, rpi, sem):
    src = src_hbm.at[pl.ds(pl.multiple_of(src_item * rpi, rpi), rpi), :]
    dst = dst_vmem.at[pl.ds(pl.multiple_of(dst_item * rpi, rpi), rpi), :]
    return pltpu.make_async_copy(src, dst, sem)


def _start_items(idx_ref, base, n_items, rpi, src_hbm, dst_vmem, sem, unroll=False):
    def issue(r, c):
        _item_copy(src_hbm, dst_vmem, idx_ref[base + r], r, rpi, sem).start()
        return c

    lax.fori_loop(0, n_items, issue, 0, unroll=unroll)


def _wait_items(n_items, rpi, src_hbm, dst_vmem, sem):
    pltpu.make_async_copy(src_hbm.at[pl.ds(0, n_items * rpi), :], dst_vmem, sem).wait()


def _expert_kernel(tok_ref, first_ref, u_hbm, wg_hbm, wu_hbm, wd_hbm, y_hbm,
                   xbuf_ref, xs_ref, obuf_ref, stage_gu_ref, stage_d_ref, wg_ref, wu_ref, wd_ref,
                   gsems, osems, wsems, *, layer, tb, nseg, nrow, n_tiles):
    e = pl.program_id(0)
    n_exp = pl.num_programs(0)
    n_used = first_ref[n_exp]
    fc = wg_ref.shape[1] // WEIGHT_CHUNKS

    def weight_copies(expert, c):
        cols = pl.ds(c * fc, fc)
        return (pltpu.make_async_copy(wg_hbm.at[layer, expert, :, cols], stage_gu_ref.at[c, 0], wsems.at[c]),
                pltpu.make_async_copy(wu_hbm.at[layer, expert, :, cols], stage_gu_ref.at[c, 1], wsems.at[c]),
                pltpu.make_async_copy(wd_hbm.at[layer, expert, cols, :], stage_d_ref.at[c], wsems.at[c]))

    def start_gather(k, unroll):
        s = k % GATHER_SLOTS
        tile = jnp.minimum(k, n_used - 1)
        _start_items(tok_ref, tile * tb, tb, nseg, u_hbm, xbuf_ref.at[s], gsems.at[s], unroll)

    def wait_gather(k):
        s = k % GATHER_SLOTS
        _wait_items(tb, nseg, u_hbm, xbuf_ref.at[s], gsems.at[s])

    def out_copy(tile, s):
        rows = pl.ds(pl.multiple_of(tile * (tb * nrow), tb * nrow), tb * nrow)
        return pltpu.make_async_copy(obuf_ref.at[s], y_hbm.at[rows, :], osems.at[s])

    @pl.when(e == 0)
    def _():
        for c in range(WEIGHT_CHUNKS):
            for cp in weight_copies(0, c):
                cp.start()
        for k in range(GATHER_SLOTS - 1):
            start_gather(k, False)

    for c in range(WEIGHT_CHUNKS):
        for cp in weight_copies(e, c):
            cp.wait()
        wg_ref[:, c * fc:(c + 1) * fc] = stage_gu_ref[c, 0].astype(BF16)
        wu_ref[:, c * fc:(c + 1) * fc] = stage_gu_ref[c, 1].astype(BF16)
        wd_ref[c * fc:(c + 1) * fc, :] = stage_d_ref[c].astype(BF16)

        @pl.when(e + 1 < n_exp)
        def _():
            for cp in weight_copies(e + 1, c):
                cp.start()

    def tile_body(i, carry):
        slot = i % GATHER_SLOTS
        oslot = i % 2

        @pl.when(i >= 2)
        def _():
            out_copy(i - 2, oslot).wait()

        wait_gather(i)
        start_gather(i + GATHER_SLOTS - 1, True)
        for s in range(nseg):
            w = xbuf_ref[slot, pl.ds(s, tb, stride=nseg), :]
            lo = lax.bitcast_convert_type(lax.shift_left(w, jnp.uint32(16)), F32)
            hi = lax.bitcast_convert_type(w & jnp.uint32(HI_HALF), F32)
            xs_ref[:, 2 * s * LANES:(2 * s + 1) * LANES] = lo.astype(BF16)
            xs_ref[:, (2 * s + 1) * LANES:(2 * s + 2) * LANES] = hi.astype(BF16)
        x = xs_ref[...]
        g = _dot(x, wg_ref[...])
        u = _dot(x, wu_ref[...])
        h = (g * jax.nn.sigmoid(g) * u).astype(BF16)
        y = _dot(h, wd_ref[...])
        for s in range(nrow):
            obuf_ref[oslot, pl.ds(s, tb, stride=nrow), :] = y[:, s * LANES:(s + 1) * LANES]
        out_copy(i, oslot).start()
        return carry

    lax.fori_loop(first_ref[e], first_ref[e + 1], tile_body, 0)

    @pl.when(e == n_exp - 1)
    def _():
        for k in range(GATHER_SLOTS - 1):
            wait_gather(n_used + k)
        for back in (1, 2):

            @pl.when(n_used >= back)
            def _():
                out_copy(n_used - back, (n_used - back) % 2).wait()

        obuf_ref[0] = jnp.zeros(obuf_ref.shape[1:], obuf_ref.dtype)

        def zero_tile(i, carry):
            cp = out_copy(i, 0)
            cp.start()
            cp.wait()
            return carry

        lax.fori_loop(n_used, n_tiles, zero_tile, 0)


def expert_mlp(pad_tok, first_tile, u2p, wg, wu, wd, layer, tb):
    P = pad_tok.shape[0]
    E, D, F = wg.shape[1:]
    nseg = D // (2 * LANES)
    nrow = D // LANES
    fc = F // WEIGHT_CHUNKS
    any_spec = pl.BlockSpec(memory_space=pl.ANY)
    grid_spec = pltpu.PrefetchScalarGridSpec(
        num_scalar_prefetch=2,
        grid=(E,),
        in_specs=[any_spec, any_spec, any_spec, any_spec],
        out_specs=any_spec,
        scratch_shapes=[
            pltpu.VMEM((GATHER_SLOTS, tb * nseg, LANES), jnp.uint32),
            pltpu.VMEM((tb, D), BF16),
            pltpu.VMEM((2, tb * nrow, LANES), F32),
            pltpu.VMEM((WEIGHT_CHUNKS, 2, D, fc), F32),
            pltpu.VMEM((WEIGHT_CHUNKS, fc, D), F32),
            pltpu.VMEM((D, F), BF16),
            pltpu.VMEM((D, F), BF16),
            pltpu.VMEM((F, D), BF16),
            pltpu.SemaphoreType.DMA((GATHER_SLOTS,)),
            pltpu.SemaphoreType.DMA((2,)),
            pltpu.SemaphoreType.DMA((WEIGHT_CHUNKS,)),
        ],
    )
    return pl.pallas_call(
        functools.partial(_expert_kernel, layer=layer, tb=tb, nseg=nseg, nrow=nrow, n_tiles=P // tb),
        grid_spec=grid_spec,
        out_shape=jax.ShapeDtypeStruct((P * nrow, LANES), F32),
        compiler_params=_cparams(("arbitrary",)),
        name="expert",
    )(pad_tok, first_tile, u2p, wg, wu, wd)


def _final_kernel(dest_ref, x1_ref, route_ref, g2_ref, lng_ref, lnb_ref, yb_hbm, o_ref, buf_ref, sems,
                  *, alpha, tm, m, nrow):
    i = pl.program_id(0)
    slot = i % GATHER_SLOTS
    last = pl.num_programs(0) - 1

    def start(n, unroll):
        s = n % GATHER_SLOTS
        tile = jnp.minimum(n, last)
        for k in range(TOP_K):
            _start_items(dest_ref, k * m + tile * tm, tm, nrow, yb_hbm, buf_ref.at[s, k], sems.at[s], unroll)

    def wait(n):
        s = n % GATHER_SLOTS
        for k in range(TOP_K):
            _wait_items(tm, nrow, yb_hbm, buf_ref.at[s, k], sems.at[s])

    @pl.when(i == 0)
    def _():
        for n in range(GATHER_SLOTS - 1):
            start(n, False)

    wait(i)
    start(i + GATHER_SLOTS - 1, True)

    def expert_out(k):
        chunks = [buf_ref[slot, k, pl.ds(s, tm, stride=nrow), :] for s in range(nrow)]
        return jnp.concatenate(chunks, axis=1)

    route = route_ref[...]
    w1 = route[:, ROUTE_W1:ROUTE_W1 + 1]
    w2 = route[:, ROUTE_W2:ROUTE_W2 + 1]
    y = w1 * expert_out(0) + w2 * expert_out(1)
    h = alpha * x1_ref[...] + (1.0 + g2_ref[...]) * y
    o_ref[...] = _layer_norm(h, lng_ref[...], lnb_ref[...])

    @pl.when(i == last)
    def _():
        for n in range(1, GATHER_SLOTS):
            wait(i + n)


def combine_ln(dest_flat, x1, route, g2, ln_g, ln_b, yb, seq, alpha, tm=256):
    M, D = x1.shape
    bpb = seq // tm
    nrow = D // LANES
    grid_spec = pltpu.PrefetchScalarGridSpec(
        num_scalar_prefetch=1,
        grid=(M // tm,),
        in_specs=[
            pl.BlockSpec((tm, D), lambda i, d: (i, 0)),
            pl.BlockSpec((tm, LANES), lambda i, d: (i, 0)),
            pl.BlockSpec((None, 1, D), lambda i, d: (i // bpb, 0, 0)),
            pl.BlockSpec((1, D), lambda i, d: (0, 0)),
            pl.BlockSpec((1, D), lambda i, d: (0, 0)),
            pl.BlockSpec(memory_space=pl.ANY),
        ],
        out_specs=pl.BlockSpec((tm, D), lambda i, d: (i, 0)),
        scratch_shapes=[pltpu.VMEM((GATHER_SLOTS, TOP_K, tm * nrow, LANES), F32),
                        pltpu.SemaphoreType.DMA((GATHER_SLOTS,))],
    )
    return pl.pallas_call(
        functools.partial(_final_kernel, alpha=alpha, tm=tm, m=M, nrow=nrow),
        grid_spec=grid_spec,
        out_shape=jax.ShapeDtypeStruct((M, D), F32),
        compiler_params=_cparams(("arbitrary",)),
        name="final",
    )(dest_flat, x1, route, g2, ln_g.reshape(1, D), ln_b.reshape(1, D), yb)


def moe_layout(route, cnt, tb):
    M = route.shape[0]
    n_assign = M * TOP_K
    counts = cnt[0, :N_EXPERTS].astype(jnp.int32)
    padded = ((counts + tb - 1) // tb) * tb
    pends = jnp.cumsum(padded)
    pstarts = pends - padded
    n_tiles = (n_assign + N_EXPERTS * (tb - 1) + tb - 1) // tb
    e = route[:, ROUTE_E1:ROUTE_E2 + 1].astype(jnp.int32)
    rank = route[:, ROUTE_R1:ROUTE_R2 + 1].astype(jnp.int32)
    dest = pstarts[e] + rank
    dest_flat = dest.T.reshape(-1)
    tok = jnp.tile(jnp.arange(M, dtype=jnp.int32), TOP_K)
    pad_tok = jnp.zeros((n_tiles * tb,), jnp.int32).at[dest_flat].set(tok)
    first_tile = (jnp.concatenate([pstarts, pends[-1:]]) // tb).astype(jnp.int32)
    return dest_flat, pad_tok, first_tile


def _forward(x, c, w_ada, b_ada, w_in, diff_lambda, diff_subln_g, rel_bias, w_o, ln_g, ln_b, w_group, b_group,
             w_router, b_router, w_gate, w_up, w_down, *, diff_tile, sb_tile, moe_tile, ada_tn):
    B, S, D = x.shape
    depth = w_ada.shape[0]
    M = B * S
    alpha = (2 * depth) ** 0.25

    c8 = jnp.zeros((8, D), F32).at[:B].set(c)
    mod = ada_modulation(c8, w_ada, b_ada, ada_tn)[:, :B]
    mod = mod.reshape(depth, B, 6, 1, D)
    bias_tiles = rel_bias_tiles(rel_bias, diff_tile)
    w_in_b, w_o_b = w_in.astype(BF16), w_o.astype(BF16)

    x2d = x.reshape(M, D)
    for l in range(depth):
        sh1, sc1, g1, sh2, sc2, g2 = (mod[l, :, n] for n in range(6))
        proj = in_projection(x2d, sc1, sh1, w_in_b, l, S)

        lam_init = 0.8 - 0.6 * math.exp(-0.3 * l)
        lp = diff_lambda[l].astype(F32)
        lam = jnp.exp(jnp.sum(lp[0] * lp[1])) - jnp.exp(jnp.sum(lp[2] * lp[3])) + lam_init
        a_out = diff_attention(proj, lam, bias_tiles, diff_subln_g[l], B, S, lam_init, diff_tile)
        b_out = sb_attention(proj, B, S, sb_tile)

        w_rt = jnp.zeros((D, LANES), F32).at[:, :N_GROUPS].set(w_group[l])
        w_rt = w_rt.at[:, N_GROUPS:N_GROUPS + N_EXPERTS].set(w_router[l])
        b_rt = jnp.zeros((1, LANES), F32).at[0, :N_GROUPS].set(b_group[l])
        b_rt = b_rt.at[0, N_GROUPS:N_GROUPS + N_EXPERTS].set(b_router[l])
        x1, u2, route, cnt = mix_ln_route(a_out, b_out, w_o_b, l, x2d, g1, sc2, sh2,
                                          ln_g[l, 0], ln_b[l, 0], w_rt, b_rt, S, alpha)

        dest_flat, pad_tok, first_tile = moe_layout(route, cnt, moe_tile)
        yb = expert_mlp(pad_tok, first_tile, u2, w_gate, w_up, w_down, l, moe_tile)
        x2d = combine_ln(dest_flat, x1, route, g2, ln_g[l, 1], ln_b[l, 1], yb, S, alpha)
    return x2d.reshape(B, S, D)


def kernel(x, c, w_ada, b_ada, w_in, diff_lambda, diff_subln_g, rel_bias, w_o, ln_g, ln_b, w_group, b_group,
           w_router, b_router, w_gate, w_up, w_down):
    return _forward(x, c, w_ada, b_ada, w_in, diff_lambda, diff_subln_g, rel_bias, w_o, ln_g, ln_b, w_group,
                    b_group, w_router, b_router, w_gate, w_up, w_down,
                    diff_tile=512, sb_tile=256, moe_tile=256, ada_tn=1024)
```

```python
import functools
import math

import jax
import jax.numpy as jnp
import numpy as np
from jax import lax
from jax.experimental import pallas as pl
from jax.experimental.pallas import tpu as pltpu

F32 = jnp.float32
BF16 = jnp.bfloat16

N_DIFF_HEADS = 8
N_SB_HEADS = 8
HEAD_DIM = 128
DIFF_QK_DIM = 64
NUM_BUCKETS = 32
MAX_DISTANCE = 128
N_GROUPS = 4
EXPERTS_PER_GROUP = 8
N_EXPERTS = N_GROUPS * EXPERTS_PER_GROUP
TOP_K = 2
LN_EPS = 1e-5
SUBLN_EPS = 1e-5

LANES = 128
MASK_VALUE = -1e30
LOG2E = math.log2(math.e)
SB_EXIT_LOG2 = 104.0 * LOG2E
DIFF_ROW_CHUNK = 128
SB_HEADS_PER_STEP = 4
HI_HALF = 0xFFFF0000
GATHER_SLOTS = 3
WEIGHT_CHUNKS = 4
VMEM_LIMIT = 56 * 1024 * 1024


def _cparams(sem):
    return pltpu.CompilerParams(dimension_semantics=sem, vmem_limit_bytes=VMEM_LIMIT)


def _dot(a, b):
    return jnp.dot(a, b, preferred_element_type=F32)


def _dot_nt(a, b):
    return lax.dot_general(a, b, (((1,), (1,)), ((), ())), preferred_element_type=F32)


def _split(a):
    hi = a.astype(BF16)
    lo = (a - hi.astype(F32)).astype(BF16)
    return hi, lo


def _dot3(a, b):
    a_hi, a_lo = _split(a)
    b_hi, b_lo = _split(b)
    return _dot(a_hi, b_hi) + (_dot(a_hi, b_lo) + _dot(a_lo, b_hi))


def _layer_norm(h, g, b):
    mu = jnp.mean(h, axis=-1, keepdims=True)
    d = h - mu
    var = jnp.mean(d * d, axis=-1, keepdims=True)
    return d * lax.rsqrt(var + LN_EPS) * g + b


def _ada_kernel(c_ref, w_ref, b_ref, o_ref):
    c = c_ref[...]
    s = c * jax.nn.sigmoid(c)
    o_ref[...] = _dot3(s, w_ref[...]) + b_ref[...]


def ada_modulation(c8, w_ada, b_ada, tn=1024):
    L, D, N = w_ada.shape
    return pl.pallas_call(
        _ada_kernel,
        grid=(L, N // tn),
        in_specs=[
            pl.BlockSpec((8, D), lambda l, j: (0, 0)),
            pl.BlockSpec((None, D, tn), lambda l, j: (l, 0, j)),
            pl.BlockSpec((None, 1, tn), lambda l, j: (l, 0, j)),
        ],
        out_specs=pl.BlockSpec((None, 8, tn), lambda l, j: (l, 0, j)),
        out_shape=jax.ShapeDtypeStruct((L, 8, N), F32),
        compiler_params=_cparams(("arbitrary", "arbitrary")),
        name="ada",
    )(c8, w_ada, b_ada.reshape(L, 1, N))


def _inproj_kernel(x_ref, sc_ref, sh_ref, w_ref, o_ref, u_ref):
    @pl.when(pl.program_id(1) == 0)
    def _():
        u_ref[...] = (x_ref[...] * (1.0 + sc_ref[...]) + sh_ref[...]).astype(BF16)

    o_ref[...] = _dot(u_ref[...], w_ref[...]).astype(o_ref.dtype)


def in_projection(x2d, sc, sh, w_bf16, layer, seq, tm=1024, tn=1536):
    M, D = x2d.shape
    N = w_bf16.shape[2]
    tm = min(tm, seq)
    tn = min(tn, N)
    bpb = seq // tm
    return pl.pallas_call(
        _inproj_kernel,
        grid=(M // tm, N // tn),
        in_specs=[
            pl.BlockSpec((tm, D), lambda i, j: (i, 0)),
            pl.BlockSpec((None, 1, D), lambda i, j: (i // bpb, 0, 0)),
            pl.BlockSpec((None, 1, D), lambda i, j: (i // bpb, 0, 0)),
            pl.BlockSpec((None, D, tn), lambda i, j: (layer, 0, j)),
        ],
        out_specs=pl.BlockSpec((tm, tn), lambda i, j: (i, j)),
        out_shape=jax.ShapeDtypeStruct((M, N), BF16),
        scratch_shapes=[pltpu.VMEM((tm, D), BF16)],
        compiler_params=_cparams(("arbitrary", "arbitrary")),
        name="inproj",
    )(x2d, sc, sh, w_bf16)


def _t5_bucket_np(rel):
    n = np.maximum(rel, 0)
    max_exact = NUM_BUCKETS // 2
    ratio = np.maximum(n, 1).astype(np.float32) / np.float32(max_exact)
    large = max_exact + (np.log(ratio).astype(np.float32) / np.float32(math.log(MAX_DISTANCE / max_exact))
                         * np.float32(NUM_BUCKETS - max_exact)).astype(np.int32)
    large = np.minimum(large, NUM_BUCKETS - 1)
    return np.where(n < max_exact, n, large)


def _bucket_thresholds():
    buckets = _t5_bucket_np(np.arange(2 * MAX_DISTANCE))
    assert np.all(np.diff(buckets) >= 0) and buckets[MAX_DISTANCE] == NUM_BUCKETS - 1
    return [int(np.argmax(buckets >= k)) for k in range(NUM_BUCKETS)]


def _relbias_kernel(tab_ref, o_ref, *, t, n_heads, thresholds):
    h = pl.program_id(0)
    row = lax.broadcasted_iota(jnp.int32, (t, t), 0)
    col = lax.broadcasted_iota(jnp.int32, (t, t), 1)
    for blk in range(2):
        rel = row - col + blk * t
        for mp in range(2):
            entry = lambda k: tab_ref[(k * n_heads + h) * 2 + mp]
            far = entry(NUM_BUCKETS - 1)
            val = jnp.full((t, t), entry(0) - far, F32)
            for k in range(1, NUM_BUCKETS):
                val = jnp.where(rel >= thresholds[k], entry(k) - far, val)
            if blk == 0:
                val = jnp.where(rel < 0, MASK_VALUE, val)
            o_ref[blk, mp * t:(mp + 1) * t, :] = val


def rel_bias_tiles(rel_bias, t):
    assert t >= MAX_DISTANCE
    n_heads = rel_bias.shape[1]
    kern = functools.partial(_relbias_kernel, t=t, n_heads=n_heads, thresholds=_bucket_thresholds())
    grid_spec = pltpu.PrefetchScalarGridSpec(
        num_scalar_prefetch=1,
        grid=(n_heads,),
        in_specs=[],
        out_specs=pl.BlockSpec((None, 2, 2 * t, t), lambda h, tab: (h, 0, 0, 0)),
    )
    return pl.pallas_call(
        kern,
        grid_spec=grid_spec,
        out_shape=jax.ShapeDtypeStruct((n_heads, 2, 2 * t, t), F32),
        compiler_params=_cparams(("arbitrary",)),
        name="relbias",
    )(rel_bias.reshape(-1).astype(F32))


def _diff_kernel(lam_ref, q_ref, k_ref, v_ref, bias_ref, g_ref, o_ref, qs_ref, qn_ref, s_ref, m_ref, acc_ref,
                 *, t, rc, nq, out_scale):
    ones = jnp.ones((t, HEAD_DIM), BF16)

    def keys(j):
        return pl.ds(pl.multiple_of(j * t, t), t)

    def stack_queries(qi, dst_ref):
        q = q_ref[keys(qi), :] * BF16(DIFF_QK_DIM ** -0.5)
        lane = lax.broadcasted_iota(jnp.int32, q.shape, 1)
        zero = jnp.zeros_like(q)
        dst_ref[0:t, :] = jnp.where(lane < DIFF_QK_DIM, q, zero)
        dst_ref[t:2 * t, :] = jnp.where(lane >= DIFF_QK_DIM, q, zero)

    def step(j, bias_blk, next_queries_ref, next_j):
        vb = v_ref[keys(j), :]
        vb1 = jnp.concatenate([vb, ones], axis=1)
        kb_next = k_ref[keys(next_j), :]
        diagonal = bias_blk == 0
        bias_reach = rc + MAX_DISTANCE
        for r in range(2 * t // rc):
            rows = pl.ds(r * rc, rc)
            nc = (r % (t // rc) + 1) * rc if diagonal else t
            s = s_ref[rows, :nc]
            s_ref[rows, :] = _dot_nt(next_queries_ref[rows, :], kb_next)
            if bias_blk is not None:
                c = r % (t // rc)
                if diagonal or c == 0:
                    b0 = max(nc - bias_reach, 0)
                    biased = s[:, b0:] + bias_ref[bias_blk, rows, b0:nc]
                    s = biased if b0 == 0 else jnp.concatenate([s[:, :b0], biased], axis=1)
            m_old = m_ref[rows, :]
            m_new = jnp.maximum(m_old, jnp.max(s, axis=1, keepdims=True))
            p = jnp.exp(s - jnp.concatenate([m_new] * (nc // LANES), axis=1)).astype(BF16)
            alpha = jnp.exp(m_old - m_new)
            acc_ref[rows, :] = jnp.concatenate([alpha, alpha], axis=1) * acc_ref[rows, :] + _dot(p, vb1[:nc])
            m_ref[rows, :] = m_new

    stack_queries(0, qs_ref)
    kb0 = k_ref[keys(0), :]
    for r in range(2 * t // rc):
        rows = pl.ds(r * rc, rc)
        s_ref[rows, :] = _dot_nt(qs_ref[rows, :], kb0)

    def reset():
        m_ref[...] = jnp.full(m_ref.shape, MASK_VALUE, F32)
        acc_ref[...] = jnp.zeros(acc_ref.shape, F32)

    def diagonal_step_and_output(qi):
        stack_queries(jnp.minimum(qi + 1, nq - 1), qn_ref)
        step(qi, 0, qn_ref, 0)
        acc = acc_ref[...]
        ratio = acc[:, :HEAD_DIM] / acc[:, HEAD_DIM:]
        o = ratio[:t] - lam_ref[0] * ratio[t:]
        o = o * lax.rsqrt(jnp.mean(o * o, axis=-1, keepdims=True) + SUBLN_EPS)
        o_ref[keys(qi), :] = (o * g_ref[...] * out_scale).astype(o_ref.dtype)
        qs_ref[...] = qn_ref[...]

    reset()
    diagonal_step_and_output(0)

    def tile_body(qi, carry):
        reset()
        n_far = qi - 1

        def far_pair(p, c):
            step(2 * p, None, qs_ref, 2 * p + 1)
            step(2 * p + 1, None, qs_ref, 2 * p + 2)
            return c

        lax.fori_loop(0, n_far // 2, far_pair, 0)

        @pl.when(n_far % 2 == 1)
        def _():
            step(n_far - 1, None, qs_ref, n_far)

        step(qi - 1, 1, qs_ref, qi)
        diagonal_step_and_output(qi)
        return carry

    lax.fori_loop(1, nq, tile_body, 0)


def diff_attention(proj, lam, bias_tiles, subln_g, batch, seq, lam_init, t):
    M = proj.shape[0]
    H = N_DIFF_HEADS
    nq = seq // t
    rc = min(DIFF_ROW_CHUNK, t)
    assert rc % LANES == 0 and rc >= MAX_DISTANCE and MAX_DISTANCE % LANES == 0
    kern = functools.partial(_diff_kernel, t=t, rc=rc, nq=nq, out_scale=1.0 - lam_init)
    grid_spec = pltpu.PrefetchScalarGridSpec(
        num_scalar_prefetch=1,
        grid=(H, batch),
        in_specs=[
            pl.BlockSpec((seq, HEAD_DIM), lambda h, b, s: (b, h)),
            pl.BlockSpec((seq, HEAD_DIM), lambda h, b, s: (b, H + h)),
            pl.BlockSpec((seq, HEAD_DIM), lambda h, b, s: (b, 2 * H + h)),
            pl.BlockSpec((None, 2, 2 * t, t), lambda h, b, s: (h, 0, 0, 0)),
            pl.BlockSpec((1, HEAD_DIM), lambda h, b, s: (0, 0)),
        ],
        out_specs=pl.BlockSpec((seq, HEAD_DIM), lambda h, b, s: (b, h)),
        scratch_shapes=[
            pltpu.VMEM((2 * t, HEAD_DIM), BF16),
            pltpu.VMEM((2 * t, HEAD_DIM), BF16),
            pltpu.VMEM((2 * t, t), F32),
            pltpu.VMEM((2 * t, LANES), F32),
            pltpu.VMEM((2 * t, 2 * HEAD_DIM), F32),
        ],
    )
    return pl.pallas_call(
        kern,
        grid_spec=grid_spec,
        out_shape=jax.ShapeDtypeStruct((M, H * HEAD_DIM), BF16),
        compiler_params=_cparams(("arbitrary", "arbitrary")),
        name="diffattn",
    )(lam.reshape(1).astype(F32), proj, proj, proj, bias_tiles, subln_g.reshape(1, HEAD_DIM))


def _sb_kernel(q_ref, k_ref, v_ref, o_ref, carry_ref, acc_ref, *, t, hp, scale2):
    qi = pl.program_id(2)

    def keys(j):
        return pl.ds(pl.multiple_of(j * t, t), t)

    def upper_ones(n):
        row = lax.broadcasted_iota(jnp.int32, (n, n), 0)
        col = lax.broadcasted_iota(jnp.int32, (n, n), 1)
        return jnp.where(row > col, 1.0, 0.0).astype(BF16)

    def sticks(z2, valid):
        lp = jnp.log2(1.0 + jnp.exp2(-jnp.abs(z2)))
        drop = jnp.maximum(z2, 0.0) + lp
        log2_beta = z2 - drop
        if valid is not None:
            drop = jnp.where(valid, drop, 0.0)
        return drop, log2_beta

    prev = jnp.maximum(qi - 1, 0)
    qrow = lax.broadcasted_iota(jnp.int32, (t, 2 * t), 0)
    kcol = lax.broadcasted_iota(jnp.int32, (t, 2 * t), 1)
    n_prev = jnp.where(qi > 0, t, 0)
    valid = (kcol < n_prev) | ((kcol >= t) & (kcol - t < qrow))
    upper = upper_ones(t)
    for hh in range(hp):
        cols = slice(hh * HEAD_DIM, (hh + 1) * HEAD_DIM)
        kw = jnp.concatenate([k_ref[keys(prev), cols], k_ref[keys(qi), cols]], axis=0)
        vw = jnp.concatenate([v_ref[keys(prev), cols], v_ref[keys(qi), cols]], axis=0)
        drop, log2_beta = sticks(_dot_nt(q_ref[:, cols], kw) * scale2, valid)
        hi, lo = _split(drop)
        stacked = jnp.concatenate([hi[:, :t], lo[:, :t], hi[:, t:], lo[:, t:]], axis=0)
        within = _dot(stacked, upper)
        diag_total = jnp.sum(drop[:, t:], axis=1, keepdims=True)
        later = jnp.concatenate([within[0:t] + within[t:2 * t] + diag_total,
                                 within[2 * t:3 * t] + within[3 * t:4 * t]], axis=1)
        a = jnp.where(valid, jnp.exp2(log2_beta - later), 0.0)
        acc_ref[:, cols] = _dot(a.astype(BF16), vw)
        prev_total = jnp.sum(drop[:, :t], axis=1, keepdims=True)
        carry_ref[hh] = jnp.broadcast_to(prev_total + diag_total, (t, LANES))

    def step(j):
        for hh in range(hp):
            cols = slice(hh * HEAD_DIM, (hh + 1) * HEAD_DIM)
            drop, log2_beta = sticks(_dot_nt(q_ref[:, cols], k_ref[keys(j), cols]) * scale2, None)
            hi, lo = _split(drop)
            later = _dot(hi, upper) + _dot(lo, upper)
            carry = carry_ref[hh]
            a = jnp.exp2(log2_beta - later - jnp.concatenate([carry] * (t // LANES), axis=1))
            acc_ref[:, cols] += _dot(a.astype(BF16), v_ref[keys(j), cols])
            carry_ref[hh] = carry + jnp.sum(drop, axis=1, keepdims=True)

    def cond(state):
        n, live = state
        return jnp.logical_and(n < qi - 1, live < SB_EXIT_LOG2)

    def body(state):
        n, _ = state
        step(qi - 2 - n)
        return n + 1, jnp.min(carry_ref[...])

    lax.while_loop(cond, body, (jnp.int32(0), jnp.min(carry_ref[...])))
    o_ref[...] = acc_ref[...].astype(o_ref.dtype)


def sb_attention(proj, batch, seq, t, hp=SB_HEADS_PER_STEP):
    M = proj.shape[0]
    H = N_SB_HEADS
    base = 3 * N_DIFF_HEADS // hp
    nq = seq // t
    w = hp * HEAD_DIM
    kern = functools.partial(_sb_kernel, t=t, hp=hp, scale2=HEAD_DIM ** -0.5 * LOG2E)
    return pl.pallas_call(
        kern,
        grid=(H // hp, batch, nq),
        in_specs=[
            pl.BlockSpec((t, w), lambda h, b, qi: (b * nq + qi, base + h)),
            pl.BlockSpec((seq, w), lambda h, b, qi: (b, base + H // hp + h)),
            pl.BlockSpec((seq, w), lambda h, b, qi: (b, base + 2 * (H // hp) + h)),
        ],
        out_specs=pl.BlockSpec((t, w), lambda h, b, qi: (b * nq + qi, h)),
        out_shape=jax.ShapeDtypeStruct((M, H * HEAD_DIM), BF16),
        scratch_shapes=[pltpu.VMEM((hp, t, LANES), F32), pltpu.VMEM((t, w), F32)],
        compiler_params=_cparams(("arbitrary", "arbitrary", "arbitrary")),
        name="sbattn",
    )(proj, proj, proj)


ROUTE_E1, ROUTE_E2, ROUTE_W1, ROUTE_W2, ROUTE_R1, ROUTE_R2 = range(6)


def _mix_kernel(a_ref, b_ref, wo_ref, x_ref, g1_ref, sc2_ref, sh2_ref, lng_ref, lnb_ref, wrt_ref, brt_ref,
                x1_ref, u2_ref, route_ref, cnt_ref, carry_ref, wsplit_ref, *, alpha, half):
    i = pl.program_id(0)

    @pl.when(i == 0)
    def _():
        carry_ref[...] = jnp.zeros(carry_ref.shape, F32)
        w_hi, w_lo = _split(wrt_ref[...])
        wsplit_ref[...] = jnp.concatenate([w_hi, w_lo], axis=1)

    mix = _dot(a_ref[...], wo_ref[pl.ds(0, half), :]) + _dot(b_ref[...], wo_ref[pl.ds(half, half), :])
    h = alpha * x_ref[...] + (1.0 + g1_ref[...]) * mix
    x1 = _layer_norm(h, lng_ref[...], lnb_ref[...])
    x1_ref[...] = x1
    u2 = x1 * (1.0 + sc2_ref[...]) + sh2_ref[...]
    nseg = u2.shape[1] // (2 * LANES)
    for s in range(nseg):
        lo = u2[:, 2 * s * LANES:(2 * s + 1) * LANES].astype(BF16).astype(F32)
        hi = u2[:, (2 * s + 1) * LANES:(2 * s + 2) * LANES].astype(BF16).astype(F32)
        lo_bits = lax.shift_right_logical(lax.bitcast_convert_type(lo, jnp.uint32), jnp.uint32(16))
        hi_bits = lax.bitcast_convert_type(hi, jnp.uint32) & jnp.uint32(HI_HALF)
        u2_ref[pl.ds(s, u2.shape[0], stride=nseg), :] = lo_bits | hi_bits

    u_hi, u_lo = _split(u2)
    hh_hl = _dot(u_hi, wsplit_ref[...])
    lh = _dot(u_lo, wsplit_ref[:, :LANES])
    logits = hh_hl[:, :LANES] + (hh_hl[:, LANES:] + lh) + brt_ref[...]
    tm = logits.shape[0]
    lane = lax.broadcasted_iota(jnp.int32, (tm, LANES), 1).astype(F32)
    gmask = lane < N_GROUPS
    gl = jnp.where(gmask, logits, MASK_VALUE)
    gmax = jnp.max(gl, axis=1, keepdims=True)
    gidx = jnp.min(jnp.where(gmask & (gl == gmax), lane, float(LANES)), axis=1, keepdims=True)
    gsum = jnp.sum(jnp.where(gmask, jnp.exp(gl - gmax), 0.0), axis=1, keepdims=True)
    gp = 1.0 / gsum

    lo = N_GROUPS + gidx * EXPERTS_PER_GROUP
    emask = (lane >= lo) & (lane < lo + EXPERTS_PER_GROUP)
    el = jnp.where(emask, logits, MASK_VALUE)
    v1 = jnp.max(el, axis=1, keepdims=True)
    i1 = jnp.min(jnp.where(emask & (el == v1), lane, float(LANES)), axis=1, keepdims=True)
    emask2 = emask & (lane != i1)
    el2 = jnp.where(emask2, logits, MASK_VALUE)
    v2 = jnp.max(el2, axis=1, keepdims=True)
    i2 = jnp.min(jnp.where(emask2 & (el2 == v2), lane, float(LANES)), axis=1, keepdims=True)
    tt = jnp.exp(v2 - v1)
    w1 = gp / (1.0 + tt)
    w2 = gp * tt / (1.0 + tt)
    e1 = i1 - N_GROUPS
    e2 = i2 - N_GROUPS

    oh1 = lane == e1
    oh2 = lane == e2
    oh1b = jnp.where(oh1, 1.0, 0.0).astype(BF16)
    oh2b = jnp.where(oh2, 1.0, 0.0).astype(BF16)
    row = lax.broadcasted_iota(jnp.int32, (tm, tm), 0)
    col = lax.broadcasted_iota(jnp.int32, (tm, tm), 1)
    lower = jnp.where(col < row, 1.0, 0.0).astype(BF16)
    c1 = jnp.sum(jnp.where(oh1, 1.0, 0.0), axis=0, keepdims=True)
    c2 = jnp.sum(jnp.where(oh2, 1.0, 0.0), axis=0, keepdims=True)
    carry = carry_ref[...]
    r1 = _dot(lower, oh1b) + carry
    r2 = _dot(lower, oh2b) + (carry + c1)
    rank1 = jnp.sum(jnp.where(oh1, r1, 0.0), axis=1, keepdims=True)
    rank2 = jnp.sum(jnp.where(oh2, r2, 0.0), axis=1, keepdims=True)
    carry = carry + c1 + c2
    carry_ref[...] = carry
    cnt_ref[...] = jnp.broadcast_to(carry, cnt_ref.shape)

    out = jnp.zeros((tm, LANES), F32)
    for ln, val in ((ROUTE_E1, e1), (ROUTE_E2, e2), (ROUTE_W1, w1), (ROUTE_W2, w2),
                    (ROUTE_R1, rank1), (ROUTE_R2, rank2)):
        out = jnp.where(lane == float(ln), val, out)
    route_ref[...] = out


def mix_ln_route(a_out, b_out, wo_bf16, layer, x2d, g1, sc2, sh2, ln_g, ln_b, w_rt, b_rt, seq, alpha, tm=256):
    M, D = x2d.shape
    half = a_out.shape[1]
    bpb = seq // tm
    nseg = D // (2 * LANES)
    row = lambda i: (i // bpb, 0, 0)
    full = lambda i: (0, 0)
    kern = functools.partial(_mix_kernel, alpha=alpha, half=half)
    return pl.pallas_call(
        kern,
        grid=(M // tm,),
        in_specs=[
            pl.BlockSpec((tm, half), lambda i: (i, 0)),
            pl.BlockSpec((tm, half), lambda i: (i, 0)),
            pl.BlockSpec((None, 2 * half, D), lambda i: (layer, 0, 0)),
            pl.BlockSpec((tm, D), lambda i: (i, 0)),
            pl.BlockSpec((None, 1, D), row),
            pl.BlockSpec((None, 1, D), row),
            pl.BlockSpec((None, 1, D), row),
            pl.BlockSpec((1, D), full),
            pl.BlockSpec((1, D), full),
            pl.BlockSpec((D, LANES), full),
            pl.BlockSpec((1, LANES), full),
        ],
        out_specs=[
            pl.BlockSpec((tm, D), lambda i: (i, 0)),
            pl.BlockSpec((tm * nseg, LANES), lambda i: (i, 0)),
            pl.BlockSpec((tm, LANES), lambda i: (i, 0)),
            pl.BlockSpec((8, LANES), full),
        ],
        out_shape=[
            jax.ShapeDtypeStruct((M, D), F32),
            jax.ShapeDtypeStruct((M * nseg, LANES), jnp.uint32),
            jax.ShapeDtypeStruct((M, LANES), F32),
            jax.ShapeDtypeStruct((8, LANES), F32),
        ],
        scratch_shapes=[pltpu.VMEM((1, LANES), F32), pltpu.VMEM((D, 2 * LANES), BF16)],
        compiler_params=_cparams(("arbitrary",)),
        name="mix",
    )(a_out, b_out, wo_bf16, x2d, g1, sc2, sh2, ln_g.reshape(1, D), ln_b.reshape(1, D), w_rt, b_rt)


def _item_copy(src_hbm, dst_vmem, src_item, dst_item, rpi, sem):
    src = src_hbm.at[pl.ds(pl.multiple_of(src_item * rpi, rpi), rpi), :]
    dst = dst_vmem.at[pl.ds(pl.multiple_of(dst_item * rpi, rpi), rpi), :]
    return pltpu.make_async_copy(src, dst, sem)


def _start_items(idx_ref, base, n_items, rpi, src_hbm, dst_vmem, sem, unroll=False):
    def issue(r, c):
        _item_copy(src_hbm, dst_vmem, idx_ref[base + r], r, rpi, sem).start()
        return c

    lax.fori_loop(0, n_items, issue, 0, unroll=unroll)


def _wait_items(n_items, rpi, src_hbm, dst_vmem, sem):
    pltpu.make_async_copy(src_hbm.at[pl.ds(0, n_items * rpi), :], dst_vmem, sem).wait()


def _expert_kernel(tok_ref, first_ref, u_hbm, wg_hbm, wu_hbm, wd_hbm, y_hbm,
                   xbuf_ref, xs_ref, obuf_ref, stage_gu_ref, stage_d_ref, wg_ref, wu_ref, wd_ref,
                   gsems, osems, wsems, *, layer, tb, nseg, nrow, n_tiles):
    e = pl.program_id(0)
    n_exp = pl.num_programs(0)
    n_used = first_ref[n_exp]
    dc = wg_ref.shape[0] // WEIGHT_CHUNKS
    fc = wd_ref.shape[0] // WEIGHT_CHUNKS

    def weight_copies(expert, c):
        drows = pl.ds(c * dc, dc)
        frows = pl.ds(c * fc, fc)
        return (pltpu.make_async_copy(wg_hbm.at[layer, expert, drows, :], stage_gu_ref.at[c, 0], wsems.at[c]),
                pltpu.make_async_copy(wu_hbm.at[layer, expert, drows, :], stage_gu_ref.at[c, 1], wsems.at[c]),
                pltpu.make_async_copy(wd_hbm.at[layer, expert, frows, :], stage_d_ref.at[c], wsems.at[c]))

    def start_gather(k, unroll):
        s = k % GATHER_SLOTS
        tile = jnp.minimum(k, n_used - 1)
        _start_items(tok_ref, tile * tb, tb, nseg, u_hbm, xbuf_ref.at[s], gsems.at[s], unroll)

    def wait_gather(k):
        s = k % GATHER_SLOTS
        _wait_items(tb, nseg, u_hbm, xbuf_ref.at[s], gsems.at[s])

    def out_copy(tile, s):
        rows = pl.ds(pl.multiple_of(tile * (tb * nrow), tb * nrow), tb * nrow)
        return pltpu.make_async_copy(obuf_ref.at[s], y_hbm.at[rows, :], osems.at[s])

    @pl.when(e == 0)
    def _():
        for c in range(WEIGHT_CHUNKS):
            for cp in weight_copies(0, c):
                cp.start()
        for k in range(GATHER_SLOTS - 1):
            start_gather(k, False)

    for c in range(WEIGHT_CHUNKS):
        for cp in weight_copies(e, c):
            cp.wait()
        wg_ref[c * dc:(c + 1) * dc, :] = stage_gu_ref[c, 0].astype(BF16)
        wu_ref[c * dc:(c + 1) * dc, :] = stage_gu_ref[c, 1].astype(BF16)
        wd_ref[c * fc:(c + 1) * fc, :] = stage_d_ref[c].astype(BF16)

        @pl.when(e + 1 < n_exp)
        def _():
            for cp in weight_copies(e + 1, c):
                cp.start()

    def tile_body(i, carry):
        slot = i % GATHER_SLOTS
        oslot = i % 2

        @pl.when(i >= 2)
        def _():
            out_copy(i - 2, oslot).wait()

        wait_gather(i)
        start_gather(i + GATHER_SLOTS - 1, True)
        for s in range(nseg):
            w = xbuf_ref[slot, pl.ds(s, tb, stride=nseg), :]
            lo = lax.bitcast_convert_type(lax.shift_left(w, jnp.uint32(16)), F32)
            hi = lax.bitcast_convert_type(w & jnp.uint32(HI_HALF), F32)
            xs_ref[:, 2 * s * LANES:(2 * s + 1) * LANES] = lo.astype(BF16)
            xs_ref[:, (2 * s + 1) * LANES:(2 * s + 2) * LANES] = hi.astype(BF16)
        x = xs_ref[...]
        g = _dot(x, wg_ref[...])
        u = _dot(x, wu_ref[...])
        h = (g * jax.nn.sigmoid(g) * u).astype(BF16)
        y = _dot(h, wd_ref[...])
        for s in range(nrow):
            obuf_ref[oslot, pl.ds(s, tb, stride=nrow), :] = y[:, s * LANES:(s + 1) * LANES]
        out_copy(i, oslot).start()
        return carry

    lax.fori_loop(first_ref[e], first_ref[e + 1], tile_body, 0)

    @pl.when(e == n_exp - 1)
    def _():
        for k in range(GATHER_SLOTS - 1):
            wait_gather(n_used + k)
        for back in (1, 2):

            @pl.when(n_used >= back)
            def _():
                out_copy(n_used - back, (n_used - back) % 2).wait()

        obuf_ref[0] = jnp.zeros(obuf_ref.shape[1:], obuf_ref.dtype)

        def zero_tile(i, carry):
            cp = out_copy(i, 0)
            cp.start()
            cp.wait()
            return carry

        lax.fori_loop(n_used, n_tiles, zero_tile, 0)


def expert_mlp(pad_tok, first_tile, u2p, wg, wu, wd, layer, tb):
    P = pad_tok.shape[0]
    E, D, F = wg.shape[1:]
    nseg = D // (2 * LANES)
    nrow = D // LANES
    fc = F // WEIGHT_CHUNKS
    any_spec = pl.BlockSpec(memory_space=pl.ANY)
    grid_spec = pltpu.PrefetchScalarGridSpec(
        num_scalar_prefetch=2,
        grid=(E,),
        in_specs=[any_spec, any_spec, any_spec, any_spec],
        out_specs=any_spec,
        scratch_shapes=[
            pltpu.VMEM((GATHER_SLOTS, tb * nseg, LANES), jnp.uint32),
            pltpu.VMEM((tb, D), BF16),
            pltpu.VMEM((2, tb * nrow, LANES), F32),
            pltpu.VMEM((WEIGHT_CHUNKS, 2, D // WEIGHT_CHUNKS, F), F32),
            pltpu.VMEM((WEIGHT_CHUNKS, fc, D), F32),
            pltpu.VMEM((D, F), BF16),
            pltpu.VMEM((D, F), BF16),
            pltpu.VMEM((F, D), BF16),
            pltpu.SemaphoreType.DMA((GATHER_SLOTS,)),
            pltpu.SemaphoreType.DMA((2,)),
            pltpu.SemaphoreType.DMA((WEIGHT_CHUNKS,)),
        ],
    )
    return pl.pallas_call(
        functools.partial(_expert_kernel, layer=layer, tb=tb, nseg=nseg, nrow=nrow, n_tiles=P // tb),
        grid_spec=grid_spec,
        out_shape=jax.ShapeDtypeStruct((P * nrow, LANES), F32),
        compiler_params=_cparams(("arbitrary",)),
        name="expert",
    )(pad_tok, first_tile, u2p, wg, wu, wd)


def _final_kernel(dest_ref, x1_ref, route_ref, g2_ref, lng_ref, lnb_ref, yb_hbm, o_ref, buf_ref, sems,
                  *, alpha, tm, m, nrow):
    i = pl.program_id(0)
    slot = i % GATHER_SLOTS
    last = pl.num_programs(0) - 1

    def start(n, unroll):
        s = n % GATHER_SLOTS
        tile = jnp.minimum(n, last)
        for k in range(TOP_K):
            _start_items(dest_ref, k * m + tile * tm, tm, nrow, yb_hbm, buf_ref.at[s, k], sems.at[s], unroll)

    def wait(n):
        s = n % GATHER_SLOTS
        for k in range(TOP_K):
            _wait_items(tm, nrow, yb_hbm, buf_ref.at[s, k], sems.at[s])

    @pl.when(i == 0)
    def _():
        for n in range(GATHER_SLOTS - 1):
            start(n, False)

    wait(i)
    start(i + GATHER_SLOTS - 1, True)

    def expert_out(k):
        chunks = [buf_ref[slot, k, pl.ds(s, tm, stride=nrow), :] for s in range(nrow)]
        return jnp.concatenate(chunks, axis=1)

    route = route_ref[...]
    w1 = route[:, ROUTE_W1:ROUTE_W1 + 1]
    w2 = route[:, ROUTE_W2:ROUTE_W2 + 1]
    y = w1 * expert_out(0) + w2 * expert_out(1)
    h = alpha * x1_ref[...] + (1.0 + g2_ref[...]) * y
    o_ref[...] = _layer_norm(h, lng_ref[...], lnb_ref[...])

    @pl.when(i == last)
    def _():
        for n in range(1, GATHER_SLOTS):
            wait(i + n)


def combine_ln(dest_flat, x1, route, g2, ln_g, ln_b, yb, seq, alpha, tm=256):
    M, D = x1.shape
    bpb = seq // tm
    nrow = D // LANES
    grid_spec = pltpu.PrefetchScalarGridSpec(
        num_scalar_prefetch=1,
        grid=(M // tm,),
        in_specs=[
            pl.BlockSpec((tm, D), lambda i, d: (i, 0)),
            pl.BlockSpec((tm, LANES), lambda i, d: (i, 0)),
            pl.BlockSpec((None, 1, D), lambda i, d: (i // bpb, 0, 0)),
            pl.BlockSpec((1, D), lambda i, d: (0, 0)),
            pl.BlockSpec((1, D), lambda i, d: (0, 0)),
            pl.BlockSpec(memory_space=pl.ANY),
        ],
        out_specs=pl.BlockSpec((tm, D), lambda i, d: (i, 0)),
        scratch_shapes=[pltpu.VMEM((GATHER_SLOTS, TOP_K, tm * nrow, LANES), F32),
                        pltpu.SemaphoreType.DMA((GATHER_SLOTS,))],
    )
    return pl.pallas_call(
        functools.partial(_final_kernel, alpha=alpha, tm=tm, m=M, nrow=nrow),
        grid_spec=grid_spec,
        out_shape=jax.ShapeDtypeStruct((M, D), F32),
        compiler_params=_cparams(("arbitrary",)),
        name="final",
    )(dest_flat, x1, route, g2, ln_g.reshape(1, D), ln_b.reshape(1, D), yb)


def moe_layout(route, cnt, tb):
    M = route.shape[0]
    n_assign = M * TOP_K
    counts = cnt[0, :N_EXPERTS].astype(jnp.int32)
    padded = ((counts + tb - 1) // tb) * tb
    pends = jnp.cumsum(padded)
    pstarts = pends - padded
    n_tiles = (n_assign + N_EXPERTS * (tb - 1) + tb - 1) // tb
    e = route[:, ROUTE_E1:ROUTE_E2 + 1].astype(jnp.int32)
    rank = route[:, ROUTE_R1:ROUTE_R2 + 1].astype(jnp.int32)
    dest = pstarts[e] + rank
    dest_flat = dest.T.reshape(-1)
    tok = jnp.tile(jnp.arange(M, dtype=jnp.int32), TOP_K)
    pad_tok = jnp.zeros((n_tiles * tb,), jnp.int32).at[dest_flat].set(tok)
    first_tile = (jnp.concatenate([pstarts, pends[-1:]]) // tb).astype(jnp.int32)
    return dest_flat, pad_tok, first_tile


def _forward(x, c, w_ada, b_ada, w_in, diff_lambda, diff_subln_g, rel_bias, w_o, ln_g, ln_b, w_group, b_group,
             w_router, b_router, w_gate, w_up, w_down, *, diff_tile, sb_tile, moe_tile, ada_tn):
    B, S, D = x.shape
    depth = w_ada.shape[0]
    M = B * S
    alpha = (2 * depth) ** 0.25

    c8 = jnp.zeros((8, D), F32).at[:B].set(c)
    mod = ada_modulation(c8, w_ada, b_ada, ada_tn)[:, :B]
    mod = mod.reshape(depth, B, 6, 1, D)
    bias_tiles = rel_bias_tiles(rel_bias, diff_tile)
    w_in_b, w_o_b = w_in.astype(BF16), w_o.astype(BF16)

    x2d = x.reshape(M, D)
    for l in range(depth):
        sh1, sc1, g1, sh2, sc2, g2 = (mod[l, :, n] for n in range(6))
        proj = in_projection(x2d, sc1, sh1, w_in_b, l, S)

        lam_init = 0.8 - 0.6 * math.exp(-0.3 * l)
        lp = diff_lambda[l].astype(F32)
        lam = jnp.exp(jnp.sum(lp[0] * lp[1])) - jnp.exp(jnp.sum(lp[2] * lp[3])) + lam_init
        a_out = diff_attention(proj, lam, bias_tiles, diff_subln_g[l], B, S, lam_init, diff_tile)
        b_out = sb_attention(proj, B, S, sb_tile)

        w_rt = jnp.zeros((D, LANES), F32).at[:, :N_GROUPS].set(w_group[l])
        w_rt = w_rt.at[:, N_GROUPS:N_GROUPS + N_EXPERTS].set(w_router[l])
        b_rt = jnp.zeros((1, LANES), F32).at[0, :N_GROUPS].set(b_group[l])
        b_rt = b_rt.at[0, N_GROUPS:N_GROUPS + N_EXPERTS].set(b_router[l])
        x1, u2, route, cnt = mix_ln_route(a_out, b_out, w_o_b, l, x2d, g1, sc2, sh2,
                                          ln_g[l, 0], ln_b[l, 0], w_rt, b_rt, S, alpha)

        dest_flat, pad_tok, first_tile = moe_layout(route, cnt, moe_tile)
        yb = expert_mlp(pad_tok, first_tile, u2, w_gate, w_up, w_down, l, moe_tile)
        x2d = combine_ln(dest_flat, x1, route, g2, ln_g[l, 1], ln_b[l, 1], yb, S, alpha)
    return x2d.reshape(B, S, D)


def kernel(x, c, w_ada, b_ada, w_in, diff_lambda, diff_subln_g, rel_bias, w_o, ln_g, ln_b, w_group, b_group,
           w_router, b_router, w_gate, w_up, w_down):
    return _forward(x, c, w_ada, b_ada, w_in, diff_lambda, diff_subln_g, rel_bias, w_o, ln_g, ln_b, w_group,
                    b_group, w_router, b_router, w_gate, w_up, w_down,
                    diff_tile=512, sb_tile=256, moe_tile=256, ada_tn=1024)
```

```python
import functools
import math

import jax
import jax.numpy as jnp
import numpy as np
from jax import lax
from jax.experimental import pallas as pl
from jax.experimental.pallas import tpu as pltpu

F32 = jnp.float32
BF16 = jnp.bfloat16

N_DIFF_HEADS = 8
N_SB_HEADS = 8
HEAD_DIM = 128
DIFF_QK_DIM = 64
NUM_BUCKETS = 32
MAX_DISTANCE = 128
N_GROUPS = 4
EXPERTS_PER_GROUP = 8
N_EXPERTS = N_GROUPS * EXPERTS_PER_GROUP
TOP_K = 2
LN_EPS = 1e-5
SUBLN_EPS = 1e-5

LANES = 128
MASK_VALUE = -1e30
LOG2E = math.log2(math.e)
SB_EXIT_LOG2 = 104.0 * LOG2E
DIFF_ROW_CHUNK = 128
SB_HEADS_PER_STEP = 8
HI_HALF = 0xFFFF0000
GATHER_SLOTS = 3
WEIGHT_CHUNKS = 4
WEIGHT_DMA_PRIORITY = 1
VMEM_LIMIT = 56 * 1024 * 1024


def _cparams(sem):
    return pltpu.CompilerParams(dimension_semantics=sem, vmem_limit_bytes=VMEM_LIMIT)


def _dot(a, b):
    return jnp.dot(a, b, preferred_element_type=F32)


def _dot_nt(a, b):
    return lax.dot_general(a, b, (((1,), (1,)), ((), ())), preferred_element_type=F32)


def _split(a):
    hi = a.astype(BF16)
    lo = (a - hi.astype(F32)).astype(BF16)
    return hi, lo


def _dot3(a, b):
    a_hi, a_lo = _split(a)
    b_hi, b_lo = _split(b)
    return _dot(a_hi, b_hi) + (_dot(a_hi, b_lo) + _dot(a_lo, b_hi))


def _layer_norm(h, g, b):
    mu = jnp.mean(h, axis=-1, keepdims=True)
    d = h - mu
    var = jnp.mean(d * d, axis=-1, keepdims=True)
    return d * lax.rsqrt(var + LN_EPS) * g + b


def _ada_kernel(c_ref, w_ref, b_ref, o_ref):
    c = c_ref[...]
    s = c * jax.nn.sigmoid(c)
    o_ref[...] = _dot3(s, w_ref[...]) + b_ref[...]


def ada_modulation(c8, w_ada, b_ada, tn=1024):
    L, D, N = w_ada.shape
    return pl.pallas_call(
        _ada_kernel,
        grid=(L, N // tn),
        in_specs=[
            pl.BlockSpec((8, D), lambda l, j: (0, 0)),
            pl.BlockSpec((None, D, tn), lambda l, j: (l, 0, j)),
            pl.BlockSpec((None, 1, tn), lambda l, j: (l, 0, j)),
        ],
        out_specs=pl.BlockSpec((None, 8, tn), lambda l, j: (l, 0, j)),
        out_shape=jax.ShapeDtypeStruct((L, 8, N), F32),
        compiler_params=_cparams(("arbitrary", "arbitrary")),
        name="ada",
    )(c8, w_ada, b_ada.reshape(L, 1, N))


def _inproj_kernel(x_ref, sc_ref, sh_ref, w_ref, o_ref, u_ref):
    @pl.when(pl.program_id(1) == 0)
    def _():
        u_ref[...] = (x_ref[...] * (1.0 + sc_ref[...]) + sh_ref[...]).astype(BF16)

    o_ref[...] = _dot(u_ref[...], w_ref[...]).astype(o_ref.dtype)


def in_projection(x2d, sc, sh, w_bf16, layer, seq, tm=1024, tn=1536):
    M, D = x2d.shape
    N = w_bf16.shape[2]
    tm = min(tm, seq)
    tn = min(tn, N)
    bpb = seq // tm
    return pl.pallas_call(
        _inproj_kernel,
        grid=(M // tm, N // tn),
        in_specs=[
            pl.BlockSpec((tm, D), lambda i, j: (i, 0)),
            pl.BlockSpec((None, 1, D), lambda i, j: (i // bpb, 0, 0)),
            pl.BlockSpec((None, 1, D), lambda i, j: (i // bpb, 0, 0)),
            pl.BlockSpec((None, D, tn), lambda i, j: (layer, 0, j)),
        ],
        out_specs=pl.BlockSpec((tm, tn), lambda i, j: (i, j)),
        out_shape=jax.ShapeDtypeStruct((M, N), BF16),
        scratch_shapes=[pltpu.VMEM((tm, D), BF16)],
        compiler_params=_cparams(("arbitrary", "arbitrary")),
        name="inproj",
    )(x2d, sc, sh, w_bf16)


def _t5_bucket_np(rel):
    n = np.maximum(rel, 0)
    max_exact = NUM_BUCKETS // 2
    ratio = np.maximum(n, 1).astype(np.float32) / np.float32(max_exact)
    large = max_exact + (np.log(ratio).astype(np.float32) / np.float32(math.log(MAX_DISTANCE / max_exact))
                         * np.float32(NUM_BUCKETS - max_exact)).astype(np.int32)
    large = np.minimum(large, NUM_BUCKETS - 1)
    return np.where(n < max_exact, n, large)


def _bucket_thresholds():
    buckets = _t5_bucket_np(np.arange(2 * MAX_DISTANCE))
    assert np.all(np.diff(buckets) >= 0) and buckets[MAX_DISTANCE] == NUM_BUCKETS - 1
    return [int(np.argmax(buckets >= k)) for k in range(NUM_BUCKETS)]


def _relbias_kernel(tab_ref, o_ref, *, t, n_heads, thresholds):
    h = pl.program_id(0)
    row = lax.broadcasted_iota(jnp.int32, (t, t), 0)
    col = lax.broadcasted_iota(jnp.int32, (t, t), 1)
    for blk in range(2):
        rel = row - col + blk * t
        for mp in range(2):
            entry = lambda k: tab_ref[(k * n_heads + h) * 2 + mp]
            far = entry(NUM_BUCKETS - 1)
            val = jnp.full((t, t), entry(0) - far, F32)
            for k in range(1, NUM_BUCKETS):
                val = jnp.where(rel >= thresholds[k], entry(k) - far, val)
            if blk == 0:
                val = jnp.where(rel < 0, MASK_VALUE, val)
            o_ref[blk, mp * t:(mp + 1) * t, :] = val


def rel_bias_tiles(rel_bias, t):
    assert t >= MAX_DISTANCE
    n_heads = rel_bias.shape[1]
    kern = functools.partial(_relbias_kernel, t=t, n_heads=n_heads, thresholds=_bucket_thresholds())
    grid_spec = pltpu.PrefetchScalarGridSpec(
        num_scalar_prefetch=1,
        grid=(n_heads,),
        in_specs=[],
        out_specs=pl.BlockSpec((None, 2, 2 * t, t), lambda h, tab: (h, 0, 0, 0)),
    )
    return pl.pallas_call(
        kern,
        grid_spec=grid_spec,
        out_shape=jax.ShapeDtypeStruct((n_heads, 2, 2 * t, t), F32),
        compiler_params=_cparams(("arbitrary",)),
        name="relbias",
    )(rel_bias.reshape(-1).astype(F32))


def _diff_kernel(lam_ref, q_ref, k_ref, v_ref, bias_ref, g_ref, o_ref, qs_ref, qn_ref, s_ref, m_ref, acc_ref,
                 *, t, rc, nq, out_scale):
    ones = jnp.ones((t, HEAD_DIM), BF16)

    def keys(j):
        return pl.ds(pl.multiple_of(j * t, t), t)

    def stack_queries(qi, dst_ref):
        q = q_ref[keys(qi), :] * BF16(DIFF_QK_DIM ** -0.5)
        lane = lax.broadcasted_iota(jnp.int32, q.shape, 1)
        zero = jnp.zeros_like(q)
        dst_ref[0:t, :] = jnp.where(lane < DIFF_QK_DIM, q, zero)
        dst_ref[t:2 * t, :] = jnp.where(lane >= DIFF_QK_DIM, q, zero)

    def step(j, bias_blk, next_queries_ref, next_j):
        vb = v_ref[keys(j), :]
        vb1 = jnp.concatenate([vb, ones], axis=1)
        kb_next = k_ref[keys(next_j), :]
        diagonal = bias_blk == 0
        bias_reach = rc + MAX_DISTANCE
        for r in range(2 * t // rc):
            rows = pl.ds(r * rc, rc)
            nc = (r % (t // rc) + 1) * rc if diagonal else t
            s = s_ref[rows, :nc]
            s_ref[rows, :] = _dot_nt(next_queries_ref[rows, :], kb_next)
            if bias_blk is not None:
                c = r % (t // rc)
                if diagonal or c == 0:
                    b0 = max(nc - bias_reach, 0)
                    biased = s[:, b0:] + bias_ref[bias_blk, rows, b0:nc]
                    s = biased if b0 == 0 else jnp.concatenate([s[:, :b0], biased], axis=1)
            m_old = m_ref[rows, :]
            m_new = jnp.maximum(m_old, jnp.max(s, axis=1, keepdims=True))
            p = jnp.exp(s - jnp.concatenate([m_new] * (nc // LANES), axis=1)).astype(BF16)
            alpha = jnp.exp(m_old - m_new)
            acc_ref[rows, :] = jnp.concatenate([alpha, alpha], axis=1) * acc_ref[rows, :] + _dot(p, vb1[:nc])
            m_ref[rows, :] = m_new

    stack_queries(0, qs_ref)
    kb0 = k_ref[keys(0), :]
    for r in range(2 * t // rc):
        rows = pl.ds(r * rc, rc)
        s_ref[rows, :] = _dot_nt(qs_ref[rows, :], kb0)

    def reset():
        m_ref[...] = jnp.full(m_ref.shape, MASK_VALUE, F32)
        acc_ref[...] = jnp.zeros(acc_ref.shape, F32)

    def diagonal_step_and_output(qi):
        stack_queries(jnp.minimum(qi + 1, nq - 1), qn_ref)
        step(qi, 0, qn_ref, 0)
        acc = acc_ref[...]
        ratio = acc[:, :HEAD_DIM] / acc[:, HEAD_DIM:]
        o = ratio[:t] - lam_ref[0] * ratio[t:]
        o = o * lax.rsqrt(jnp.mean(o * o, axis=-1, keepdims=True) + SUBLN_EPS)
        o_ref[keys(qi), :] = (o * g_ref[...] * out_scale).astype(o_ref.dtype)
        qs_ref[...] = qn_ref[...]

    reset()
    diagonal_step_and_output(0)

    def tile_body(qi, carry):
        reset()
        n_far = qi - 1

        def far_pair(p, c):
            step(2 * p, None, qs_ref, 2 * p + 1)
            step(2 * p + 1, None, qs_ref, 2 * p + 2)
            return c

        lax.fori_loop(0, n_far // 2, far_pair, 0)

        @pl.when(n_far % 2 == 1)
        def _():
            step(n_far - 1, None, qs_ref, n_far)

        step(qi - 1, 1, qs_ref, qi)
        diagonal_step_and_output(qi)
        return carry

    lax.fori_loop(1, nq, tile_body, 0)


def diff_attention(proj, lam, bias_tiles, subln_g, batch, seq, lam_init, t):
    M = proj.shape[0]
    H = N_DIFF_HEADS
    nq = seq // t
    rc = min(DIFF_ROW_CHUNK, t)
    assert rc % LANES == 0 and rc >= MAX_DISTANCE and MAX_DISTANCE % LANES == 0
    kern = functools.partial(_diff_kernel, t=t, rc=rc, nq=nq, out_scale=1.0 - lam_init)
    grid_spec = pltpu.PrefetchScalarGridSpec(
        num_scalar_prefetch=1,
        grid=(H, batch),
        in_specs=[
            pl.BlockSpec((seq, HEAD_DIM), lambda h, b, s: (b, h)),
            pl.BlockSpec((seq, HEAD_DIM), lambda h, b, s: (b, H + h)),
            pl.BlockSpec((seq, HEAD_DIM), lambda h, b, s: (b, 2 * H + h)),
            pl.BlockSpec((None, 2, 2 * t, t), lambda h, b, s: (h, 0, 0, 0)),
            pl.BlockSpec((1, HEAD_DIM), lambda h, b, s: (0, 0)),
        ],
        out_specs=pl.BlockSpec((seq, HEAD_DIM), lambda h, b, s: (b, h)),
        scratch_shapes=[
            pltpu.VMEM((2 * t, HEAD_DIM), BF16),
            pltpu.VMEM((2 * t, HEAD_DIM), BF16),
            pltpu.VMEM((2 * t, t), F32),
            pltpu.VMEM((2 * t, LANES), F32),
            pltpu.VMEM((2 * t, 2 * HEAD_DIM), F32),
        ],
    )
    return pl.pallas_call(
        kern,
        grid_spec=grid_spec,
        out_shape=jax.ShapeDtypeStruct((M, H * HEAD_DIM), BF16),
        compiler_params=_cparams(("arbitrary", "arbitrary")),
        name="diffattn",
    )(lam.reshape(1).astype(F32), proj, proj, proj, bias_tiles, subln_g.reshape(1, HEAD_DIM))


def _sb_kernel(q_ref, k_ref, v_ref, o_ref, carry_ref, acc_ref, *, t, hp, scale2):
    qi = pl.program_id(2)

    def keys(j):
        return pl.ds(pl.multiple_of(j * t, t), t)

    def upper_ones(n):
        row = lax.broadcasted_iota(jnp.int32, (n, n), 0)
        col = lax.broadcasted_iota(jnp.int32, (n, n), 1)
        return jnp.where(row > col, 1.0, 0.0).astype(BF16)

    def sticks(z2, valid):
        lp = jnp.log2(1.0 + jnp.exp2(-jnp.abs(z2)))
        drop = jnp.maximum(z2, 0.0) + lp
        log2_beta = z2 - drop
        if valid is not None:
            drop = jnp.where(valid, drop, 0.0)
        return drop, log2_beta

    prev = jnp.maximum(qi - 1, 0)
    qrow = lax.broadcasted_iota(jnp.int32, (t, 2 * t), 0)
    kcol = lax.broadcasted_iota(jnp.int32, (t, 2 * t), 1)
    n_prev = jnp.where(qi > 0, t, 0)
    valid = (kcol < n_prev) | ((kcol >= t) & (kcol - t < qrow))
    upper = upper_ones(t)
    for hh in range(hp):
        cols = slice(hh * HEAD_DIM, (hh + 1) * HEAD_DIM)
        kw = jnp.concatenate([k_ref[keys(prev), cols], k_ref[keys(qi), cols]], axis=0)
        vw = jnp.concatenate([v_ref[keys(prev), cols], v_ref[keys(qi), cols]], axis=0)
        drop, log2_beta = sticks(_dot_nt(q_ref[:, cols], kw) * scale2, valid)
        hi, lo = _split(drop)
        stacked = jnp.concatenate([hi[:, :t], lo[:, :t], hi[:, t:], lo[:, t:]], axis=0)
        within = _dot(stacked, upper)
        diag_total = jnp.sum(drop[:, t:], axis=1, keepdims=True)
        later = jnp.concatenate([within[0:t] + within[t:2 * t] + diag_total,
                                 within[2 * t:3 * t] + within[3 * t:4 * t]], axis=1)
        a = jnp.where(valid, jnp.exp2(log2_beta - later), 0.0)
        acc_ref[:, cols] = _dot(a.astype(BF16), vw)
        prev_total = jnp.sum(drop[:, :t], axis=1, keepdims=True)
        carry_ref[hh] = jnp.broadcast_to(prev_total + diag_total, (t, LANES))

    def step(j):
        for hh in range(hp):
            cols = slice(hh * HEAD_DIM, (hh + 1) * HEAD_DIM)
            drop, log2_beta = sticks(_dot_nt(q_ref[:, cols], k_ref[keys(j), cols]) * scale2, None)
            hi, lo = _split(drop)
            later = _dot(hi, upper) + _dot(lo, upper)
            carry = carry_ref[hh]
            a = jnp.exp2(log2_beta - later - jnp.concatenate([carry] * (t // LANES), axis=1))
            acc_ref[:, cols] += _dot(a.astype(BF16), v_ref[keys(j), cols])
            carry_ref[hh] = carry + jnp.sum(drop, axis=1, keepdims=True)

    def cond(state):
        n, live = state
        return jnp.logical_and(n < qi - 1, live < SB_EXIT_LOG2)

    def body(state):
        n, _ = state
        step(qi - 2 - n)
        return n + 1, jnp.min(carry_ref[...])

    lax.while_loop(cond, body, (jnp.int32(0), jnp.min(carry_ref[...])))
    o_ref[...] = acc_ref[...].astype(o_ref.dtype)


def sb_attention(proj, batch, seq, t, hp=SB_HEADS_PER_STEP):
    M = proj.shape[0]
    H = N_SB_HEADS
    base = 3 * N_DIFF_HEADS // hp
    nq = seq // t
    w = hp * HEAD_DIM
    kern = functools.partial(_sb_kernel, t=t, hp=hp, scale2=HEAD_DIM ** -0.5 * LOG2E)
    return pl.pallas_call(
        kern,
        grid=(H // hp, batch, nq),
        in_specs=[
            pl.BlockSpec((t, w), lambda h, b, qi: (b * nq + qi, base + h)),
            pl.BlockSpec((seq, w), lambda h, b, qi: (b, base + H // hp + h)),
            pl.BlockSpec((seq, w), lambda h, b, qi: (b, base + 2 * (H // hp) + h)),
        ],
        out_specs=pl.BlockSpec((t, w), lambda h, b, qi: (b * nq + qi, h)),
        out_shape=jax.ShapeDtypeStruct((M, H * HEAD_DIM), BF16),
        scratch_shapes=[pltpu.VMEM((hp, t, LANES), F32), pltpu.VMEM((t, w), F32)],
        compiler_params=_cparams(("arbitrary", "arbitrary", "arbitrary")),
        name="sbattn",
    )(proj, proj, proj)


ROUTE_E1, ROUTE_E2, ROUTE_W1, ROUTE_W2, ROUTE_R1, ROUTE_R2 = range(6)


def _mix_kernel(a_ref, b_ref, wo_ref, x_ref, g1_ref, sc2_ref, sh2_ref, lng_ref, lnb_ref, wrt_ref, brt_ref,
                x1_ref, u2_ref, route_ref, route_t_ref, cnt_ref, carry_ref, wsplit_ref, *, alpha, half):
    i = pl.program_id(0)

    @pl.when(i == 0)
    def _():
        carry_ref[...] = jnp.zeros(carry_ref.shape, F32)
        w_hi, w_lo = _split(wrt_ref[...])
        wsplit_ref[...] = jnp.concatenate([w_hi, w_lo], axis=1)

    mix = _dot(a_ref[...], wo_ref[pl.ds(0, half), :]) + _dot(b_ref[...], wo_ref[pl.ds(half, half), :])
    h = alpha * x_ref[...] + (1.0 + g1_ref[...]) * mix
    x1 = _layer_norm(h, lng_ref[...], lnb_ref[...])
    x1_ref[...] = x1
    u2 = x1 * (1.0 + sc2_ref[...]) + sh2_ref[...]
    nseg = u2.shape[1] // (2 * LANES)
    for s in range(nseg):
        lo = u2[:, 2 * s * LANES:(2 * s + 1) * LANES].astype(BF16).astype(F32)
        hi = u2[:, (2 * s + 1) * LANES:(2 * s + 2) * LANES].astype(BF16).astype(F32)
        lo_bits = lax.shift_right_logical(lax.bitcast_convert_type(lo, jnp.uint32), jnp.uint32(16))
        hi_bits = lax.bitcast_convert_type(hi, jnp.uint32) & jnp.uint32(HI_HALF)
        u2_ref[pl.ds(s, u2.shape[0], stride=nseg), :] = lo_bits | hi_bits

    u_hi, u_lo = _split(u2)
    hh_hl = _dot(u_hi, wsplit_ref[...])
    lh = _dot(u_lo, wsplit_ref[:, :LANES])
    logits = hh_hl[:, :LANES] + (hh_hl[:, LANES:] + lh) + brt_ref[...]
    tm = logits.shape[0]
    lane = lax.broadcasted_iota(jnp.int32, (tm, LANES), 1).astype(F32)
    gmask = lane < N_GROUPS
    gl = jnp.where(gmask, logits, MASK_VALUE)
    gmax = jnp.max(gl, axis=1, keepdims=True)
    gidx = jnp.min(jnp.where(gmask & (gl == gmax), lane, float(LANES)), axis=1, keepdims=True)
    gsum = jnp.sum(jnp.where(gmask, jnp.exp(gl - gmax), 0.0), axis=1, keepdims=True)
    gp = 1.0 / gsum

    lo = N_GROUPS + gidx * EXPERTS_PER_GROUP
    emask = (lane >= lo) & (lane < lo + EXPERTS_PER_GROUP)
    el = jnp.where(emask, logits, MASK_VALUE)
    v1 = jnp.max(el, axis=1, keepdims=True)
    i1 = jnp.min(jnp.where(emask & (el == v1), lane, float(LANES)), axis=1, keepdims=True)
    emask2 = emask & (lane != i1)
    el2 = jnp.where(emask2, logits, MASK_VALUE)
    v2 = jnp.max(el2, axis=1, keepdims=True)
    i2 = jnp.min(jnp.where(emask2 & (el2 == v2), lane, float(LANES)), axis=1, keepdims=True)
    tt = jnp.exp(v2 - v1)
    w1 = gp / (1.0 + tt)
    w2 = gp * tt / (1.0 + tt)
    e1 = i1 - N_GROUPS
    e2 = i2 - N_GROUPS

    oh1 = lane == e1
    oh2 = lane == e2
    oh1b = jnp.where(oh1, 1.0, 0.0).astype(BF16)
    oh2b = jnp.where(oh2, 1.0, 0.0).astype(BF16)
    row = lax.broadcasted_iota(jnp.int32, (tm, tm), 0)
    col = lax.broadcasted_iota(jnp.int32, (tm, tm), 1)
    lower = jnp.where(col < row, 1.0, 0.0).astype(BF16)
    c1 = jnp.sum(jnp.where(oh1, 1.0, 0.0), axis=0, keepdims=True)
    c2 = jnp.sum(jnp.where(oh2, 1.0, 0.0), axis=0, keepdims=True)
    carry = carry_ref[...]
    r1 = _dot(lower, oh1b) + carry
    r2 = _dot(lower, oh2b) + (carry + c1)
    rank1 = jnp.sum(jnp.where(oh1, r1, 0.0), axis=1, keepdims=True)
    rank2 = jnp.sum(jnp.where(oh2, r2, 0.0), axis=1, keepdims=True)
    carry = carry + c1 + c2
    carry_ref[...] = carry
    cnt_ref[...] = jnp.broadcast_to(carry, cnt_ref.shape)

    out = jnp.zeros((tm, LANES), F32)
    for ln, val in ((ROUTE_E1, e1), (ROUTE_E2, e2), (ROUTE_W1, w1), (ROUTE_W2, w2),
                    (ROUTE_R1, rank1), (ROUTE_R2, rank2)):
        out = jnp.where(lane == float(ln), val, out)
    route_ref[...] = out
    route_t_ref[...] = out.T[:8]


def mix_ln_route(a_out, b_out, wo_bf16, layer, x2d, g1, sc2, sh2, ln_g, ln_b, w_rt, b_rt, seq, alpha, tm=256):
    M, D = x2d.shape
    half = a_out.shape[1]
    bpb = seq // tm
    nseg = D // (2 * LANES)
    row = lambda i: (i // bpb, 0, 0)
    full = lambda i: (0, 0)
    kern = functools.partial(_mix_kernel, alpha=alpha, half=half)
    return pl.pallas_call(
        kern,
        grid=(M // tm,),
        in_specs=[
            pl.BlockSpec((tm, half), lambda i: (i, 0)),
            pl.BlockSpec((tm, half), lambda i: (i, 0)),
            pl.BlockSpec((None, 2 * half, D), lambda i: (layer, 0, 0)),
            pl.BlockSpec((tm, D), lambda i: (i, 0)),
            pl.BlockSpec((None, 1, D), row),
            pl.BlockSpec((None, 1, D), row),
            pl.BlockSpec((None, 1, D), row),
            pl.BlockSpec((1, D), full),
            pl.BlockSpec((1, D), full),
            pl.BlockSpec((D, LANES), full),
            pl.BlockSpec((1, LANES), full),
        ],
        out_specs=[
            pl.BlockSpec((tm, D), lambda i: (i, 0)),
            pl.BlockSpec((tm * nseg, LANES), lambda i: (i, 0)),
            pl.BlockSpec((tm, LANES), lambda i: (i, 0)),
            pl.BlockSpec((8, tm), lambda i: (0, i)),
            pl.BlockSpec((8, LANES), full),
        ],
        out_shape=[
            jax.ShapeDtypeStruct((M, D), F32),
            jax.ShapeDtypeStruct((M * nseg, LANES), jnp.uint32),
            jax.ShapeDtypeStruct((M, LANES), F32),
            jax.ShapeDtypeStruct((8, M), F32),
            jax.ShapeDtypeStruct((8, LANES), F32),
        ],
        scratch_shapes=[pltpu.VMEM((1, LANES), F32), pltpu.VMEM((D, 2 * LANES), BF16)],
        compiler_params=_cparams(("arbitrary",)),
        name="mix",
    )(a_out, b_out, wo_bf16, x2d, g1, sc2, sh2, ln_g.reshape(1, D), ln_b.reshape(1, D), w_rt, b_rt)


def _item_copy(src_hbm, dst_vmem, src_item, dst_item, rpi, sem):
    src = src_hbm.at[pl.ds(pl.multiple_of(src_item * rpi, rpi), rpi), :]
    dst = dst_vmem.at[pl.ds(pl.multiple_of(dst_item * rpi, rpi), rpi), :]
    return pltpu.make_async_copy(src, dst, sem)


def _start_items(idx_ref, base, n_items, rpi, src_hbm, dst_vmem, sem, unroll=False):
    def issue(r, c):
        _item_copy(src_hbm, dst_vmem, idx_ref[base + r], r, rpi, sem).start()
        return c

    lax.fori_loop(0, n_items, issue, 0, unroll=unroll)


def _wait_items(n_items, rpi, src_hbm, dst_vmem, sem):
    pltpu.make_async_copy(src_hbm.at[pl.ds(0, n_items * rpi), :], dst_vmem, sem).wait()


def _expert_kernel(tok_ref, first_ref, u_hbm, wg_hbm, wu_hbm, wd_hbm, y_hbm,
                   xbuf_ref, xs_ref, obuf_ref, stage_gu_ref, stage_d_ref, wg_ref, wu_ref, wd_ref,
                   gsems, osems, wsems, *, layer, tb, nseg, nrow, n_tiles):
    e = pl.program_id(0)
    n_exp = pl.num_programs(0)
    n_used = first_ref[n_exp]
    dc = wg_ref.shape[0] // WEIGHT_CHUNKS
    fc = wd_ref.shape[0] // WEIGHT_CHUNKS

    def weight_copies(expert, c):
        drows = pl.ds(c * dc, dc)
        frows = pl.ds(c * fc, fc)
        return (pltpu.make_async_copy(wg_hbm.at[layer, expert, drows, :], stage_gu_ref.at[c, 0], wsems.at[c]),
                pltpu.make_async_copy(wu_hbm.at[layer, expert, drows, :], stage_gu_ref.at[c, 1], wsems.at[c]),
                pltpu.make_async_copy(wd_hbm.at[layer, expert, frows, :], stage_d_ref.at[c], wsems.at[c]))

    def start_gather(k, unroll):
        s = k % GATHER_SLOTS
        tile = jnp.minimum(k, n_used - 1)
        _start_items(tok_ref, tile * tb, tb, nseg, u_hbm, xbuf_ref.at[s], gsems.at[s], unroll)

    def wait_gather(k):
        s = k % GATHER_SLOTS
        _wait_items(tb, nseg, u_hbm, xbuf_ref.at[s], gsems.at[s])

    def out_copy(tile, s):
        rows = pl.ds(pl.multiple_of(tile * (tb * nrow), tb * nrow), tb * nrow)
        return pltpu.make_async_copy(obuf_ref.at[s], y_hbm.at[rows, :], osems.at[s])

    @pl.when(e == 0)
    def _():
        for c in range(WEIGHT_CHUNKS):
            for cp in weight_copies(0, c):
                cp.start(priority=WEIGHT_DMA_PRIORITY)
        for k in range(GATHER_SLOTS - 1):
            start_gather(k, False)

    for c in range(WEIGHT_CHUNKS):
        for cp in weight_copies(e, c):
            cp.wait()
        wg_ref[c * dc:(c + 1) * dc, :] = stage_gu_ref[c, 0].astype(BF16)
        wu_ref[c * dc:(c + 1) * dc, :] = stage_gu_ref[c, 1].astype(BF16)
        wd_ref[c * fc:(c + 1) * fc, :] = stage_d_ref[c].astype(BF16)

        @pl.when(e + 1 < n_exp)
        def _():
            for cp in weight_copies(e + 1, c):
                cp.start(priority=WEIGHT_DMA_PRIORITY)

    def tile_body(i, carry):
        slot = i % GATHER_SLOTS
        oslot = i % 2

        @pl.when(i >= 2)
        def _():
            out_copy(i - 2, oslot).wait()

        wait_gather(i)
        start_gather(i + GATHER_SLOTS - 1, True)
        for s in range(nseg):
            w = xbuf_ref[slot, pl.ds(s, tb, stride=nseg), :]
            lo = lax.bitcast_convert_type(lax.shift_left(w, jnp.uint32(16)), F32)
            hi = lax.bitcast_convert_type(w & jnp.uint32(HI_HALF), F32)
            xs_ref[:, 2 * s * LANES:(2 * s + 1) * LANES] = lo.astype(BF16)
            xs_ref[:, (2 * s + 1) * LANES:(2 * s + 2) * LANES] = hi.astype(BF16)
        x = xs_ref[...]
        g = _dot(x, wg_ref[...])
        u = _dot(x, wu_ref[...])
        h = (g * jax.nn.sigmoid(g) * u).astype(BF16)
        y = _dot(h, wd_ref[...])
        for s in range(nrow):
            obuf_ref[oslot, pl.ds(s, tb, stride=nrow), :] = y[:, s * LANES:(s + 1) * LANES]
        out_copy(i, oslot).start()
        return carry

    lax.fori_loop(first_ref[e], first_ref[e + 1], tile_body, 0)

    @pl.when(e == n_exp - 1)
    def _():
        for k in range(GATHER_SLOTS - 1):
            wait_gather(n_used + k)
        for back in (1, 2):

            @pl.when(n_used >= back)
            def _():
                out_copy(n_used - back, (n_used - back) % 2).wait()

        obuf_ref[0] = jnp.zeros(obuf_ref.shape[1:], obuf_ref.dtype)

        def zero_tile(i, carry):
            cp = out_copy(i, 0)
            cp.start()
            cp.wait()
            return carry

        lax.fori_loop(n_used, n_tiles, zero_tile, 0)


def expert_mlp(pad_tok, first_tile, u2p, wg, wu, wd, layer, tb):
    P = pad_tok.shape[0]
    E, D, F = wg.shape[1:]
    nseg = D // (2 * LANES)
    nrow = D // LANES
    fc = F // WEIGHT_CHUNKS
    any_spec = pl.BlockSpec(memory_space=pl.ANY)
    grid_spec = pltpu.PrefetchScalarGridSpec(
        num_scalar_prefetch=2,
        grid=(E,),
        in_specs=[any_spec, any_spec, any_spec, any_spec],
        out_specs=any_spec,
        scratch_shapes=[
            pltpu.VMEM((GATHER_SLOTS, tb * nseg, LANES), jnp.uint32),
            pltpu.VMEM((tb, D), BF16),
            pltpu.VMEM((2, tb * nrow, LANES), F32),
            pltpu.VMEM((WEIGHT_CHUNKS, 2, D // WEIGHT_CHUNKS, F), F32),
            pltpu.VMEM((WEIGHT_CHUNKS, fc, D), F32),
            pltpu.VMEM((D, F), BF16),
            pltpu.VMEM((D, F), BF16),
            pltpu.VMEM((F, D), BF16),
            pltpu.SemaphoreType.DMA((GATHER_SLOTS,)),
            pltpu.SemaphoreType.DMA((2,)),
            pltpu.SemaphoreType.DMA((WEIGHT_CHUNKS,)),
        ],
    )
    return pl.pallas_call(
        functools.partial(_expert_kernel, layer=layer, tb=tb, nseg=nseg, nrow=nrow, n_tiles=P // tb),
        grid_spec=grid_spec,
        out_shape=jax.ShapeDtypeStruct((P * nrow, LANES), F32),
        compiler_params=_cparams(("arbitrary",)),
        name="expert",
    )(pad_tok, first_tile, u2p, wg, wu, wd)


def _final_kernel(dest_ref, x1_ref, route_ref, g2_ref, lng_ref, lnb_ref, yb_hbm, o_ref, buf_ref, sems,
                  *, alpha, tm, m, nrow):
    i = pl.program_id(0)
    slot = i % GATHER_SLOTS
    last = pl.num_programs(0) - 1

    def start(n, unroll):
        s = n % GATHER_SLOTS
        tile = jnp.minimum(n, last)
        for k in range(TOP_K):
            _start_items(dest_ref, k * m + tile * tm, tm, nrow, yb_hbm, buf_ref.at[s, k], sems.at[s], unroll)

    def wait(n):
        s = n % GATHER_SLOTS
        for k in range(TOP_K):
            _wait_items(tm, nrow, yb_hbm, buf_ref.at[s, k], sems.at[s])

    @pl.when(i == 0)
    def _():
        for n in range(GATHER_SLOTS - 1):
            start(n, False)

    wait(i)
    start(i + GATHER_SLOTS - 1, True)

    def expert_out(k):
        chunks = [buf_ref[slot, k, pl.ds(s, tm, stride=nrow), :] for s in range(nrow)]
        return jnp.concatenate(chunks, axis=1)

    route = route_ref[...]
    w1 = route[:, ROUTE_W1:ROUTE_W1 + 1]
    w2 = route[:, ROUTE_W2:ROUTE_W2 + 1]
    y = w1 * expert_out(0) + w2 * expert_out(1)
    h = alpha * x1_ref[...] + (1.0 + g2_ref[...]) * y
    o_ref[...] = _layer_norm(h, lng_ref[...], lnb_ref[...])

    @pl.when(i == last)
    def _():
        for n in range(1, GATHER_SLOTS):
            wait(i + n)


def combine_ln(dest_flat, x1, route, g2, ln_g, ln_b, yb, seq, alpha, tm=256):
    M, D = x1.shape
    bpb = seq // tm
    nrow = D // LANES
    grid_spec = pltpu.PrefetchScalarGridSpec(
        num_scalar_prefetch=1,
        grid=(M // tm,),
        in_specs=[
            pl.BlockSpec((tm, D), lambda i, d: (i, 0)),
            pl.BlockSpec((tm, LANES), lambda i, d: (i, 0)),
            pl.BlockSpec((None, 1, D), lambda i, d: (i // bpb, 0, 0)),
            pl.BlockSpec((1, D), lambda i, d: (0, 0)),
            pl.BlockSpec((1, D), lambda i, d: (0, 0)),
            pl.BlockSpec(memory_space=pl.ANY),
        ],
        out_specs=pl.BlockSpec((tm, D), lambda i, d: (i, 0)),
        scratch_shapes=[pltpu.VMEM((GATHER_SLOTS, TOP_K, tm * nrow, LANES), F32),
                        pltpu.SemaphoreType.DMA((GATHER_SLOTS,))],
    )
    return pl.pallas_call(
        functools.partial(_final_kernel, alpha=alpha, tm=tm, m=M, nrow=nrow),
        grid_spec=grid_spec,
        out_shape=jax.ShapeDtypeStruct((M, D), F32),
        compiler_params=_cparams(("arbitrary",)),
        name="final",
    )(dest_flat, x1, route, g2, ln_g.reshape(1, D), ln_b.reshape(1, D), yb)


def moe_layout(route_t, cnt, tb):
    M = route_t.shape[1]
    n_assign = M * TOP_K
    counts = cnt[0, :N_EXPERTS].astype(jnp.int32)
    padded = ((counts + tb - 1) // tb) * tb
    pends = jnp.cumsum(padded)
    pstarts = pends - padded
    n_tiles = (n_assign + N_EXPERTS * (tb - 1) + tb - 1) // tb
    e = route_t[ROUTE_E1:ROUTE_E2 + 1].astype(jnp.int32)
    rank = route_t[ROUTE_R1:ROUTE_R2 + 1].astype(jnp.int32)
    before = jnp.arange(N_EXPERTS, dtype=jnp.int32)[:, None, None] < e[None]
    dest = jnp.sum(jnp.where(before, padded[:, None, None], 0), axis=0) + rank
    dest_flat = dest.reshape(-1)
    tok = jnp.tile(jnp.arange(M, dtype=jnp.int32), TOP_K)
    pad_tok = jnp.zeros((n_tiles * tb,), jnp.int32).at[dest_flat].set(tok)
    first_tile = (jnp.concatenate([pstarts, pends[-1:]]) // tb).astype(jnp.int32)
    return dest_flat, pad_tok, first_tile


def _forward(x, c, w_ada, b_ada, w_in, diff_lambda, diff_subln_g, rel_bias, w_o, ln_g, ln_b, w_group, b_group,
             w_router, b_router, w_gate, w_up, w_down, *, diff_tile, sb_tile, moe_tile, ada_tn):
    B, S, D = x.shape
    depth = w_ada.shape[0]
    M = B * S
    alpha = (2 * depth) ** 0.25

    c8 = jnp.zeros((8, D), F32).at[:B].set(c)
    mod = ada_modulation(c8, w_ada, b_ada, ada_tn)[:, :B]
    mod = mod.reshape(depth, B, 6, 1, D)
    bias_tiles = rel_bias_tiles(rel_bias, diff_tile)
    w_in_b, w_o_b = w_in.astype(BF16), w_o.astype(BF16)

    x2d = x.reshape(M, D)
    for l in range(depth):
        sh1, sc1, g1, sh2, sc2, g2 = (mod[l, :, n] for n in range(6))
        proj = in_projection(x2d, sc1, sh1, w_in_b, l, S)

        lam_init = 0.8 - 0.6 * math.exp(-0.3 * l)
        lp = diff_lambda[l].astype(F32)
        lam = jnp.exp(jnp.sum(lp[0] * lp[1])) - jnp.exp(jnp.sum(lp[2] * lp[3])) + lam_init
        a_out = diff_attention(proj, lam, bias_tiles, diff_subln_g[l], B, S, lam_init, diff_tile)
        b_out = sb_attention(proj, B, S, sb_tile)

        w_rt = jnp.zeros((D, LANES), F32).at[:, :N_GROUPS].set(w_group[l])
        w_rt = w_rt.at[:, N_GROUPS:N_GROUPS + N_EXPERTS].set(w_router[l])
        b_rt = jnp.zeros((1, LANES), F32).at[0, :N_GROUPS].set(b_group[l])
        b_rt = b_rt.at[0, N_GROUPS:N_GROUPS + N_EXPERTS].set(b_router[l])
        x1, u2, route, route_t, cnt = mix_ln_route(a_out, b_out, w_o_b, l, x2d, g1, sc2, sh2,
                                                   ln_g[l, 0], ln_b[l, 0], w_rt, b_rt, S, alpha)

        dest_flat, pad_tok, first_tile = moe_layout(route_t, cnt, moe_tile)
        yb = expert_mlp(pad_tok, first_tile, u2, w_gate, w_up, w_down, l, moe_tile)
        x2d = combine_ln(dest_flat, x1, route, g2, ln_g[l, 1], ln_b[l, 1], yb, S, alpha)
    return x2d.reshape(B, S, D)


def kernel(x, c, w_ada, b_ada, w_in, diff_lambda, diff_subln_g, rel_bias, w_o, ln_g, ln_b, w_group, b_group,
           w_router, b_router, w_gate, w_up, w_down):
    return _forward(x, c, w_ada, b_ada, w_in, diff_lambda, diff_subln_g, rel_bias, w_o, ln_g, ln_b, w_group,
                    b_group, w_router, b_router, w_gate, w_up, w_down,
                    diff_tile=512, sb_tile=256, moe_tile=256, ada_tn=1024)
```

```python
import functools
import math

import jax
import jax.numpy as jnp
import numpy as np
from jax import lax
from jax.experimental import pallas as pl
from jax.experimental.pallas import tpu as pltpu

F32 = jnp.float32
BF16 = jnp.bfloat16

N_DIFF_HEADS = 8
N_SB_HEADS = 8
HEAD_DIM = 128
DIFF_QK_DIM = 64
NUM_BUCKETS = 32
MAX_DISTANCE = 128
N_GROUPS = 4
EXPERTS_PER_GROUP = 8
N_EXPERTS = N_GROUPS * EXPERTS_PER_GROUP
TOP_K = 2
LN_EPS = 1e-5
SUBLN_EPS = 1e-5

LANES = 128
MASK_VALUE = -1e30
LOG2E = math.log2(math.e)
SB_EXIT_LOG2 = 104.0 * LOG2E
DIFF_ROW_CHUNK = 256
SB_HEADS_PER_STEP = 8
HI_HALF = 0xFFFF0000
OUT_ITEM_WIDTH = LANES
GATHER_SLOTS = 3
WEIGHT_CHUNKS = 4
WEIGHT_DMA_PRIORITY = 1
VMEM_LIMIT = 56 * 1024 * 1024


def _cparams(sem):
    return pltpu.CompilerParams(dimension_semantics=sem, vmem_limit_bytes=VMEM_LIMIT)


def _dot(a, b):
    return jnp.dot(a, b, preferred_element_type=F32)


def _dot_nt(a, b):
    return lax.dot_general(a, b, (((1,), (1,)), ((), ())), preferred_element_type=F32)


def _split(a):
    hi = a.astype(BF16)
    lo = (a - hi.astype(F32)).astype(BF16)
    return hi, lo


def _dot3(a, b):
    a_hi, a_lo = _split(a)
    b_hi, b_lo = _split(b)
    return _dot(a_hi, b_hi) + (_dot(a_hi, b_lo) + _dot(a_lo, b_hi))


def _layer_norm(h, g, b):
    mu = jnp.mean(h, axis=-1, keepdims=True)
    d = h - mu
    var = jnp.mean(d * d, axis=-1, keepdims=True)
    return d * lax.rsqrt(var + LN_EPS) * g + b


def _ada_kernel(c_ref, w_ref, b_ref, o_ref):
    c = c_ref[...]
    s = c * jax.nn.sigmoid(c)
    o_ref[...] = _dot3(s, w_ref[...]) + b_ref[...]


def ada_modulation(c8, w_ada, b_ada, tn=1024):
    L, D, N = w_ada.shape
    return pl.pallas_call(
        _ada_kernel,
        grid=(L, N // tn),
        in_specs=[
            pl.BlockSpec((8, D), lambda l, j: (0, 0)),
            pl.BlockSpec((None, D, tn), lambda l, j: (l, 0, j)),
            pl.BlockSpec((None, 1, tn), lambda l, j: (l, 0, j)),
        ],
        out_specs=pl.BlockSpec((None, 8, tn), lambda l, j: (l, 0, j)),
        out_shape=jax.ShapeDtypeStruct((L, 8, N), F32),
        compiler_params=_cparams(("arbitrary", "arbitrary")),
        name="ada",
    )(c8, w_ada, b_ada.reshape(L, 1, N))


def _inproj_kernel(x_ref, sc_ref, sh_ref, w_ref, o_ref, u_ref):
    @pl.when(pl.program_id(1) == 0)
    def _():
        u_ref[...] = (x_ref[...] * (1.0 + sc_ref[...]) + sh_ref[...]).astype(BF16)

    o_ref[...] = _dot(u_ref[...], w_ref[...]).astype(o_ref.dtype)


def in_projection(x2d, sc, sh, w_bf16, layer, seq, tm=1024, tn=1536):
    M, D = x2d.shape
    N = w_bf16.shape[2]
    tm = min(tm, seq)
    tn = min(tn, N)
    bpb = seq // tm
    return pl.pallas_call(
        _inproj_kernel,
        grid=(M // tm, N // tn),
        in_specs=[
            pl.BlockSpec((tm, D), lambda i, j: (i, 0)),
            pl.BlockSpec((None, 1, D), lambda i, j: (i // bpb, 0, 0)),
            pl.BlockSpec((None, 1, D), lambda i, j: (i // bpb, 0, 0)),
            pl.BlockSpec((None, D, tn), lambda i, j: (layer, 0, j)),
        ],
        out_specs=pl.BlockSpec((tm, tn), lambda i, j: (i, j)),
        out_shape=jax.ShapeDtypeStruct((M, N), BF16),
        scratch_shapes=[pltpu.VMEM((tm, D), BF16)],
        compiler_params=_cparams(("arbitrary", "arbitrary")),
        name="inproj",
    )(x2d, sc, sh, w_bf16)


def _t5_bucket_np(rel):
    n = np.maximum(rel, 0)
    max_exact = NUM_BUCKETS // 2
    ratio = np.maximum(n, 1).astype(np.float32) / np.float32(max_exact)
    large = max_exact + (np.log(ratio).astype(np.float32) / np.float32(math.log(MAX_DISTANCE / max_exact))
                         * np.float32(NUM_BUCKETS - max_exact)).astype(np.int32)
    large = np.minimum(large, NUM_BUCKETS - 1)
    return np.where(n < max_exact, n, large)


def _bucket_thresholds():
    buckets = _t5_bucket_np(np.arange(2 * MAX_DISTANCE))
    assert np.all(np.diff(buckets) >= 0) and buckets[MAX_DISTANCE] == NUM_BUCKETS - 1
    return [int(np.argmax(buckets >= k)) for k in range(NUM_BUCKETS)]


def _relbias_kernel(tab_ref, o_ref, *, t, n_heads, thresholds):
    h = pl.program_id(0)
    row = lax.broadcasted_iota(jnp.int32, (t, t), 0)
    col = lax.broadcasted_iota(jnp.int32, (t, t), 1)
    for blk in range(2):
        rel = row - col + blk * t
        for mp in range(2):
            entry = lambda k: tab_ref[(k * n_heads + h) * 2 + mp]
            far = entry(NUM_BUCKETS - 1)
            val = jnp.full((t, t), entry(0) - far, F32)
            for k in range(1, NUM_BUCKETS):
                val = jnp.where(rel >= thresholds[k], entry(k) - far, val)
            if blk == 0:
                val = jnp.where(rel < 0, MASK_VALUE, val)
            o_ref[blk, mp * t:(mp + 1) * t, :] = val


def rel_bias_tiles(rel_bias, t):
    assert t >= MAX_DISTANCE
    n_heads = rel_bias.shape[1]
    kern = functools.partial(_relbias_kernel, t=t, n_heads=n_heads, thresholds=_bucket_thresholds())
    grid_spec = pltpu.PrefetchScalarGridSpec(
        num_scalar_prefetch=1,
        grid=(n_heads,),
        in_specs=[],
        out_specs=pl.BlockSpec((None, 2, 2 * t, t), lambda h, tab: (h, 0, 0, 0)),
    )
    return pl.pallas_call(
        kern,
        grid_spec=grid_spec,
        out_shape=jax.ShapeDtypeStruct((n_heads, 2, 2 * t, t), F32),
        compiler_params=_cparams(("arbitrary",)),
        name="relbias",
    )(rel_bias.reshape(-1).astype(F32))


def _diff_kernel(lam_ref, q_ref, k_ref, v_ref, bias_ref, g_ref, o_ref, qs_ref, qn_ref, s_ref, m_ref, acc_ref,
                 *, t, rc, nq, out_scale):
    ones = jnp.ones((t, HEAD_DIM), BF16)

    def keys(j):
        return pl.ds(pl.multiple_of(j * t, t), t)

    def stack_queries(qi, dst_ref):
        q = q_ref[keys(qi), :] * BF16(DIFF_QK_DIM ** -0.5)
        lane = lax.broadcasted_iota(jnp.int32, q.shape, 1)
        zero = jnp.zeros_like(q)
        dst_ref[0:t, :] = jnp.where(lane < DIFF_QK_DIM, q, zero)
        dst_ref[t:2 * t, :] = jnp.where(lane >= DIFF_QK_DIM, q, zero)

    def step(j, bias_blk, next_queries_ref, next_j):
        vb = v_ref[keys(j), :]
        vb1 = jnp.concatenate([vb, ones], axis=1)
        kb_next = k_ref[keys(next_j), :]
        diagonal = bias_blk == 0
        bias_reach = rc + MAX_DISTANCE
        for r in range(2 * t // rc):
            rows = pl.ds(r * rc, rc)
            nc = (r % (t // rc) + 1) * rc if diagonal else t
            s = s_ref[rows, :nc]
            s_ref[rows, :] = _dot_nt(next_queries_ref[rows, :], kb_next)
            if bias_blk is not None:
                c = r % (t // rc)
                if diagonal or c == 0:
                    b0 = max(nc - bias_reach, 0)
                    biased = s[:, b0:] + bias_ref[bias_blk, rows, b0:nc]
                    s = biased if b0 == 0 else jnp.concatenate([s[:, :b0], biased], axis=1)
            m_old = m_ref[rows, :]
            m_new = jnp.maximum(m_old, jnp.max(s, axis=1, keepdims=True))
            p = jnp.exp(s - jnp.concatenate([m_new] * (nc // LANES), axis=1)).astype(BF16)
            alpha = jnp.exp(m_old - m_new)
            acc_ref[rows, :] = jnp.concatenate([alpha, alpha], axis=1) * acc_ref[rows, :] + _dot(p, vb1[:nc])
            m_ref[rows, :] = m_new

    stack_queries(0, qs_ref)
    kb0 = k_ref[keys(0), :]
    for r in range(2 * t // rc):
        rows = pl.ds(r * rc, rc)
        s_ref[rows, :] = _dot_nt(qs_ref[rows, :], kb0)

    def reset():
        m_ref[...] = jnp.full(m_ref.shape, MASK_VALUE, F32)
        acc_ref[...] = jnp.zeros(acc_ref.shape, F32)

    def diagonal_step_and_output(qi):
        stack_queries(jnp.minimum(qi + 1, nq - 1), qn_ref)
        step(qi, 0, qn_ref, 0)
        acc = acc_ref[...]
        ratio = acc[:, :HEAD_DIM] / acc[:, HEAD_DIM:]
        o = ratio[:t] - lam_ref[0] * ratio[t:]
        o = o * lax.rsqrt(jnp.mean(o * o, axis=-1, keepdims=True) + SUBLN_EPS)
        o_ref[keys(qi), :] = (o * g_ref[...] * out_scale).astype(o_ref.dtype)
        qs_ref[...] = qn_ref[...]

    reset()
    diagonal_step_and_output(0)

    def tile_body(qi, carry):
        reset()
        n_far = qi - 1

        def far_pair(p, c):
            step(2 * p, None, qs_ref, 2 * p + 1)
            step(2 * p + 1, None, qs_ref, 2 * p + 2)
            return c

        lax.fori_loop(0, n_far // 2, far_pair, 0)

        @pl.when(n_far % 2 == 1)
        def _():
            step(n_far - 1, None, qs_ref, n_far)

        step(qi - 1, 1, qs_ref, qi)
        diagonal_step_and_output(qi)
        return carry

    lax.fori_loop(1, nq, tile_body, 0)


def diff_attention(proj, lam, bias_tiles, subln_g, batch, seq, lam_init, t):
    M = proj.shape[0]
    H = N_DIFF_HEADS
    nq = seq // t
    rc = min(DIFF_ROW_CHUNK, t)
    assert rc % LANES == 0 and rc >= MAX_DISTANCE and MAX_DISTANCE % LANES == 0
    kern = functools.partial(_diff_kernel, t=t, rc=rc, nq=nq, out_scale=1.0 - lam_init)
    grid_spec = pltpu.PrefetchScalarGridSpec(
        num_scalar_prefetch=1,
        grid=(H, batch),
        in_specs=[
            pl.BlockSpec((seq, HEAD_DIM), lambda h, b, s: (b, h)),
            pl.BlockSpec((seq, HEAD_DIM), lambda h, b, s: (b, H + h)),
            pl.BlockSpec((seq, HEAD_DIM), lambda h, b, s: (b, 2 * H + h)),
            pl.BlockSpec((None, 2, 2 * t, t), lambda h, b, s: (h, 0, 0, 0)),
            pl.BlockSpec((1, HEAD_DIM), lambda h, b, s: (0, 0)),
        ],
        out_specs=pl.BlockSpec((seq, HEAD_DIM), lambda h, b, s: (b, h)),
        scratch_shapes=[
            pltpu.VMEM((2 * t, HEAD_DIM), BF16),
            pltpu.VMEM((2 * t, HEAD_DIM), BF16),
            pltpu.VMEM((2 * t, t), F32),
            pltpu.VMEM((2 * t, LANES), F32),
            pltpu.VMEM((2 * t, 2 * HEAD_DIM), F32),
        ],
    )
    return pl.pallas_call(
        kern,
        grid_spec=grid_spec,
        out_shape=jax.ShapeDtypeStruct((M, H * HEAD_DIM), BF16),
        compiler_params=_cparams(("arbitrary", "arbitrary")),
        name="diffattn",
    )(lam.reshape(1).astype(F32), proj, proj, proj, bias_tiles, subln_g.reshape(1, HEAD_DIM))


def _sb_kernel(q_ref, k_ref, v_ref, o_ref, carry_ref, acc_ref, *, t, hp, scale2):
    qi = pl.program_id(2)

    def keys(j):
        return pl.ds(pl.multiple_of(j * t, t), t)

    def upper_ones(n):
        row = lax.broadcasted_iota(jnp.int32, (n, n), 0)
        col = lax.broadcasted_iota(jnp.int32, (n, n), 1)
        return jnp.where(row > col, 1.0, 0.0).astype(BF16)

    def sticks(z2, valid):
        lp = jnp.log2(1.0 + jnp.exp2(-jnp.abs(z2)))
        drop = jnp.maximum(z2, 0.0) + lp
        log2_beta = z2 - drop
        if valid is not None:
            drop = jnp.where(valid, drop, 0.0)
        return drop, log2_beta

    prev = jnp.maximum(qi - 1, 0)
    qrow = lax.broadcasted_iota(jnp.int32, (t, 2 * t), 0)
    kcol = lax.broadcasted_iota(jnp.int32, (t, 2 * t), 1)
    n_prev = jnp.where(qi > 0, t, 0)
    valid = (kcol < n_prev) | ((kcol >= t) & (kcol - t < qrow))
    upper = upper_ones(t)
    for hh in range(hp):
        cols = slice(hh * HEAD_DIM, (hh + 1) * HEAD_DIM)
        kw = jnp.concatenate([k_ref[keys(prev), cols], k_ref[keys(qi), cols]], axis=0)
        vw = jnp.concatenate([v_ref[keys(prev), cols], v_ref[keys(qi), cols]], axis=0)
        drop, log2_beta = sticks(_dot_nt(q_ref[:, cols], kw) * scale2, valid)
        hi, lo = _split(drop)
        stacked = jnp.concatenate([hi[:, :t], lo[:, :t], hi[:, t:], lo[:, t:]], axis=0)
        within = _dot(stacked, upper)
        diag_total = jnp.sum(drop[:, t:], axis=1, keepdims=True)
        later = jnp.concatenate([within[0:t] + within[t:2 * t] + diag_total,
                                 within[2 * t:3 * t] + within[3 * t:4 * t]], axis=1)
        a = jnp.where(valid, jnp.exp2(log2_beta - later), 0.0)
        acc_ref[:, cols] = _dot(a.astype(BF16), vw)
        prev_total = jnp.sum(drop[:, :t], axis=1, keepdims=True)
        carry_ref[hh] = jnp.broadcast_to(prev_total + diag_total, (t, LANES))

    def step(j):
        for hh in range(hp):
            cols = slice(hh * HEAD_DIM, (hh + 1) * HEAD_DIM)
            drop, log2_beta = sticks(_dot_nt(q_ref[:, cols], k_ref[keys(j), cols]) * scale2, None)
            hi, lo = _split(drop)
            later = _dot(hi, upper) + _dot(lo, upper)
            carry = carry_ref[hh]
            a = jnp.exp2(log2_beta - later - jnp.concatenate([carry] * (t // LANES), axis=1))
            acc_ref[:, cols] += _dot(a.astype(BF16), v_ref[keys(j), cols])
            carry_ref[hh] = carry + jnp.sum(drop, axis=1, keepdims=True)

    def cond(state):
        n, live = state
        return jnp.logical_and(n < qi - 1, live < SB_EXIT_LOG2)

    def body(state):
        n, _ = state
        step(qi - 2 - n)
        return n + 1, jnp.min(carry_ref[...])

    lax.while_loop(cond, body, (jnp.int32(0), jnp.min(carry_ref[...])))
    o_ref[...] = acc_ref[...].astype(o_ref.dtype)


def sb_attention(proj, batch, seq, t, hp=SB_HEADS_PER_STEP):
    M = proj.shape[0]
    H = N_SB_HEADS
    base = 3 * N_DIFF_HEADS // hp
    nq = seq // t
    w = hp * HEAD_DIM
    kern = functools.partial(_sb_kernel, t=t, hp=hp, scale2=HEAD_DIM ** -0.5 * LOG2E)
    return pl.pallas_call(
        kern,
        grid=(H // hp, batch, nq),
        in_specs=[
            pl.BlockSpec((t, w), lambda h, b, qi: (b * nq + qi, base + h)),
            pl.BlockSpec((seq, w), lambda h, b, qi: (b, base + H // hp + h)),
            pl.BlockSpec((seq, w), lambda h, b, qi: (b, base + 2 * (H // hp) + h)),
        ],
        out_specs=pl.BlockSpec((t, w), lambda h, b, qi: (b * nq + qi, h)),
        out_shape=jax.ShapeDtypeStruct((M, H * HEAD_DIM), BF16),
        scratch_shapes=[pltpu.VMEM((hp, t, LANES), F32), pltpu.VMEM((t, w), F32)],
        compiler_params=_cparams(("arbitrary", "arbitrary", "arbitrary")),
        name="sbattn",
    )(proj, proj, proj)


ROUTE_E1, ROUTE_E2, ROUTE_W1, ROUTE_W2, ROUTE_R1, ROUTE_R2 = range(6)


def _mix_kernel(a_ref, b_ref, wo_ref, x_ref, g1_ref, sc2_ref, sh2_ref, lng_ref, lnb_ref, wrt_ref, brt_ref,
                x1_ref, u2_ref, route_ref, route_t_ref, cnt_ref, carry_ref, wsplit_ref, *, alpha, half):
    i = pl.program_id(0)

    @pl.when(i == 0)
    def _():
        carry_ref[...] = jnp.zeros(carry_ref.shape, F32)
        w_hi, w_lo = _split(wrt_ref[...])
        wsplit_ref[...] = jnp.concatenate([w_hi, w_lo], axis=1)

    mix = _dot(a_ref[...], wo_ref[pl.ds(0, half), :]) + _dot(b_ref[...], wo_ref[pl.ds(half, half), :])
    h = alpha * x_ref[...] + (1.0 + g1_ref[...]) * mix
    x1 = _layer_norm(h, lng_ref[...], lnb_ref[...])
    x1_ref[...] = x1
    u2 = x1 * (1.0 + sc2_ref[...]) + sh2_ref[...]
    nseg = u2.shape[1] // (2 * LANES)
    for s in range(nseg):
        lo = u2[:, 2 * s * LANES:(2 * s + 1) * LANES].astype(BF16).astype(F32)
        hi = u2[:, (2 * s + 1) * LANES:(2 * s + 2) * LANES].astype(BF16).astype(F32)
        lo_bits = lax.shift_right_logical(lax.bitcast_convert_type(lo, jnp.uint32), jnp.uint32(16))
        hi_bits = lax.bitcast_convert_type(hi, jnp.uint32) & jnp.uint32(HI_HALF)
        u2_ref[pl.ds(s, u2.shape[0], stride=nseg), :] = lo_bits | hi_bits

    u_hi, u_lo = _split(u2)
    hh_hl = _dot(u_hi, wsplit_ref[...])
    lh = _dot(u_lo, wsplit_ref[:, :LANES])
    logits = hh_hl[:, :LANES] + (hh_hl[:, LANES:] + lh) + brt_ref[...]
    tm = logits.shape[0]
    lane = lax.broadcasted_iota(jnp.int32, (tm, LANES), 1).astype(F32)
    gmask = lane < N_GROUPS
    gl = jnp.where(gmask, logits, MASK_VALUE)
    gmax = jnp.max(gl, axis=1, keepdims=True)
    gidx = jnp.min(jnp.where(gmask & (gl == gmax), lane, float(LANES)), axis=1, keepdims=True)
    gsum = jnp.sum(jnp.where(gmask, jnp.exp(gl - gmax), 0.0), axis=1, keepdims=True)
    gp = 1.0 / gsum

    lo = N_GROUPS + gidx * EXPERTS_PER_GROUP
    emask = (lane >= lo) & (lane < lo + EXPERTS_PER_GROUP)
    el = jnp.where(emask, logits, MASK_VALUE)
    v1 = jnp.max(el, axis=1, keepdims=True)
    i1 = jnp.min(jnp.where(emask & (el == v1), lane, float(LANES)), axis=1, keepdims=True)
    emask2 = emask & (lane != i1)
    el2 = jnp.where(emask2, logits, MASK_VALUE)
    v2 = jnp.max(el2, axis=1, keepdims=True)
    i2 = jnp.min(jnp.where(emask2 & (el2 == v2), lane, float(LANES)), axis=1, keepdims=True)
    tt = jnp.exp(v2 - v1)
    w1 = gp / (1.0 + tt)
    w2 = gp * tt / (1.0 + tt)
    e1 = i1 - N_GROUPS
    e2 = i2 - N_GROUPS

    oh1 = lane == e1
    oh2 = lane == e2
    oh1b = jnp.where(oh1, 1.0, 0.0).astype(BF16)
    oh2b = jnp.where(oh2, 1.0, 0.0).astype(BF16)
    row = lax.broadcasted_iota(jnp.int32, (tm, tm), 0)
    col = lax.broadcasted_iota(jnp.int32, (tm, tm), 1)
    lower = jnp.where(col < row, 1.0, 0.0).astype(BF16)
    c1 = jnp.sum(jnp.where(oh1, 1.0, 0.0), axis=0, keepdims=True)
    c2 = jnp.sum(jnp.where(oh2, 1.0, 0.0), axis=0, keepdims=True)
    carry = carry_ref[...]
    r1 = _dot(lower, oh1b) + carry
    r2 = _dot(lower, oh2b) + (carry + c1)
    rank1 = jnp.sum(jnp.where(oh1, r1, 0.0), axis=1, keepdims=True)
    rank2 = jnp.sum(jnp.where(oh2, r2, 0.0), axis=1, keepdims=True)
    carry = carry + c1 + c2
    carry_ref[...] = carry
    cnt_ref[...] = jnp.broadcast_to(carry, cnt_ref.shape)

    out = jnp.zeros((tm, LANES), F32)
    for ln, val in ((ROUTE_E1, e1), (ROUTE_E2, e2), (ROUTE_W1, w1), (ROUTE_W2, w2),
                    (ROUTE_R1, rank1), (ROUTE_R2, rank2)):
        out = jnp.where(lane == float(ln), val, out)
    route_ref[...] = out
    route_t_ref[...] = out.T[:8]


def mix_ln_route(a_out, b_out, wo_bf16, layer, x2d, g1, sc2, sh2, ln_g, ln_b, w_rt, b_rt, seq, alpha, tm=512):
    M, D = x2d.shape
    half = a_out.shape[1]
    bpb = seq // tm
    nseg = D // (2 * LANES)
    row = lambda i: (i // bpb, 0, 0)
    full = lambda i: (0, 0)
    kern = functools.partial(_mix_kernel, alpha=alpha, half=half)
    return pl.pallas_call(
        kern,
        grid=(M // tm,),
        in_specs=[
            pl.BlockSpec((tm, half), lambda i: (i, 0)),
            pl.BlockSpec((tm, half), lambda i: (i, 0)),
            pl.BlockSpec((None, 2 * half, D), lambda i: (layer, 0, 0)),
            pl.BlockSpec((tm, D), lambda i: (i, 0)),
            pl.BlockSpec((None, 1, D), row),
            pl.BlockSpec((None, 1, D), row),
            pl.BlockSpec((None, 1, D), row),
            pl.BlockSpec((1, D), full),
            pl.BlockSpec((1, D), full),
            pl.BlockSpec((D, LANES), full),
            pl.BlockSpec((1, LANES), full),
        ],
        out_specs=[
            pl.BlockSpec((tm, D), lambda i: (i, 0)),
            pl.BlockSpec((tm * nseg, LANES), lambda i: (i, 0)),
            pl.BlockSpec((tm, LANES), lambda i: (i, 0)),
            pl.BlockSpec((8, tm), lambda i: (0, i)),
            pl.BlockSpec((8, LANES), full),
        ],
        out_shape=[
            jax.ShapeDtypeStruct((M, D), F32),
            jax.ShapeDtypeStruct((M * nseg, LANES), jnp.uint32),
            jax.ShapeDtypeStruct((M, LANES), F32),
            jax.ShapeDtypeStruct((8, M), F32),
            jax.ShapeDtypeStruct((8, LANES), F32),
        ],
        scratch_shapes=[pltpu.VMEM((1, LANES), F32), pltpu.VMEM((D, 2 * LANES), BF16)],
        compiler_params=_cparams(("arbitrary",)),
        name="mix",
    )(a_out, b_out, wo_bf16, x2d, g1, sc2, sh2, ln_g.reshape(1, D), ln_b.reshape(1, D), w_rt, b_rt)


def _item_copy(src_hbm, dst_vmem, src_item, dst_item, rpi, sem):
    src = src_hbm.at[pl.ds(pl.multiple_of(src_item * rpi, rpi), rpi), :]
    dst = dst_vmem.at[pl.ds(pl.multiple_of(dst_item * rpi, rpi), rpi), :]
    return pltpu.make_async_copy(src, dst, sem)


def _start_items(idx_ref, base, n_items, rpi, src_hbm, dst_vmem, sem, unroll=False):
    def issue(r, c):
        _item_copy(src_hbm, dst_vmem, idx_ref[base + r], r, rpi, sem).start()
        return c

    lax.fori_loop(0, n_items, issue, 0, unroll=unroll)


def _wait_items(n_items, rpi, src_hbm, dst_vmem, sem):
    pltpu.make_async_copy(src_hbm.at[pl.ds(0, n_items * rpi), :], dst_vmem, sem).wait()


def _expert_kernel(tok_ref, first_ref, u_hbm, wg_hbm, wu_hbm, wd_hbm, y_hbm,
                   xbuf_ref, xs_ref, obuf_ref, stage_gu_ref, stage_d_ref, wg_ref, wu_ref, wd_ref,
                   gsems, osems, wsems, *, layer, tb, nseg, nrow, n_tiles):
    e = pl.program_id(0)
    n_exp = pl.num_programs(0)
    n_used = first_ref[n_exp]
    dc = wg_ref.shape[0] // WEIGHT_CHUNKS
    fc = wd_ref.shape[0] // WEIGHT_CHUNKS

    def weight_copies(expert, c):
        drows = pl.ds(c * dc, dc)
        frows = pl.ds(c * fc, fc)
        return (pltpu.make_async_copy(wg_hbm.at[layer, expert, drows, :], stage_gu_ref.at[c, 0], wsems.at[c]),
                pltpu.make_async_copy(wu_hbm.at[layer, expert, drows, :], stage_gu_ref.at[c, 1], wsems.at[c]),
                pltpu.make_async_copy(wd_hbm.at[layer, expert, frows, :], stage_d_ref.at[c], wsems.at[c]))

    def start_gather(k, unroll):
        s = k % GATHER_SLOTS
        tile = jnp.minimum(k, n_used - 1)
        _start_items(tok_ref, tile * tb, tb, nseg, u_hbm, xbuf_ref.at[s], gsems.at[s], unroll)

    def wait_gather(k):
        s = k % GATHER_SLOTS
        _wait_items(tb, nseg, u_hbm, xbuf_ref.at[s], gsems.at[s])

    def out_copy(tile, s):
        rows = pl.ds(pl.multiple_of(tile * (tb * nrow), tb * nrow), tb * nrow)
        return pltpu.make_async_copy(obuf_ref.at[s], y_hbm.at[rows, :], osems.at[s])

    @pl.when(e == 0)
    def _():
        for c in range(WEIGHT_CHUNKS):
            for cp in weight_copies(0, c):
                cp.start(priority=WEIGHT_DMA_PRIORITY)
        for k in range(GATHER_SLOTS - 1):
            start_gather(k, False)

    for c in range(WEIGHT_CHUNKS):
        for cp in weight_copies(e, c):
            cp.wait()
        wg_ref[c * dc:(c + 1) * dc, :] = stage_gu_ref[c, 0].astype(BF16)
        wu_ref[c * dc:(c + 1) * dc, :] = stage_gu_ref[c, 1].astype(BF16)
        wd_ref[c * fc:(c + 1) * fc, :] = stage_d_ref[c].astype(BF16)

        @pl.when(e + 1 < n_exp)
        def _():
            for cp in weight_copies(e + 1, c):
                cp.start(priority=WEIGHT_DMA_PRIORITY)

    def tile_body(i, carry):
        slot = i % GATHER_SLOTS
        oslot = i % 2

        @pl.when(i >= 2)
        def _():
            out_copy(i - 2, oslot).wait()

        wait_gather(i)
        start_gather(i + GATHER_SLOTS - 1, True)
        for s in range(nseg):
            w = xbuf_ref[slot, pl.ds(s, tb, stride=nseg), :]
            lo = lax.bitcast_convert_type(lax.shift_left(w, jnp.uint32(16)), F32)
            hi = lax.bitcast_convert_type(w & jnp.uint32(HI_HALF), F32)
            xs_ref[:, 2 * s * LANES:(2 * s + 1) * LANES] = lo.astype(BF16)
            xs_ref[:, (2 * s + 1) * LANES:(2 * s + 2) * LANES] = hi.astype(BF16)
        x = xs_ref[...]
        g = _dot(x, wg_ref[...])
        u = _dot(x, wu_ref[...])
        h = (g * jax.nn.sigmoid(g) * u).astype(BF16)
        y = _dot(h, wd_ref[...])
        for s in range(nrow):
            obuf_ref[oslot, pl.ds(s, tb, stride=nrow), :] = y[:, s * OUT_ITEM_WIDTH:(s + 1) * OUT_ITEM_WIDTH]
        out_copy(i, oslot).start()
        return carry

    lax.fori_loop(first_ref[e], first_ref[e + 1], tile_body, 0)

    @pl.when(e == n_exp - 1)
    def _():
        for k in range(GATHER_SLOTS - 1):
            wait_gather(n_used + k)
        for back in (1, 2):

            @pl.when(n_used >= back)
            def _():
                out_copy(n_used - back, (n_used - back) % 2).wait()

        obuf_ref[0] = jnp.zeros(obuf_ref.shape[1:], obuf_ref.dtype)

        def zero_tile(i, carry):
            cp = out_copy(i, 0)
            cp.start()
            cp.wait()
            return carry

        lax.fori_loop(n_used, n_tiles, zero_tile, 0)


def expert_mlp(pad_tok, first_tile, u2p, wg, wu, wd, layer, tb):
    P = pad_tok.shape[0]
    E, D, F = wg.shape[1:]
    nseg = D // (2 * LANES)
    nrow = D // OUT_ITEM_WIDTH
    fc = F // WEIGHT_CHUNKS
    any_spec = pl.BlockSpec(memory_space=pl.ANY)
    grid_spec = pltpu.PrefetchScalarGridSpec(
        num_scalar_prefetch=2,
        grid=(E,),
        in_specs=[any_spec, any_spec, any_spec, any_spec],
        out_specs=any_spec,
        scratch_shapes=[
            pltpu.VMEM((GATHER_SLOTS, tb * nseg, LANES), jnp.uint32),
            pltpu.VMEM((tb, D), BF16),
            pltpu.VMEM((2, tb * nrow, OUT_ITEM_WIDTH), F32),
            pltpu.VMEM((WEIGHT_CHUNKS, 2, D // WEIGHT_CHUNKS, F), F32),
            pltpu.VMEM((WEIGHT_CHUNKS, fc, D), F32),
            pltpu.VMEM((D, F), BF16),
            pltpu.VMEM((D, F), BF16),
            pltpu.VMEM((F, D), BF16),
            pltpu.SemaphoreType.DMA((GATHER_SLOTS,)),
            pltpu.SemaphoreType.DMA((2,)),
            pltpu.SemaphoreType.DMA((WEIGHT_CHUNKS,)),
        ],
    )
    return pl.pallas_call(
        functools.partial(_expert_kernel, layer=layer, tb=tb, nseg=nseg, nrow=nrow, n_tiles=P // tb),
        grid_spec=grid_spec,
        out_shape=jax.ShapeDtypeStruct((P * nrow, OUT_ITEM_WIDTH), F32),
        compiler_params=_cparams(("arbitrary",)),
        name="expert",
    )(pad_tok, first_tile, u2p, wg, wu, wd)


def _final_kernel(dest_ref, x1_ref, route_ref, g2_ref, lng_ref, lnb_ref, yb_hbm, o_ref, buf_ref, sems,
                  *, alpha, tm, m, nrow):
    i = pl.program_id(0)
    slot = i % GATHER_SLOTS
    last = pl.num_programs(0) - 1

    def start(n, unroll):
        s = n % GATHER_SLOTS
        tile = jnp.minimum(n, last)
        for k in range(TOP_K):
            _start_items(dest_ref, k * m + tile * tm, tm, nrow, yb_hbm, buf_ref.at[s, k], sems.at[s], unroll)

    def wait(n):
        s = n % GATHER_SLOTS
        for k in range(TOP_K):
            _wait_items(tm, nrow, yb_hbm, buf_ref.at[s, k], sems.at[s])

    @pl.when(i == 0)
    def _():
        for n in range(GATHER_SLOTS - 1):
            start(n, False)

    wait(i)
    start(i + GATHER_SLOTS - 1, True)

    def expert_out(k):
        chunks = [buf_ref[slot, k, pl.ds(s, tm, stride=nrow), :] for s in range(nrow)]
        return jnp.concatenate(chunks, axis=1)

    route = route_ref[...]
    w1 = route[:, ROUTE_W1:ROUTE_W1 + 1]
    w2 = route[:, ROUTE_W2:ROUTE_W2 + 1]
    y = w1 * expert_out(0) + w2 * expert_out(1)
    h = alpha * x1_ref[...] + (1.0 + g2_ref[...]) * y
    o_ref[...] = _layer_norm(h, lng_ref[...], lnb_ref[...])

    @pl.when(i == last)
    def _():
        for n in range(1, GATHER_SLOTS):
            wait(i + n)


def combine_ln(dest_flat, x1, route, g2, ln_g, ln_b, yb, seq, alpha, tm=512):
    M, D = x1.shape
    bpb = seq // tm
    nrow = D // OUT_ITEM_WIDTH
    grid_spec = pltpu.PrefetchScalarGridSpec(
        num_scalar_prefetch=1,
        grid=(M // tm,),
        in_specs=[
            pl.BlockSpec((tm, D), lambda i, d: (i, 0)),
            pl.BlockSpec((tm, LANES), lambda i, d: (i, 0)),
            pl.BlockSpec((None, 1, D), lambda i, d: (i // bpb, 0, 0)),
            pl.BlockSpec((1, D), lambda i, d: (0, 0)),
            pl.BlockSpec((1, D), lambda i, d: (0, 0)),
            pl.BlockSpec(memory_space=pl.ANY),
        ],
        out_specs=pl.BlockSpec((tm, D), lambda i, d: (i, 0)),
        scratch_shapes=[pltpu.VMEM((GATHER_SLOTS, TOP_K, tm * nrow, OUT_ITEM_WIDTH), F32),
                        pltpu.SemaphoreType.DMA((GATHER_SLOTS,))],
    )
    return pl.pallas_call(
        functools.partial(_final_kernel, alpha=alpha, tm=tm, m=M, nrow=nrow),
        grid_spec=grid_spec,
        out_shape=jax.ShapeDtypeStruct((M, D), F32),
        compiler_params=_cparams(("arbitrary",)),
        name="final",
    )(dest_flat, x1, route, g2, ln_g.reshape(1, D), ln_b.reshape(1, D), yb)


def moe_layout(route_t, cnt, tb):
    M = route_t.shape[1]
    n_assign = M * TOP_K
    counts = cnt[0, :N_EXPERTS].astype(jnp.int32)
    padded = ((counts + tb - 1) // tb) * tb
    pends = jnp.cumsum(padded)
    pstarts = pends - padded
    n_tiles = (n_assign + N_EXPERTS * (tb - 1) + tb - 1) // tb
    e = route_t[ROUTE_E1:ROUTE_E2 + 1].astype(jnp.int32)
    rank = route_t[ROUTE_R1:ROUTE_R2 + 1].astype(jnp.int32)
    before = jnp.arange(N_EXPERTS, dtype=jnp.int32)[:, None, None] < e[None]
    dest = jnp.sum(jnp.where(before, padded[:, None, None], 0), axis=0) + rank
    dest_flat = dest.reshape(-1)
    tok = jnp.tile(jnp.arange(M, dtype=jnp.int32), TOP_K)
    pad_tok = jnp.zeros((n_tiles * tb,), jnp.int32).at[dest_flat].set(tok)
    first_tile = (jnp.concatenate([pstarts, pends[-1:]]) // tb).astype(jnp.int32)
    return dest_flat, pad_tok, first_tile


def _forward(x, c, w_ada, b_ada, w_in, diff_lambda, diff_subln_g, rel_bias, w_o, ln_g, ln_b, w_group, b_group,
             w_router, b_router, w_gate, w_up, w_down, *, diff_tile, sb_tile, moe_tile, ada_tn):
    B, S, D = x.shape
    depth = w_ada.shape[0]
    M = B * S
    alpha = (2 * depth) ** 0.25

    c8 = jnp.zeros((8, D), F32).at[:B].set(c)
    mod = ada_modulation(c8, w_ada, b_ada, ada_tn)[:, :B]
    mod = mod.reshape(depth, B, 6, 1, D)
    bias_tiles = rel_bias_tiles(rel_bias, diff_tile)
    w_in_b, w_o_b = w_in.astype(BF16), w_o.astype(BF16)

    x2d = x.reshape(M, D)
    for l in range(depth):
        sh1, sc1, g1, sh2, sc2, g2 = (mod[l, :, n] for n in range(6))
        proj = in_projection(x2d, sc1, sh1, w_in_b, l, S)

        lam_init = 0.8 - 0.6 * math.exp(-0.3 * l)
        lp = diff_lambda[l].astype(F32)
        lam = jnp.exp(jnp.sum(lp[0] * lp[1])) - jnp.exp(jnp.sum(lp[2] * lp[3])) + lam_init
        a_out = diff_attention(proj, lam, bias_tiles, diff_subln_g[l], B, S, lam_init, diff_tile)
        b_out = sb_attention(proj, B, S, sb_tile)

        w_rt = jnp.zeros((D, LANES), F32).at[:, :N_GROUPS].set(w_group[l])
        w_rt = w_rt.at[:, N_GROUPS:N_GROUPS + N_EXPERTS].set(w_router[l])
        b_rt = jnp.zeros((1, LANES), F32).at[0, :N_GROUPS].set(b_group[l])
        b_rt = b_rt.at[0, N_GROUPS:N_GROUPS + N_EXPERTS].set(b_router[l])
        x1, u2, route, route_t, cnt = mix_ln_route(a_out, b_out, w_o_b, l, x2d, g1, sc2, sh2,
                                                   ln_g[l, 0], ln_b[l, 0], w_rt, b_rt, S, alpha)

        dest_flat, pad_tok, first_tile = moe_layout(route_t, cnt, moe_tile)
        yb = expert_mlp(pad_tok, first_tile, u2, w_gate, w_up, w_down, l, moe_tile)
        x2d = combine_ln(dest_flat, x1, route, g2, ln_g[l, 1], ln_b[l, 1], yb, S, alpha)
    return x2d.reshape(B, S, D)


def kernel(x, c, w_ada, b_ada, w_in, diff_lambda, diff_subln_g, rel_bias, w_o, ln_g, ln_b, w_group, b_group,
           w_router, b_router, w_gate, w_up, w_down):
    return _forward(x, c, w_ada, b_ada, w_in, diff_lambda, diff_subln_g, rel_bias, w_o, ln_g, ln_b, w_group,
                    b_group, w_router, b_router, w_gate, w_up, w_down,
                    diff_tile=512, sb_tile=256, moe_tile=256, ada_tn=1024)
```

```python
import functools
import math

import jax
import jax.numpy as jnp
import numpy as np
from jax import lax
from jax.experimental import pallas as pl
from jax.experimental.pallas import tpu as pltpu

F32 = jnp.float32
BF16 = jnp.bfloat16

N_DIFF_HEADS = 8
N_SB_HEADS = 8
HEAD_DIM = 128
DIFF_QK_DIM = 64
NUM_BUCKETS = 32
MAX_DISTANCE = 128
N_GROUPS = 4
EXPERTS_PER_GROUP = 8
N_EXPERTS = N_GROUPS * EXPERTS_PER_GROUP
TOP_K = 2
LN_EPS = 1e-5
SUBLN_EPS = 1e-5

LANES = 128
MASK_VALUE = -1e30
LOG2E = math.log2(math.e)
SB_EXIT_LOG2 = 104.0 * LOG2E
DIFF_ROW_CHUNK = 256
SB_HEADS_PER_STEP = 8
HI_HALF = 0xFFFF0000
OUT_ITEM_WIDTH = LANES
GATHER_SLOTS = 3
WEIGHT_CHUNKS = 4
WEIGHT_DMA_PRIORITY = 1
VMEM_LIMIT = 56 * 1024 * 1024


def _cparams(sem):
    return pltpu.CompilerParams(dimension_semantics=sem, vmem_limit_bytes=VMEM_LIMIT)


def _dot(a, b):
    return jnp.dot(a, b, preferred_element_type=F32)


def _dot_nt(a, b):
    return lax.dot_general(a, b, (((1,), (1,)), ((), ())), preferred_element_type=F32)


def _split(a):
    hi = a.astype(BF16)
    lo = (a - hi.astype(F32)).astype(BF16)
    return hi, lo


def _dot3(a, b):
    a_hi, a_lo = _split(a)
    b_hi, b_lo = _split(b)
    return _dot(a_hi, b_hi) + (_dot(a_hi, b_lo) + _dot(a_lo, b_hi))


def _layer_norm(h, g, b):
    mu = jnp.mean(h, axis=-1, keepdims=True)
    d = h - mu
    var = jnp.mean(d * d, axis=-1, keepdims=True)
    return d * lax.rsqrt(var + LN_EPS) * g + b


def _ada_kernel(c_ref, w_ref, b_ref, o_ref):
    c = c_ref[...]
    s = c * jax.nn.sigmoid(c)
    o_ref[...] = _dot3(s, w_ref[...]) + b_ref[...]


def ada_modulation(c8, w_ada, b_ada, tn=1024):
    L, D, N = w_ada.shape
    return pl.pallas_call(
        _ada_kernel,
        grid=(L, N // tn),
        in_specs=[
            pl.BlockSpec((8, D), lambda l, j: (0, 0)),
            pl.BlockSpec((None, D, tn), lambda l, j: (l, 0, j)),
            pl.BlockSpec((None, 1, tn), lambda l, j: (l, 0, j)),
        ],
        out_specs=pl.BlockSpec((None, 8, tn), lambda l, j: (l, 0, j)),
        out_shape=jax.ShapeDtypeStruct((L, 8, N), F32),
        compiler_params=_cparams(("arbitrary", "arbitrary")),
        name="ada",
    )(c8, w_ada, b_ada.reshape(L, 1, N))


def _inproj_kernel(x_ref, sc_ref, sh_ref, w_ref, o_ref, u_ref):
    @pl.when(pl.program_id(1) == 0)
    def _():
        u_ref[...] = (x_ref[...] * (1.0 + sc_ref[...]) + sh_ref[...]).astype(BF16)

    o_ref[...] = _dot(u_ref[...], w_ref[...]).astype(o_ref.dtype)


def in_projection(x2d, sc, sh, w_bf16, layer, seq, tm=1024, tn=1536):
    M, D = x2d.shape
    N = w_bf16.shape[2]
    tm = min(tm, seq)
    tn = min(tn, N)
    bpb = seq // tm
    return pl.pallas_call(
        _inproj_kernel,
        grid=(M // tm, N // tn),
        in_specs=[
            pl.BlockSpec((tm, D), lambda i, j: (i, 0)),
            pl.BlockSpec((None, 1, D), lambda i, j: (i // bpb, 0, 0)),
            pl.BlockSpec((None, 1, D), lambda i, j: (i // bpb, 0, 0)),
            pl.BlockSpec((None, D, tn), lambda i, j: (layer, 0, j)),
        ],
        out_specs=pl.BlockSpec((tm, tn), lambda i, j: (i, j)),
        out_shape=jax.ShapeDtypeStruct((M, N), BF16),
        scratch_shapes=[pltpu.VMEM((tm, D), BF16)],
        compiler_params=_cparams(("arbitrary", "arbitrary")),
        name="inproj",
    )(x2d, sc, sh, w_bf16)


def _t5_bucket_np(rel):
    n = np.maximum(rel, 0)
    max_exact = NUM_BUCKETS // 2
    ratio = np.maximum(n, 1).astype(np.float32) / np.float32(max_exact)
    large = max_exact + (np.log(ratio).astype(np.float32) / np.float32(math.log(MAX_DISTANCE / max_exact))
                         * np.float32(NUM_BUCKETS - max_exact)).astype(np.int32)
    large = np.minimum(large, NUM_BUCKETS - 1)
    return np.where(n < max_exact, n, large)


def _bucket_thresholds():
    buckets = _t5_bucket_np(np.arange(2 * MAX_DISTANCE))
    assert np.all(np.diff(buckets) >= 0) and buckets[MAX_DISTANCE] == NUM_BUCKETS - 1
    return [int(np.argmax(buckets >= k)) for k in range(NUM_BUCKETS)]


def _relbias_kernel(tab_ref, o_ref, *, t, n_heads, thresholds):
    h = pl.program_id(0)
    row = lax.broadcasted_iota(jnp.int32, (t, t), 0)
    col = lax.broadcasted_iota(jnp.int32, (t, t), 1)
    for blk in range(2):
        rel = row - col + blk * t
        for mp in range(2):
            entry = lambda k: tab_ref[(k * n_heads + h) * 2 + mp]
            far = entry(NUM_BUCKETS - 1)
            val = jnp.full((t, t), entry(0) - far, F32)
            for k in range(1, NUM_BUCKETS):
                val = jnp.where(rel >= thresholds[k], entry(k) - far, val)
            if blk == 0:
                val = jnp.where(rel < 0, MASK_VALUE, val)
            o_ref[blk, mp * t:(mp + 1) * t, :] = val


def rel_bias_tiles(rel_bias, t):
    assert t >= MAX_DISTANCE
    n_heads = rel_bias.shape[1]
    kern = functools.partial(_relbias_kernel, t=t, n_heads=n_heads, thresholds=_bucket_thresholds())
    grid_spec = pltpu.PrefetchScalarGridSpec(
        num_scalar_prefetch=1,
        grid=(n_heads,),
        in_specs=[],
        out_specs=pl.BlockSpec((None, 2, 2 * t, t), lambda h, tab: (h, 0, 0, 0)),
    )
    return pl.pallas_call(
        kern,
        grid_spec=grid_spec,
        out_shape=jax.ShapeDtypeStruct((n_heads, 2, 2 * t, t), F32),
        compiler_params=_cparams(("arbitrary",)),
        name="relbias",
    )(rel_bias.reshape(-1).astype(F32))


def _diff_kernel(lam_ref, q_ref, k_ref, v_ref, bias_ref, g_ref, o_ref, qs_ref, qn_ref, s_ref, m_ref, acc_ref,
                 *, t, rc, nq, out_scale):
    ones = jnp.ones((t, HEAD_DIM), BF16)

    def keys(j):
        return pl.ds(pl.multiple_of(j * t, t), t)

    def stack_queries(qi, dst_ref):
        q = q_ref[keys(qi), :] * BF16(DIFF_QK_DIM ** -0.5)
        lane = lax.broadcasted_iota(jnp.int32, q.shape, 1)
        zero = jnp.zeros_like(q)
        dst_ref[0:t, :] = jnp.where(lane < DIFF_QK_DIM, q, zero)
        dst_ref[t:2 * t, :] = jnp.where(lane >= DIFF_QK_DIM, q, zero)

    def step(j, bias_blk, next_queries_ref, next_j):
        vb = v_ref[keys(j), :]
        vb1 = jnp.concatenate([vb, ones], axis=1)
        kb_next = k_ref[keys(next_j), :]
        diagonal = bias_blk == 0
        bias_reach = rc + MAX_DISTANCE
        for r in range(2 * t // rc):
            rows = pl.ds(r * rc, rc)
            nc = (r % (t // rc) + 1) * rc if diagonal else t
            s = s_ref[rows, :nc]
            s_ref[rows, :] = _dot_nt(next_queries_ref[rows, :], kb_next)
            if bias_blk is not None:
                c = r % (t // rc)
                if diagonal or c == 0:
                    b0 = max(nc - bias_reach, 0)
                    biased = s[:, b0:] + bias_ref[bias_blk, rows, b0:nc]
                    s = biased if b0 == 0 else jnp.concatenate([s[:, :b0], biased], axis=1)
            m_old = m_ref[rows, :]
            m_new = jnp.maximum(m_old, jnp.max(s, axis=1, keepdims=True))
            p = jnp.exp(s - jnp.concatenate([m_new] * (nc // LANES), axis=1)).astype(BF16)
            alpha = jnp.exp(m_old - m_new)
            acc_ref[rows, :] = jnp.concatenate([alpha, alpha], axis=1) * acc_ref[rows, :] + _dot(p, vb1[:nc])
            m_ref[rows, :] = m_new

    stack_queries(0, qs_ref)
    kb0 = k_ref[keys(0), :]
    for r in range(2 * t // rc):
        rows = pl.ds(r * rc, rc)
        s_ref[rows, :] = _dot_nt(qs_ref[rows, :], kb0)

    def reset():
        m_ref[...] = jnp.full(m_ref.shape, MASK_VALUE, F32)
        acc_ref[...] = jnp.zeros(acc_ref.shape, F32)

    def diagonal_step_and_output(qi):
        stack_queries(jnp.minimum(qi + 1, nq - 1), qn_ref)
        step(qi, 0, qn_ref, 0)
        acc = acc_ref[...]
        ratio = acc[:, :HEAD_DIM] / acc[:, HEAD_DIM:]
        o = ratio[:t] - lam_ref[0] * ratio[t:]
        o = o * lax.rsqrt(jnp.mean(o * o, axis=-1, keepdims=True) + SUBLN_EPS)
        o_ref[keys(qi), :] = (o * g_ref[...] * out_scale).astype(o_ref.dtype)
        qs_ref[...] = qn_ref[...]

    reset()
    diagonal_step_and_output(0)

    def tile_body(qi, carry):
        reset()
        n_far = qi - 1

        def far_pair(p, c):
            step(2 * p, None, qs_ref, 2 * p + 1)
            step(2 * p + 1, None, qs_ref, 2 * p + 2)
            return c

        lax.fori_loop(0, n_far // 2, far_pair, 0)

        @pl.when(n_far % 2 == 1)
        def _():
            step(n_far - 1, None, qs_ref, n_far)

        step(qi - 1, 1, qs_ref, qi)
        diagonal_step_and_output(qi)
        return carry

    lax.fori_loop(1, nq, tile_body, 0)


def diff_attention(proj, lam, bias_tiles, subln_g, batch, seq, lam_init, t):
    M = proj.shape[0]
    H = N_DIFF_HEADS
    nq = seq // t
    rc = min(DIFF_ROW_CHUNK, t)
    assert rc % LANES == 0 and rc >= MAX_DISTANCE and MAX_DISTANCE % LANES == 0
    kern = functools.partial(_diff_kernel, t=t, rc=rc, nq=nq, out_scale=1.0 - lam_init)
    grid_spec = pltpu.PrefetchScalarGridSpec(
        num_scalar_prefetch=1,
        grid=(H, batch),
        in_specs=[
            pl.BlockSpec((seq, HEAD_DIM), lambda h, b, s: (b, h)),
            pl.BlockSpec((seq, HEAD_DIM), lambda h, b, s: (b, H + h)),
            pl.BlockSpec((seq, HEAD_DIM), lambda h, b, s: (b, 2 * H + h)),
            pl.BlockSpec((None, 2, 2 * t, t), lambda h, b, s: (h, 0, 0, 0)),
            pl.BlockSpec((1, HEAD_DIM), lambda h, b, s: (0, 0)),
        ],
        out_specs=pl.BlockSpec((seq, HEAD_DIM), lambda h, b, s: (b, h)),
        scratch_shapes=[
            pltpu.VMEM((2 * t, HEAD_DIM), BF16),
            pltpu.VMEM((2 * t, HEAD_DIM), BF16),
            pltpu.VMEM((2 * t, t), F32),
            pltpu.VMEM((2 * t, LANES), F32),
            pltpu.VMEM((2 * t, 2 * HEAD_DIM), F32),
        ],
    )
    return pl.pallas_call(
        kern,
        grid_spec=grid_spec,
        out_shape=jax.ShapeDtypeStruct((M, H * HEAD_DIM), BF16),
        compiler_params=_cparams(("arbitrary", "arbitrary")),
        name="diffattn",
    )(lam.reshape(1).astype(F32), proj, proj, proj, bias_tiles, subln_g.reshape(1, HEAD_DIM))


def _sb_kernel(q_ref, k_ref, v_ref, o_ref, carry_ref, acc_ref, *, t, hp, scale2):
    qi = pl.program_id(2)

    def keys(j):
        return pl.ds(pl.multiple_of(j * t, t), t)

    def upper_ones(n):
        row = lax.broadcasted_iota(jnp.int32, (n, n), 0)
        col = lax.broadcasted_iota(jnp.int32, (n, n), 1)
        return jnp.where(row > col, 1.0, 0.0).astype(BF16)

    def sticks(z2, valid):
        lp = jnp.log2(1.0 + jnp.exp2(-jnp.abs(z2)))
        drop = jnp.maximum(z2, 0.0) + lp
        log2_beta = z2 - drop
        if valid is not None:
            drop = jnp.where(valid, drop, 0.0)
        return drop, log2_beta

    prev = jnp.maximum(qi - 1, 0)
    qrow = lax.broadcasted_iota(jnp.int32, (t, 2 * t), 0)
    kcol = lax.broadcasted_iota(jnp.int32, (t, 2 * t), 1)
    n_prev = jnp.where(qi > 0, t, 0)
    valid = (kcol < n_prev) | ((kcol >= t) & (kcol - t < qrow))
    upper = upper_ones(t)
    for hh in range(hp):
        cols = slice(hh * HEAD_DIM, (hh + 1) * HEAD_DIM)
        kw = jnp.concatenate([k_ref[keys(prev), cols], k_ref[keys(qi), cols]], axis=0)
        vw = jnp.concatenate([v_ref[keys(prev), cols], v_ref[keys(qi), cols]], axis=0)
        drop, log2_beta = sticks(_dot_nt(q_ref[:, cols], kw) * scale2, valid)
        hi, lo = _split(drop)
        stacked = jnp.concatenate([hi[:, :t], lo[:, :t], hi[:, t:], lo[:, t:]], axis=0)
        within = _dot(stacked, upper)
        diag_total = jnp.sum(drop[:, t:], axis=1, keepdims=True)
        later = jnp.concatenate([within[0:t] + within[t:2 * t] + diag_total,
                                 within[2 * t:3 * t] + within[3 * t:4 * t]], axis=1)
        a = jnp.where(valid, jnp.exp2(log2_beta - later), 0.0)
        acc_ref[:, cols] = _dot(a.astype(BF16), vw)
        prev_total = jnp.sum(drop[:, :t], axis=1, keepdims=True)
        carry_ref[hh] = jnp.broadcast_to(prev_total + diag_total, (t, LANES))

    def step(j):
        for hh in range(hp):
            cols = slice(hh * HEAD_DIM, (hh + 1) * HEAD_DIM)
            drop, log2_beta = sticks(_dot_nt(q_ref[:, cols], k_ref[keys(j), cols]) * scale2, None)
            hi, lo = _split(drop)
            later = _dot(hi, upper) + _dot(lo, upper)
            carry = carry_ref[hh]
            a = jnp.exp2(log2_beta - later - jnp.concatenate([carry] * (t // LANES), axis=1))
            acc_ref[:, cols] += _dot(a.astype(BF16), v_ref[keys(j), cols])
            carry_ref[hh] = carry + jnp.sum(drop, axis=1, keepdims=True)

    def cond(state):
        n, live = state
        return jnp.logical_and(n < qi - 1, live < SB_EXIT_LOG2)

    def body(state):
        n, _ = state
        step(qi - 2 - n)
        return n + 1, jnp.min(carry_ref[...])

    lax.while_loop(cond, body, (jnp.int32(0), jnp.min(carry_ref[...])))
    o_ref[...] = acc_ref[...].astype(o_ref.dtype)


def sb_attention(proj, batch, seq, t, hp=SB_HEADS_PER_STEP):
    M = proj.shape[0]
    H = N_SB_HEADS
    base = 3 * N_DIFF_HEADS // hp
    nq = seq // t
    w = hp * HEAD_DIM
    kern = functools.partial(_sb_kernel, t=t, hp=hp, scale2=HEAD_DIM ** -0.5 * LOG2E)
    return pl.pallas_call(
        kern,
        grid=(H // hp, batch, nq),
        in_specs=[
            pl.BlockSpec((t, w), lambda h, b, qi: (b * nq + qi, base + h)),
            pl.BlockSpec((seq, w), lambda h, b, qi: (b, base + H // hp + h)),
            pl.BlockSpec((seq, w), lambda h, b, qi: (b, base + 2 * (H // hp) + h)),
        ],
        out_specs=pl.BlockSpec((t, w), lambda h, b, qi: (b * nq + qi, h)),
        out_shape=jax.ShapeDtypeStruct((M, H * HEAD_DIM), BF16),
        scratch_shapes=[pltpu.VMEM((hp, t, LANES), F32), pltpu.VMEM((t, w), F32)],
        compiler_params=_cparams(("arbitrary", "arbitrary", "arbitrary")),
        name="sbattn",
    )(proj, proj, proj)


ROUTE_E1, ROUTE_E2, ROUTE_W1, ROUTE_W2, ROUTE_R1, ROUTE_R2 = range(6)


def _mix_kernel(a_ref, b_ref, wo_ref, x_ref, g1_ref, sc2_ref, sh2_ref, lng_ref, lnb_ref, wrt_ref, brt_ref,
                x1_ref, u2_ref, route_ref, route_t_ref, cnt_ref, carry_ref, wsplit_ref, *, alpha, half):
    i = pl.program_id(0)

    @pl.when(i == 0)
    def _():
        carry_ref[...] = jnp.zeros(carry_ref.shape, F32)
        w_hi, w_lo = _split(wrt_ref[...])
        wsplit_ref[...] = jnp.concatenate([w_hi, w_lo], axis=1)

    mix = _dot(a_ref[...], wo_ref[pl.ds(0, half), :]) + _dot(b_ref[...], wo_ref[pl.ds(half, half), :])
    h = alpha * x_ref[...] + (1.0 + g1_ref[...]) * mix
    x1 = _layer_norm(h, lng_ref[...], lnb_ref[...])
    x1_ref[...] = x1
    u2 = x1 * (1.0 + sc2_ref[...]) + sh2_ref[...]
    nseg = u2.shape[1] // (2 * LANES)
    for s in range(nseg):
        lo = u2[:, 2 * s * LANES:(2 * s + 1) * LANES].astype(BF16).astype(F32)
        hi = u2[:, (2 * s + 1) * LANES:(2 * s + 2) * LANES].astype(BF16).astype(F32)
        lo_bits = lax.shift_right_logical(lax.bitcast_convert_type(lo, jnp.uint32), jnp.uint32(16))
        hi_bits = lax.bitcast_convert_type(hi, jnp.uint32) & jnp.uint32(HI_HALF)
        u2_ref[pl.ds(s, u2.shape[0], stride=nseg), :] = lo_bits | hi_bits

    u_hi, u_lo = _split(u2)
    hh_hl = _dot(u_hi, wsplit_ref[...])
    lh = _dot(u_lo, wsplit_ref[:, :LANES])
    logits = hh_hl[:, :LANES] + (hh_hl[:, LANES:] + lh) + brt_ref[...]
    tm = logits.shape[0]
    lane = lax.broadcasted_iota(jnp.int32, (tm, LANES), 1).astype(F32)
    gmask = lane < N_GROUPS
    gl = jnp.where(gmask, logits, MASK_VALUE)
    gmax = jnp.max(gl, axis=1, keepdims=True)
    gidx = jnp.min(jnp.where(gmask & (gl == gmax), lane, float(LANES)), axis=1, keepdims=True)
    gsum = jnp.sum(jnp.where(gmask, jnp.exp(gl - gmax), 0.0), axis=1, keepdims=True)
    gp = 1.0 / gsum

    lo = N_GROUPS + gidx * EXPERTS_PER_GROUP
    emask = (lane >= lo) & (lane < lo + EXPERTS_PER_GROUP)
    el = jnp.where(emask, logits, MASK_VALUE)
    v1 = jnp.max(el, axis=1, keepdims=True)
    i1 = jnp.min(jnp.where(emask & (el == v1), lane, float(LANES)), axis=1, keepdims=True)
    emask2 = emask & (lane != i1)
    el2 = jnp.where(emask2, logits, MASK_VALUE)
    v2 = jnp.max(el2, axis=1, keepdims=True)
    i2 = jnp.min(jnp.where(emask2 & (el2 == v2), lane, float(LANES)), axis=1, keepdims=True)
    tt = jnp.exp(v2 - v1)
    w1 = gp / (1.0 + tt)
    w2 = gp * tt / (1.0 + tt)
    e1 = i1 - N_GROUPS
    e2 = i2 - N_GROUPS

    oh1 = lane == e1
    oh2 = lane == e2
    oh1b = jnp.where(oh1, 1.0, 0.0).astype(BF16)
    oh2b = jnp.where(oh2, 1.0, 0.0).astype(BF16)
    row = lax.broadcasted_iota(jnp.int32, (tm, tm), 0)
    col = lax.broadcasted_iota(jnp.int32, (tm, tm), 1)
    lower = jnp.where(col < row, 1.0, 0.0).astype(BF16)
    c1 = jnp.sum(jnp.where(oh1, 1.0, 0.0), axis=0, keepdims=True)
    c2 = jnp.sum(jnp.where(oh2, 1.0, 0.0), axis=0, keepdims=True)
    carry = carry_ref[...]
    r1 = _dot(lower, oh1b) + carry
    r2 = _dot(lower, oh2b) + (carry + c1)
    rank1 = jnp.sum(jnp.where(oh1, r1, 0.0), axis=1, keepdims=True)
    rank2 = jnp.sum(jnp.where(oh2, r2, 0.0), axis=1, keepdims=True)
    carry = carry + c1 + c2
    carry_ref[...] = carry
    cnt_ref[...] = jnp.broadcast_to(carry, cnt_ref.shape)

    out = jnp.zeros((tm, LANES), F32)
    for ln, val in ((ROUTE_E1, e1), (ROUTE_E2, e2), (ROUTE_W1, w1), (ROUTE_W2, w2),
                    (ROUTE_R1, rank1), (ROUTE_R2, rank2)):
        out = jnp.where(lane == float(ln), val, out)
    route_ref[...] = out
    route_t_ref[...] = out.T[:8]


def mix_ln_route(a_out, b_out, wo_bf16, layer, x2d, g1, sc2, sh2, ln_g, ln_b, w_rt, b_rt, seq, alpha, tm=512):
    M, D = x2d.shape
    half = a_out.shape[1]
    bpb = seq // tm
    nseg = D // (2 * LANES)
    row = lambda i: (i // bpb, 0, 0)
    full = lambda i: (0, 0)
    kern = functools.partial(_mix_kernel, alpha=alpha, half=half)
    return pl.pallas_call(
        kern,
        grid=(M // tm,),
        in_specs=[
            pl.BlockSpec((tm, half), lambda i: (i, 0)),
            pl.BlockSpec((tm, half), lambda i: (i, 0)),
            pl.BlockSpec((None, 2 * half, D), lambda i: (layer, 0, 0)),
            pl.BlockSpec((tm, D), lambda i: (i, 0)),
            pl.BlockSpec((None, 1, D), row),
            pl.BlockSpec((None, 1, D), row),
            pl.BlockSpec((None, 1, D), row),
            pl.BlockSpec((1, D), full),
            pl.BlockSpec((1, D), full),
            pl.BlockSpec((D, LANES), full),
            pl.BlockSpec((1, LANES), full),
        ],
        out_specs=[
            pl.BlockSpec((tm, D), lambda i: (i, 0)),
            pl.BlockSpec((tm * nseg, LANES), lambda i: (i, 0)),
            pl.BlockSpec((tm, LANES), lambda i: (i, 0)),
            pl.BlockSpec((8, tm), lambda i: (0, i)),
            pl.BlockSpec((8, LANES), full),
        ],
        out_shape=[
            jax.ShapeDtypeStruct((M, D), F32),
            jax.ShapeDtypeStruct((M * nseg, LANES), jnp.uint32),
            jax.ShapeDtypeStruct((M, LANES), F32),
            jax.ShapeDtypeStruct((8, M), F32),
            jax.ShapeDtypeStruct((8, LANES), F32),
        ],
        scratch_shapes=[pltpu.VMEM((1, LANES), F32), pltpu.VMEM((D, 2 * LANES), BF16)],
        compiler_params=_cparams(("arbitrary",)),
        name="mix",
    )(a_out, b_out, wo_bf16, x2d, g1, sc2, sh2, ln_g.reshape(1, D), ln_b.reshape(1, D), w_rt, b_rt)


def _item_copy(src_hbm, dst_vmem, src_item, dst_item, rpi, sem):
    src = src_hbm.at[pl.ds(pl.multiple_of(src_item * rpi, rpi), rpi), :]
    dst = dst_vmem.at[pl.ds(pl.multiple_of(dst_item * rpi, rpi), rpi), :]
    return pltpu.make_async_copy(src, dst, sem)


def _start_items(idx_ref, base, n_items, rpi, src_hbm, dst_vmem, sem, unroll=False):
    def issue(r, c):
        _item_copy(src_hbm, dst_vmem, idx_ref[base + r], r, rpi, sem).start()
        return c

    lax.fori_loop(0, n_items, issue, 0, unroll=unroll)


def _wait_items(n_items, rpi, src_hbm, dst_vmem, sem):
    pltpu.make_async_copy(src_hbm.at[pl.ds(0, n_items * rpi), :], dst_vmem, sem).wait()


def _expert_kernel(tok_ref, first_ref, u_hbm, wg_hbm, wu_hbm, wd_hbm, y_hbm,
                   xbuf_ref, xs_ref, obuf_ref, stage_gu_ref, stage_d_ref, wg_ref, wu_ref, wd_ref,
                   gsems, osems, wsems, *, layer, tb, nseg, nrow, n_tiles):
    e = pl.program_id(0)
    n_exp = pl.num_programs(0)
    n_used = first_ref[n_exp]
    dc = wg_ref.shape[0] // WEIGHT_CHUNKS
    fc = wd_ref.shape[0] // WEIGHT_CHUNKS

    def weight_copies(expert, c):
        drows = pl.ds(c * dc, dc)
        frows = pl.ds(c * fc, fc)
        return (pltpu.make_async_copy(wg_hbm.at[layer, expert, drows, :], stage_gu_ref.at[c, 0], wsems.at[c]),
                pltpu.make_async_copy(wu_hbm.at[layer, expert, drows, :], stage_gu_ref.at[c, 1], wsems.at[c]),
                pltpu.make_async_copy(wd_hbm.at[layer, expert, frows, :], stage_d_ref.at[c], wsems.at[c]))

    def start_gather(k, unroll):
        s = k % GATHER_SLOTS
        tile = jnp.minimum(k, n_used - 1)
        _start_items(tok_ref, tile * tb, tb, nseg, u_hbm, xbuf_ref.at[s], gsems.at[s], unroll)

    def wait_gather(k):
        s = k % GATHER_SLOTS
        _wait_items(tb, nseg, u_hbm, xbuf_ref.at[s], gsems.at[s])

    def out_copy(tile, s):
        rows = pl.ds(pl.multiple_of(tile * (tb * nrow), tb * nrow), tb * nrow)
        return pltpu.make_async_copy(obuf_ref.at[s], y_hbm.at[rows, :], osems.at[s])

    @pl.when(e == 0)
    def _():
        for c in range(WEIGHT_CHUNKS):
            for cp in weight_copies(0, c):
                cp.start(priority=WEIGHT_DMA_PRIORITY)
        for k in range(GATHER_SLOTS - 1):
            start_gather(k, False)

    for c in range(WEIGHT_CHUNKS):
        for cp in weight_copies(e, c):
            cp.wait()
        wg_ref[c * dc:(c + 1) * dc, :] = stage_gu_ref[c, 0].astype(BF16)
        wu_ref[c * dc:(c + 1) * dc, :] = stage_gu_ref[c, 1].astype(BF16)
        wd_ref[c * fc:(c + 1) * fc, :] = stage_d_ref[c].astype(BF16)

        @pl.when(e + 1 < n_exp)
        def _():
            for cp in weight_copies(e + 1, c):
                cp.start(priority=WEIGHT_DMA_PRIORITY)

    def tile_body(i, carry):
        slot = i % GATHER_SLOTS
        oslot = i % 2

        @pl.when(i >= 2)
        def _():
            out_copy(i - 2, oslot).wait()

        wait_gather(i)
        start_gather(i + GATHER_SLOTS - 1, True)
        for s in range(nseg):
            w = xbuf_ref[slot, pl.ds(s, tb, stride=nseg), :]
            lo = lax.bitcast_convert_type(lax.shift_left(w, jnp.uint32(16)), F32)
            hi = lax.bitcast_convert_type(w & jnp.uint32(HI_HALF), F32)
            xs_ref[:, 2 * s * LANES:(2 * s + 1) * LANES] = lo.astype(BF16)
            xs_ref[:, (2 * s + 1) * LANES:(2 * s + 2) * LANES] = hi.astype(BF16)
        x = xs_ref[...]
        g = _dot(x, wg_ref[...])
        u = _dot(x, wu_ref[...])
        h = (g * jax.nn.sigmoid(g) * u).astype(BF16)
        y = _dot(h, wd_ref[...])
        for s in range(nrow):
            obuf_ref[oslot, pl.ds(s, tb, stride=nrow), :] = y[:, s * OUT_ITEM_WIDTH:(s + 1) * OUT_ITEM_WIDTH]
        out_copy(i, oslot).start()
        return carry

    lax.fori_loop(first_ref[e], first_ref[e + 1], tile_body, 0)

    @pl.when(e == n_exp - 1)
    def _():
        for k in range(GATHER_SLOTS - 1):
            wait_gather(n_used + k)
        for back in (1, 2):

            @pl.when(n_used >= back)
            def _():
                out_copy(n_used - back, (n_used - back) % 2).wait()

        obuf_ref[0] = jnp.zeros(obuf_ref.shape[1:], obuf_ref.dtype)

        def zero_tile(i, carry):
            cp = out_copy(i, 0)
            cp.start()
            cp.wait()
            return carry

        lax.fori_loop(n_used, n_tiles, zero_tile, 0)


def expert_mlp(pad_tok, first_tile, u2p, wg, wu, wd, layer, tb):
    P = pad_tok.shape[0]
    E, D, F = wg.shape[1:]
    nseg = D // (2 * LANES)
    nrow = D // OUT_ITEM_WIDTH
    fc = F // WEIGHT_CHUNKS
    any_spec = pl.BlockSpec(memory_space=pl.ANY)
    grid_spec = pltpu.PrefetchScalarGridSpec(
        num_scalar_prefetch=2,
        grid=(E,),
        in_specs=[any_spec, any_spec, any_spec, any_spec],
        out_specs=any_spec,
        scratch_shapes=[
            pltpu.VMEM((GATHER_SLOTS, tb * nseg, LANES), jnp.uint32),
            pltpu.VMEM((tb, D), BF16),
            pltpu.VMEM((2, tb * nrow, OUT_ITEM_WIDTH), F32),
            pltpu.VMEM((WEIGHT_CHUNKS, 2, D // WEIGHT_CHUNKS, F), F32),
            pltpu.VMEM((WEIGHT_CHUNKS, fc, D), F32),
            pltpu.VMEM((D, F), BF16),
            pltpu.VMEM((D, F), BF16),
            pltpu.VMEM((F, D), BF16),
            pltpu.SemaphoreType.DMA((GATHER_SLOTS,)),
            pltpu.SemaphoreType.DMA((2,)),
            pltpu.SemaphoreType.DMA((WEIGHT_CHUNKS,)),
        ],
    )
    return pl.pallas_call(
        functools.partial(_expert_kernel, layer=layer, tb=tb, nseg=nseg, nrow=nrow, n_tiles=P // tb),
        grid_spec=grid_spec,
        out_shape=jax.ShapeDtypeStruct((P * nrow, OUT_ITEM_WIDTH), F32),
        compiler_params=_cparams(("arbitrary",)),
        name="expert",
    )(pad_tok, first_tile, u2p, wg, wu, wd)


def _final_kernel(dest_ref, x1_ref, route_ref, g2_ref, lng_ref, lnb_ref, yb_hbm, o_ref, buf_ref, sems,
                  *, alpha, tm, m, nrow):
    i = pl.program_id(0)
    slot = i % GATHER_SLOTS
    last = pl.num_programs(0) - 1

    def start(n, unroll):
        s = n % GATHER_SLOTS
        tile = jnp.minimum(n, last)
        for k in range(TOP_K):
            _start_items(dest_ref, k * m + tile * tm, tm, nrow, yb_hbm, buf_ref.at[s, k], sems.at[s], unroll)

    def wait(n):
        s = n % GATHER_SLOTS
        for k in range(TOP_K):
            _wait_items(tm, nrow, yb_hbm, buf_ref.at[s, k], sems.at[s])

    @pl.when(i == 0)
    def _():
        for n in range(GATHER_SLOTS - 1):
            start(n, False)

    wait(i)
    start(i + GATHER_SLOTS - 1, True)

    def expert_out(k):
        chunks = [buf_ref[slot, k, pl.ds(s, tm, stride=nrow), :] for s in range(nrow)]
        return jnp.concatenate(chunks, axis=1)

    route = route_ref[...]
    w1 = route[:, ROUTE_W1:ROUTE_W1 + 1]
    w2 = route[:, ROUTE_W2:ROUTE_W2 + 1]
    y = w1 * expert_out(0) + w2 * expert_out(1)
    h = alpha * x1_ref[...] + (1.0 + g2_ref[...]) * y
    o_ref[...] = _layer_norm(h, lng_ref[...], lnb_ref[...])

    @pl.when(i == last)
    def _():
        for n in range(1, GATHER_SLOTS):
            wait(i + n)


def combine_ln(dest_flat, x1, route, g2, ln_g, ln_b, yb, seq, alpha, tm=256):
    M, D = x1.shape
    bpb = seq // tm
    nrow = D // OUT_ITEM_WIDTH
    grid_spec = pltpu.PrefetchScalarGridSpec(
        num_scalar_prefetch=1,
        grid=(M // tm,),
        in_specs=[
            pl.BlockSpec((tm, D), lambda i, d: (i, 0)),
            pl.BlockSpec((tm, LANES), lambda i, d: (i, 0)),
            pl.BlockSpec((None, 1, D), lambda i, d: (i // bpb, 0, 0)),
            pl.BlockSpec((1, D), lambda i, d: (0, 0)),
            pl.BlockSpec((1, D), lambda i, d: (0, 0)),
            pl.BlockSpec(memory_space=pl.ANY),
        ],
        out_specs=pl.BlockSpec((tm, D), lambda i, d: (i, 0)),
        scratch_shapes=[pltpu.VMEM((GATHER_SLOTS, TOP_K, tm * nrow, OUT_ITEM_WIDTH), F32),
                        pltpu.SemaphoreType.DMA((GATHER_SLOTS,))],
    )
    return pl.pallas_call(
        functools.partial(_final_kernel, alpha=alpha, tm=tm, m=M, nrow=nrow),
        grid_spec=grid_spec,
        out_shape=jax.ShapeDtypeStruct((M, D), F32),
        compiler_params=_cparams(("arbitrary",)),
        name="final",
    )(dest_flat, x1, route, g2, ln_g.reshape(1, D), ln_b.reshape(1, D), yb)


def moe_layout(route_t, cnt, tb):
    M = route_t.shape[1]
    n_assign = M * TOP_K
    counts = cnt[0, :N_EXPERTS].astype(jnp.int32)
    padded = ((counts + tb - 1) // tb) * tb
    pends = jnp.cumsum(padded)
    pstarts = pends - padded
    n_tiles = (n_assign + N_EXPERTS * (tb - 1) + tb - 1) // tb
    e = route_t[ROUTE_E1:ROUTE_E2 + 1].astype(jnp.int32)
    rank = route_t[ROUTE_R1:ROUTE_R2 + 1].astype(jnp.int32)
    before = jnp.arange(N_EXPERTS, dtype=jnp.int32)[:, None, None] < e[None]
    dest = jnp.sum(jnp.where(before, padded[:, None, None], 0), axis=0) + rank
    dest_flat = dest.reshape(-1)
    tok = jnp.tile(jnp.arange(M, dtype=jnp.int32), TOP_K)
    pad_tok = jnp.zeros((n_tiles * tb,), jnp.int32).at[dest_flat].set(tok)
    first_tile = (jnp.concatenate([pstarts, pends[-1:]]) // tb).astype(jnp.int32)
    return dest_flat, pad_tok, first_tile


def _forward(x, c, w_ada, b_ada, w_in, diff_lambda, diff_subln_g, rel_bias, w_o, ln_g, ln_b, w_group, b_group,
             w_router, b_router, w_gate, w_up, w_down, *, diff_tile, sb_tile, moe_tile, ada_tn):
    B, S, D = x.shape
    depth = w_ada.shape[0]
    M = B * S
    alpha = (2 * depth) ** 0.25

    c8 = jnp.zeros((8, D), F32).at[:B].set(c)
    mod = ada_modulation(c8, w_ada, b_ada, ada_tn)[:, :B]
    mod = mod.reshape(depth, B, 6, 1, D)
    bias_tiles = rel_bias_tiles(rel_bias, diff_tile)
    w_in_b, w_o_b = w_in.astype(BF16), w_o.astype(BF16)

    x2d = x.reshape(M, D)
    for l in range(depth):
        sh1, sc1, g1, sh2, sc2, g2 = (mod[l, :, n] for n in range(6))
        proj = in_projection(x2d, sc1, sh1, w_in_b, l, S)

        lam_init = 0.8 - 0.6 * math.exp(-0.3 * l)
        lp = diff_lambda[l].astype(F32)
        lam = jnp.exp(jnp.sum(lp[0] * lp[1])) - jnp.exp(jnp.sum(lp[2] * lp[3])) + lam_init
        a_out = diff_attention(proj, lam, bias_tiles, diff_subln_g[l], B, S, lam_init, diff_tile)
        b_out = sb_attention(proj, B, S, sb_tile)

        w_rt = jnp.zeros((D, LANES), F32).at[:, :N_GROUPS].set(w_group[l])
        w_rt = w_rt.at[:, N_GROUPS:N_GROUPS + N_EXPERTS].set(w_router[l])
        b_rt = jnp.zeros((1, LANES), F32).at[0, :N_GROUPS].set(b_group[l])
        b_rt = b_rt.at[0, N_GROUPS:N_GROUPS + N_EXPERTS].set(b_router[l])
        x1, u2, route, route_t, cnt = mix_ln_route(a_out, b_out, w_o_b, l, x2d, g1, sc2, sh2,
                                                   ln_g[l, 0], ln_b[l, 0], w_rt, b_rt, S, alpha)

        dest_flat, pad_tok, first_tile = moe_layout(route_t, cnt, moe_tile)
        yb = expert_mlp(pad_tok, first_tile, u2, w_gate, w_up, w_down, l, moe_tile)
        x2d = combine_ln(dest_flat, x1, route, g2, ln_g[l, 1], ln_b[l, 1], yb, S, alpha)
    return x2d.reshape(B, S, D)


def kernel(x, c, w_ada, b_ada, w_in, diff_lambda, diff_subln_g, rel_bias, w_o, ln_g, ln_b, w_group, b_group,
           w_router, b_router, w_gate, w_up, w_down):
    return _forward(x, c, w_ada, b_ada, w_in, diff_lambda, diff_subln_g, rel_bias, w_o, ln_g, ln_b, w_group,
                    b_group, w_router, b_router, w_gate, w_up, w_down,
                    diff_tile=512, sb_tile=256, moe_tile=256, ada_tn=1024)
```

```python
import functools
import math

import jax
import jax.numpy as jnp
import numpy as np
from jax import lax
from jax.experimental import pallas as pl
from jax.experimental.pallas import tpu as pltpu

F32 = jnp.float32
BF16 = jnp.bfloat16

N_DIFF_HEADS = 8
N_SB_HEADS = 8
HEAD_DIM = 128
DIFF_QK_DIM = 64
NUM_BUCKETS = 32
MAX_DISTANCE = 128
N_GROUPS = 4
EXPERTS_PER_GROUP = 8
N_EXPERTS = N_GROUPS * EXPERTS_PER_GROUP
TOP_K = 2
LN_EPS = 1e-5
SUBLN_EPS = 1e-5

LANES = 128
MASK_VALUE = -1e30
LOG2E = math.log2(math.e)
SB_EXIT_LOG2 = 104.0 * LOG2E
DIFF_ROW_CHUNK = 256
SB_HEADS_PER_STEP = 8
HI_HALF = 0xFFFF0000
OUT_ITEM_WIDTH = LANES
GATHER_SLOTS = 3
WEIGHT_CHUNKS = 4
WEIGHT_DMA_PRIORITY = 1
VMEM_LIMIT = 56 * 1024 * 1024


def _cparams(sem):
    return pltpu.CompilerParams(dimension_semantics=sem, vmem_limit_bytes=VMEM_LIMIT)


def _dot(a, b):
    return jnp.dot(a, b, preferred_element_type=F32)


def _dot_nt(a, b):
    return lax.dot_general(a, b, (((1,), (1,)), ((), ())), preferred_element_type=F32)


def _split(a):
    hi = a.astype(BF16)
    lo = (a - hi.astype(F32)).astype(BF16)
    return hi, lo


def _dot3(a, b):
    a_hi, a_lo = _split(a)
    b_hi, b_lo = _split(b)
    return _dot(a_hi, b_hi) + (_dot(a_hi, b_lo) + _dot(a_lo, b_hi))


def _layer_norm(h, g, b):
    mu = jnp.mean(h, axis=-1, keepdims=True)
    d = h - mu
    var = jnp.mean(d * d, axis=-1, keepdims=True)
    return d * lax.rsqrt(var + LN_EPS) * g + b


def _ada_kernel(c_ref, w_ref, b_ref, o_ref):
    c = c_ref[...]
    s = c * jax.nn.sigmoid(c)
    o_ref[...] = _dot3(s, w_ref[...]) + b_ref[...]


def ada_modulation(c8, w_ada, b_ada, tn=1024):
    L, D, N = w_ada.shape
    return pl.pallas_call(
        _ada_kernel,
        grid=(L, N // tn),
        in_specs=[
            pl.BlockSpec((8, D), lambda l, j: (0, 0)),
            pl.BlockSpec((None, D, tn), lambda l, j: (l, 0, j)),
            pl.BlockSpec((None, 1, tn), lambda l, j: (l, 0, j)),
        ],
        out_specs=pl.BlockSpec((None, 8, tn), lambda l, j: (l, 0, j)),
        out_shape=jax.ShapeDtypeStruct((L, 8, N), F32),
        compiler_params=_cparams(("arbitrary", "arbitrary")),
        name="ada",
    )(c8, w_ada, b_ada.reshape(L, 1, N))


def _inproj_kernel(x_ref, sc_ref, sh_ref, w_ref, o_ref, u_ref):
    @pl.when(pl.program_id(1) == 0)
    def _():
        u_ref[...] = (x_ref[...] * (1.0 + sc_ref[...]) + sh_ref[...]).astype(BF16)

    o_ref[...] = _dot(u_ref[...], w_ref[...]).astype(o_ref.dtype)


def in_projection(x2d, sc, sh, w_bf16, layer, seq, tm=1024, tn=1536):
    M, D = x2d.shape
    N = w_bf16.shape[2]
    tm = min(tm, seq)
    tn = min(tn, N)
    bpb = seq // tm
    return pl.pallas_call(
        _inproj_kernel,
        grid=(M // tm, N // tn),
        in_specs=[
            pl.BlockSpec((tm, D), lambda i, j: (i, 0)),
            pl.BlockSpec((None, 1, D), lambda i, j: (i // bpb, 0, 0)),
            pl.BlockSpec((None, 1, D), lambda i, j: (i // bpb, 0, 0)),
            pl.BlockSpec((None, D, tn), lambda i, j: (layer, 0, j)),
        ],
        out_specs=pl.BlockSpec((tm, tn), lambda i, j: (i, j)),
        out_shape=jax.ShapeDtypeStruct((M, N), BF16),
        scratch_shapes=[pltpu.VMEM((tm, D), BF16)],
        compiler_params=_cparams(("arbitrary", "arbitrary")),
        name="inproj",
    )(x2d, sc, sh, w_bf16)


def _t5_bucket_np(rel):
    n = np.maximum(rel, 0)
    max_exact = NUM_BUCKETS // 2
    ratio = np.maximum(n, 1).astype(np.float32) / np.float32(max_exact)
    large = max_exact + (np.log(ratio).astype(np.float32) / np.float32(math.log(MAX_DISTANCE / max_exact))
                         * np.float32(NUM_BUCKETS - max_exact)).astype(np.int32)
    large = np.minimum(large, NUM_BUCKETS - 1)
    return np.where(n < max_exact, n, large)


def _bucket_thresholds():
    buckets = _t5_bucket_np(np.arange(2 * MAX_DISTANCE))
    assert np.all(np.diff(buckets) >= 0) and buckets[MAX_DISTANCE] == NUM_BUCKETS - 1
    return [int(np.argmax(buckets >= k)) for k in range(NUM_BUCKETS)]


def _relbias_kernel(tab_ref, o_ref, *, t, n_heads, thresholds):
    h = pl.program_id(0)
    row = lax.broadcasted_iota(jnp.int32, (t, t), 0)
    col = lax.broadcasted_iota(jnp.int32, (t, t), 1)
    for blk in range(2):
        rel = row - col + blk * t
        for mp in range(2):
            entry = lambda k: tab_ref[(k * n_heads + h) * 2 + mp]
            far = entry(NUM_BUCKETS - 1)
            val = jnp.full((t, t), entry(0) - far, F32)
            for k in range(1, NUM_BUCKETS):
                val = jnp.where(rel >= thresholds[k], entry(k) - far, val)
            if blk == 0:
                val = jnp.where(rel < 0, MASK_VALUE, val)
            o_ref[blk, mp * t:(mp + 1) * t, :] = val


def rel_bias_tiles(rel_bias, t):
    assert t >= MAX_DISTANCE
    n_heads = rel_bias.shape[1]
    kern = functools.partial(_relbias_kernel, t=t, n_heads=n_heads, thresholds=_bucket_thresholds())
    grid_spec = pltpu.PrefetchScalarGridSpec(
        num_scalar_prefetch=1,
        grid=(n_heads,),
        in_specs=[],
        out_specs=pl.BlockSpec((None, 2, 2 * t, t), lambda h, tab: (h, 0, 0, 0)),
    )
    return pl.pallas_call(
        kern,
        grid_spec=grid_spec,
        out_shape=jax.ShapeDtypeStruct((n_heads, 2, 2 * t, t), F32),
        compiler_params=_cparams(("arbitrary",)),
        name="relbias",
    )(rel_bias.reshape(-1).astype(F32))


def _diff_kernel(lam_ref, q_ref, k_ref, v_ref, bias_ref, g_ref, o_ref, qs_ref, qn_ref, s_ref, m_ref, acc_ref,
                 *, t, rc, nq, out_scale):
    ones = jnp.ones((t, HEAD_DIM), BF16)

    def keys(j):
        return pl.ds(pl.multiple_of(j * t, t), t)

    def stack_queries(qi, dst_ref):
        q = q_ref[keys(qi), :] * BF16(DIFF_QK_DIM ** -0.5)
        lane = lax.broadcasted_iota(jnp.int32, q.shape, 1)
        zero = jnp.zeros_like(q)
        dst_ref[0:t, :] = jnp.where(lane < DIFF_QK_DIM, q, zero)
        dst_ref[t:2 * t, :] = jnp.where(lane >= DIFF_QK_DIM, q, zero)

    def step(j, bias_blk, next_queries_ref, next_j):
        vb = v_ref[keys(j), :]
        vb1 = jnp.concatenate([vb, ones], axis=1)
        kb_next = k_ref[keys(next_j), :]
        diagonal = bias_blk == 0
        bias_reach = rc + MAX_DISTANCE
        for r in range(2 * t // rc):
            rows = pl.ds(r * rc, rc)
            nc = (r % (t // rc) + 1) * rc if diagonal else t
            s = s_ref[rows, :nc]
            s_ref[rows, :] = _dot_nt(next_queries_ref[rows, :], kb_next)
            if bias_blk is not None:
                c = r % (t // rc)
                if diagonal or c == 0:
                    b0 = max(nc - bias_reach, 0)
                    biased = s[:, b0:] + bias_ref[bias_blk, rows, b0:nc]
                    s = biased if b0 == 0 else jnp.concatenate([s[:, :b0], biased], axis=1)
            m_old = m_ref[rows, :]
            m_new = jnp.maximum(m_old, jnp.max(s, axis=1, keepdims=True))
            p = jnp.exp(s - jnp.concatenate([m_new] * (nc // LANES), axis=1)).astype(BF16)
            alpha = jnp.exp(m_old - m_new)
            acc_ref[rows, :] = jnp.concatenate([alpha, alpha], axis=1) * acc_ref[rows, :] + _dot(p, vb1[:nc])
            m_ref[rows, :] = m_new

    stack_queries(0, qs_ref)
    kb0 = k_ref[keys(0), :]
    for r in range(2 * t // rc):
        rows = pl.ds(r * rc, rc)
        s_ref[rows, :] = _dot_nt(qs_ref[rows, :], kb0)

    def reset():
        m_ref[...] = jnp.full(m_ref.shape, MASK_VALUE, F32)
        acc_ref[...] = jnp.zeros(acc_ref.shape, F32)

    def diagonal_step_and_output(qi):
        stack_queries(jnp.minimum(qi + 1, nq - 1), qn_ref)
        step(qi, 0, qn_ref, 0)
        acc = acc_ref[...]
        ratio = acc[:, :HEAD_DIM] / acc[:, HEAD_DIM:]
        o = ratio[:t] - lam_ref[0] * ratio[t:]
        o = o * lax.rsqrt(jnp.mean(o * o, axis=-1, keepdims=True) + SUBLN_EPS)
        o_ref[keys(qi), :] = (o * g_ref[...] * out_scale).astype(o_ref.dtype)
        qs_ref[...] = qn_ref[...]

    reset()
    diagonal_step_and_output(0)

    def tile_body(qi, carry):
        reset()
        n_far = qi - 1

        def far_pair(p, c):
            step(2 * p, None, qs_ref, 2 * p + 1)
            step(2 * p + 1, None, qs_ref, 2 * p + 2)
            return c

        lax.fori_loop(0, n_far // 2, far_pair, 0)

        @pl.when(n_far % 2 == 1)
        def _():
            step(n_far - 1, None, qs_ref, n_far)

        step(qi - 1, 1, qs_ref, qi)
        diagonal_step_and_output(qi)
        return carry

    lax.fori_loop(1, nq, tile_body, 0)


def diff_attention(proj, lam, bias_tiles, subln_g, batch, seq, lam_init, t):
    M = proj.shape[0]
    H = N_DIFF_HEADS
    nq = seq // t
    rc = min(DIFF_ROW_CHUNK, t)
    assert rc % LANES == 0 and rc >= MAX_DISTANCE and MAX_DISTANCE % LANES == 0
    kern = functools.partial(_diff_kernel, t=t, rc=rc, nq=nq, out_scale=1.0 - lam_init)
    grid_spec = pltpu.PrefetchScalarGridSpec(
        num_scalar_prefetch=1,
        grid=(H, batch),
        in_specs=[
            pl.BlockSpec((seq, HEAD_DIM), lambda h, b, s: (b, h)),
            pl.BlockSpec((seq, HEAD_DIM), lambda h, b, s: (b, H + h)),
            pl.BlockSpec((seq, HEAD_DIM), lambda h, b, s: (b, 2 * H + h)),
            pl.BlockSpec((None, 2, 2 * t, t), lambda h, b, s: (h, 0, 0, 0)),
            pl.BlockSpec((1, HEAD_DIM), lambda h, b, s: (0, 0)),
        ],
        out_specs=pl.BlockSpec((seq, HEAD_DIM), lambda h, b, s: (b, h)),
        scratch_shapes=[
            pltpu.VMEM((2 * t, HEAD_DIM), BF16),
            pltpu.VMEM((2 * t, HEAD_DIM), BF16),
            pltpu.VMEM((2 * t, t), F32),
            pltpu.VMEM((2 * t, LANES), F32),
            pltpu.VMEM((2 * t, 2 * HEAD_DIM), F32),
        ],
    )
    return pl.pallas_call(
        kern,
        grid_spec=grid_spec,
        out_shape=jax.ShapeDtypeStruct((M, H * HEAD_DIM), BF16),
        compiler_params=_cparams(("arbitrary", "arbitrary")),
        name="diffattn",
    )(lam.reshape(1).astype(F32), proj, proj, proj, bias_tiles, subln_g.reshape(1, HEAD_DIM))


def _sb_kernel(q_ref, k_ref, v_ref, o_ref, carry_ref, acc_ref, *, t, hp, scale2):
    qi = pl.program_id(2)

    def keys(j):
        return pl.ds(pl.multiple_of(j * t, t), t)

    def upper_ones(n):
        row = lax.broadcasted_iota(jnp.int32, (n, n), 0)
        col = lax.broadcasted_iota(jnp.int32, (n, n), 1)
        return jnp.where(row > col, 1.0, 0.0).astype(BF16)

    def sticks(z2, valid):
        lp = jnp.log2(1.0 + jnp.exp2(-jnp.abs(z2)))
        drop = jnp.maximum(z2, 0.0) + lp
        log2_beta = z2 - drop
        if valid is not None:
            drop = jnp.where(valid, drop, 0.0)
        return drop, log2_beta

    prev = jnp.maximum(qi - 1, 0)
    qrow = lax.broadcasted_iota(jnp.int32, (t, 2 * t), 0)
    kcol = lax.broadcasted_iota(jnp.int32, (t, 2 * t), 1)
    n_prev = jnp.where(qi > 0, t, 0)
    valid = (kcol < n_prev) | ((kcol >= t) & (kcol - t < qrow))
    upper = upper_ones(t)
    for hh in range(hp):
        cols = slice(hh * HEAD_DIM, (hh + 1) * HEAD_DIM)
        kw = jnp.concatenate([k_ref[keys(prev), cols], k_ref[keys(qi), cols]], axis=0)
        vw = jnp.concatenate([v_ref[keys(prev), cols], v_ref[keys(qi), cols]], axis=0)
        drop, log2_beta = sticks(_dot_nt(q_ref[:, cols], kw) * scale2, valid)
        hi, lo = _split(drop)
        stacked = jnp.concatenate([hi[:, :t], lo[:, :t], hi[:, t:], lo[:, t:]], axis=0)
        within = _dot(stacked, upper)
        diag_total = jnp.sum(drop[:, t:], axis=1, keepdims=True)
        later = jnp.concatenate([within[0:t] + within[t:2 * t] + diag_total,
                                 within[2 * t:3 * t] + within[3 * t:4 * t]], axis=1)
        a = jnp.where(valid, jnp.exp2(log2_beta - later), 0.0)
        acc_ref[:, cols] = _dot(a.astype(BF16), vw)
        prev_total = jnp.sum(drop[:, :t], axis=1, keepdims=True)
        carry_ref[hh] = jnp.broadcast_to(prev_total + diag_total, (t, LANES))

    def step(j):
        for hh in range(hp):
            cols = slice(hh * HEAD_DIM, (hh + 1) * HEAD_DIM)
            drop, log2_beta = sticks(_dot_nt(q_ref[:, cols], k_ref[keys(j), cols]) * scale2, None)
            hi, lo = _split(drop)
            later = _dot(hi, upper) + _dot(lo, upper)
            carry = carry_ref[hh]
            a = jnp.exp2(log2_beta - later - jnp.concatenate([carry] * (t // LANES), axis=1))
            acc_ref[:, cols] += _dot(a.astype(BF16), v_ref[keys(j), cols])
            carry_ref[hh] = carry + jnp.sum(drop, axis=1, keepdims=True)

    def cond(state):
        n, live = state
        return jnp.logical_and(n < qi - 1, live < SB_EXIT_LOG2)

    def body(state):
        n, _ = state
        step(qi - 2 - n)
        return n + 1, jnp.min(carry_ref[...])

    lax.while_loop(cond, body, (jnp.int32(0), jnp.min(carry_ref[...])))
    o_ref[...] = acc_ref[...].astype(o_ref.dtype)


def sb_attention(proj, batch, seq, t, hp=SB_HEADS_PER_STEP):
    M = proj.shape[0]
    H = N_SB_HEADS
    base = 3 * N_DIFF_HEADS // hp
    nq = seq // t
    w = hp * HEAD_DIM
    kern = functools.partial(_sb_kernel, t=t, hp=hp, scale2=HEAD_DIM ** -0.5 * LOG2E)
    return pl.pallas_call(
        kern,
        grid=(H // hp, batch, nq),
        in_specs=[
            pl.BlockSpec((t, w), lambda h, b, qi: (b * nq + qi, base + h)),
            pl.BlockSpec((seq, w), lambda h, b, qi: (b, base + H // hp + h)),
            pl.BlockSpec((seq, w), lambda h, b, qi: (b, base + 2 * (H // hp) + h)),
        ],
        out_specs=pl.BlockSpec((t, w), lambda h, b, qi: (b * nq + qi, h)),
        out_shape=jax.ShapeDtypeStruct((M, H * HEAD_DIM), BF16),
        scratch_shapes=[pltpu.VMEM((hp, t, LANES), F32), pltpu.VMEM((t, w), F32)],
        compiler_params=_cparams(("arbitrary", "arbitrary", "arbitrary")),
        name="sbattn",
    )(proj, proj, proj)


ROUTE_E1, ROUTE_E2, ROUTE_W1, ROUTE_W2, ROUTE_R1, ROUTE_R2 = range(6)


def _mix_kernel(a_ref, b_ref, wo_ref, x_ref, g1_ref, sc2_ref, sh2_ref, lng_ref, lnb_ref, wrt_ref, brt_ref,
                x1_ref, u2_ref, route_ref, route_t_ref, cnt_ref, carry_ref, wsplit_ref, *, alpha, half):
    i = pl.program_id(0)

    @pl.when(i == 0)
    def _():
        carry_ref[...] = jnp.zeros(carry_ref.shape, F32)
        w_hi, w_lo = _split(wrt_ref[...])
        wsplit_ref[...] = jnp.concatenate([w_hi, w_lo], axis=1)

    mix = _dot(a_ref[...], wo_ref[pl.ds(0, half), :]) + _dot(b_ref[...], wo_ref[pl.ds(half, half), :])
    h = alpha * x_ref[...] + (1.0 + g1_ref[...]) * mix
    x1 = _layer_norm(h, lng_ref[...], lnb_ref[...])
    x1_ref[...] = x1
    u2 = x1 * (1.0 + sc2_ref[...]) + sh2_ref[...]
    nseg = u2.shape[1] // (2 * LANES)
    for s in range(nseg):
        lo = u2[:, 2 * s * LANES:(2 * s + 1) * LANES].astype(BF16).astype(F32)
        hi = u2[:, (2 * s + 1) * LANES:(2 * s + 2) * LANES].astype(BF16).astype(F32)
        lo_bits = lax.shift_right_logical(lax.bitcast_convert_type(lo, jnp.uint32), jnp.uint32(16))
        hi_bits = lax.bitcast_convert_type(hi, jnp.uint32) & jnp.uint32(HI_HALF)
        u2_ref[pl.ds(s, u2.shape[0], stride=nseg), :] = lo_bits | hi_bits

    u_hi, u_lo = _split(u2)
    hh_hl = _dot(u_hi, wsplit_ref[...])
    lh = _dot(u_lo, wsplit_ref[:, :LANES])
    logits = hh_hl[:, :LANES] + (hh_hl[:, LANES:] + lh) + brt_ref[...]
    tm = logits.shape[0]
    lane = lax.broadcasted_iota(jnp.int32, (tm, LANES), 1).astype(F32)
    gmask = lane < N_GROUPS
    gl = jnp.where(gmask, logits, MASK_VALUE)
    gmax = jnp.max(gl, axis=1, keepdims=True)
    gidx = jnp.min(jnp.where(gmask & (gl == gmax), lane, float(LANES)), axis=1, keepdims=True)
    gsum = jnp.sum(jnp.where(gmask, jnp.exp(gl - gmax), 0.0), axis=1, keepdims=True)
    gp = 1.0 / gsum

    lo = N_GROUPS + gidx * EXPERTS_PER_GROUP
    emask = (lane >= lo) & (lane < lo + EXPERTS_PER_GROUP)
    el = jnp.where(emask, logits, MASK_VALUE)
    v1 = jnp.max(el, axis=1, keepdims=True)
    i1 = jnp.min(jnp.where(emask & (el == v1), lane, float(LANES)), axis=1, keepdims=True)
    emask2 = emask & (lane != i1)
    el2 = jnp.where(emask2, logits, MASK_VALUE)
    v2 = jnp.max(el2, axis=1, keepdims=True)
    i2 = jnp.min(jnp.where(emask2 & (el2 == v2), lane, float(LANES)), axis=1, keepdims=True)
    tt = jnp.exp(v2 - v1)
    w1 = gp / (1.0 + tt)
    w2 = gp * tt / (1.0 + tt)
    e1 = i1 - N_GROUPS
    e2 = i2 - N_GROUPS

    oh1 = lane == e1
    oh2 = lane == e2
    oh1b = jnp.where(oh1, 1.0, 0.0).astype(BF16)
    oh2b = jnp.where(oh2, 1.0, 0.0).astype(BF16)
    row = lax.broadcasted_iota(jnp.int32, (tm, tm), 0)
    col = lax.broadcasted_iota(jnp.int32, (tm, tm), 1)
    lower = jnp.where(col < row, 1.0, 0.0).astype(BF16)
    c1 = jnp.sum(jnp.where(oh1, 1.0, 0.0), axis=0, keepdims=True)
    c2 = jnp.sum(jnp.where(oh2, 1.0, 0.0), axis=0, keepdims=True)
    carry = carry_ref[...]
    r1 = _dot(lower, oh1b) + carry
    r2 = _dot(lower, oh2b) + (carry + c1)
    rank1 = jnp.sum(jnp.where(oh1, r1, 0.0), axis=1, keepdims=True)
    rank2 = jnp.sum(jnp.where(oh2, r2, 0.0), axis=1, keepdims=True)
    carry = carry + c1 + c2
    carry_ref[...] = carry
    cnt_ref[...] = jnp.broadcast_to(carry, cnt_ref.shape)

    out = jnp.zeros((tm, LANES), F32)
    for ln, val in ((ROUTE_E1, e1), (ROUTE_E2, e2), (ROUTE_W1, w1), (ROUTE_W2, w2),
                    (ROUTE_R1, rank1), (ROUTE_R2, rank2)):
        out = jnp.where(lane == float(ln), val, out)
    route_ref[...] = out
    route_t_ref[...] = out.T[:8]


def mix_ln_route(a_out, b_out, wo_bf16, layer, x2d, g1, sc2, sh2, ln_g, ln_b, w_rt, b_rt, seq, alpha, tm=512):
    M, D = x2d.shape
    half = a_out.shape[1]
    bpb = seq // tm
    nseg = D // (2 * LANES)
    row = lambda i: (i // bpb, 0, 0)
    full = lambda i: (0, 0)
    kern = functools.partial(_mix_kernel, alpha=alpha, half=half)
    return pl.pallas_call(
        kern,
        grid=(M // tm,),
        in_specs=[
            pl.BlockSpec((tm, half), lambda i: (i, 0)),
            pl.BlockSpec((tm, half), lambda i: (i, 0)),
            pl.BlockSpec((None, 2 * half, D), lambda i: (layer, 0, 0)),
            pl.BlockSpec((tm, D), lambda i: (i, 0)),
            pl.BlockSpec((None, 1, D), row),
            pl.BlockSpec((None, 1, D), row),
            pl.BlockSpec((None, 1, D), row),
            pl.BlockSpec((1, D), full),
            pl.BlockSpec((1, D), full),
            pl.BlockSpec((D, LANES), full),
            pl.BlockSpec((1, LANES), full),
        ],
        out_specs=[
            pl.BlockSpec((tm, D), lambda i: (i, 0)),
            pl.BlockSpec((tm * nseg, LANES), lambda i: (i, 0)),
            pl.BlockSpec((tm, LANES), lambda i: (i, 0)),
            pl.BlockSpec((8, tm), lambda i: (0, i)),
            pl.BlockSpec((8, LANES), full),
        ],
        out_shape=[
            jax.ShapeDtypeStruct((M, D), F32),
            jax.ShapeDtypeStruct((M * nseg, LANES), jnp.uint32),
            jax.ShapeDtypeStruct((M, LANES), F32),
            jax.ShapeDtypeStruct((8, M), F32),
            jax.ShapeDtypeStruct((8, LANES), F32),
        ],
        scratch_shapes=[pltpu.VMEM((1, LANES), F32), pltpu.VMEM((D, 2 * LANES), BF16)],
        compiler_params=_cparams(("arbitrary",)),
        name="mix",
    )(a_out, b_out, wo_bf16, x2d, g1, sc2, sh2, ln_g.reshape(1, D), ln_b.reshape(1, D), w_rt, b_rt)


def _item_copy(src_hbm, dst_vmem, src_item, dst_item, rpi, sem):
    src = src_hbm.at[pl.ds(pl.multiple_of(src_item * rpi, rpi), rpi), :]
    dst = dst_vmem.at[pl.ds(pl.multiple_of(dst_item * rpi, rpi), rpi), :]
    return pltpu.make_async_copy(src, dst, sem)


def _start_items(idx_ref, base, n_items, rpi, src_hbm, dst_vmem, sem, unroll=False):
    def issue(r, c):
        _item_copy(src_hbm, dst_vmem, idx_ref[base + r], r, rpi, sem).start()
        return c

    lax.fori_loop(0, n_items, issue, 0, unroll=unroll)


def _wait_items(n_items, rpi, src_hbm, dst_vmem, sem):
    pltpu.make_async_copy(src_hbm.at[pl.ds(0, n_items * rpi), :], dst_vmem, sem).wait()


def _expert_kernel(tok_ref, first_ref, u_hbm, wg_hbm, wu_hbm, wd_hbm, y_hbm,
                   xbuf_ref, xs_ref, obuf_ref, stage_gu_ref, stage_d_ref, wg_ref, wu_ref, wd_ref,
                   gsems, osems, wsems, *, layer, tb, nseg, nrow, n_tiles):
    e = pl.program_id(0)
    n_exp = pl.num_programs(0)
    n_used = first_ref[n_exp]
    dc = wg_ref.shape[0] // WEIGHT_CHUNKS
    fc = wd_ref.shape[0] // WEIGHT_CHUNKS

    def weight_copies(expert, c):
        drows = pl.ds(c * dc, dc)
        frows = pl.ds(c * fc, fc)
        return (pltpu.make_async_copy(wg_hbm.at[layer, expert, drows, :], stage_gu_ref.at[c, 0], wsems.at[c]),
                pltpu.make_async_copy(wu_hbm.at[layer, expert, drows, :], stage_gu_ref.at[c, 1], wsems.at[c]),
                pltpu.make_async_copy(wd_hbm.at[layer, expert, frows, :], stage_d_ref.at[c], wsems.at[c]))

    def start_gather(k, unroll):
        s = k % GATHER_SLOTS
        tile = jnp.minimum(k, n_used - 1)
        _start_items(tok_ref, tile * tb, tb, nseg, u_hbm, xbuf_ref.at[s], gsems.at[s], unroll)

    def wait_gather(k):
        s = k % GATHER_SLOTS
        _wait_items(tb, nseg, u_hbm, xbuf_ref.at[s], gsems.at[s])

    def out_copy(tile, s):
        rows = pl.ds(pl.multiple_of(tile * (tb * nrow), tb * nrow), tb * nrow)
        return pltpu.make_async_copy(obuf_ref.at[s], y_hbm.at[rows, :], osems.at[s])

    @pl.when(e == 0)
    def _():
        for c in range(WEIGHT_CHUNKS):
            for cp in weight_copies(0, c):
                cp.start(priority=WEIGHT_DMA_PRIORITY)
        for k in range(GATHER_SLOTS - 1):
            start_gather(k, False)

    for c in range(WEIGHT_CHUNKS):
        for cp in weight_copies(e, c):
            cp.wait()
        wg_ref[c * dc:(c + 1) * dc, :] = stage_gu_ref[c, 0].astype(BF16)
        wu_ref[c * dc:(c + 1) * dc, :] = stage_gu_ref[c, 1].astype(BF16)
        wd_ref[c * fc:(c + 1) * fc, :] = stage_d_ref[c].astype(BF16)

        @pl.when(e + 1 < n_exp)
        def _():
            for cp in weight_copies(e + 1, c):
                cp.start(priority=WEIGHT_DMA_PRIORITY)

    def tile_body(i, carry):
        slot = i % GATHER_SLOTS
        oslot = i % 2

        @pl.when(i >= 2)
        def _():
            out_copy(i - 2, oslot).wait()

        wait_gather(i)
        start_gather(i + GATHER_SLOTS - 1, True)
        for s in range(nseg):
            w = xbuf_ref[slot, pl.ds(s, tb, stride=nseg), :]
            lo = lax.bitcast_convert_type(lax.shift_left(w, jnp.uint32(16)), F32)
            hi = lax.bitcast_convert_type(w & jnp.uint32(HI_HALF), F32)
            xs_ref[:, 2 * s * LANES:(2 * s + 1) * LANES] = lo.astype(BF16)
            xs_ref[:, (2 * s + 1) * LANES:(2 * s + 2) * LANES] = hi.astype(BF16)
        x = xs_ref[...]
        g = _dot(x, wg_ref[...])
        u = _dot(x, wu_ref[...])
        h = (g * jax.nn.sigmoid(g) * u).astype(BF16)
        y = _dot(h, wd_ref[...])
        for s in range(nrow):
            obuf_ref[oslot, pl.ds(s, tb, stride=nrow), :] = y[:, s * OUT_ITEM_WIDTH:(s + 1) * OUT_ITEM_WIDTH]
        out_copy(i, oslot).start()
        return carry

    lax.fori_loop(first_ref[e], first_ref[e + 1], tile_body, 0)

    @pl.when(e == n_exp - 1)
    def _():
        for k in range(GATHER_SLOTS - 1):
            wait_gather(n_used + k)
        for back in (1, 2):

            @pl.when(n_used >= back)
            def _():
                out_copy(n_used - back, (n_used - back) % 2).wait()

        obuf_ref[0] = jnp.zeros(obuf_ref.shape[1:], obuf_ref.dtype)

        def zero_tile(i, carry):
            cp = out_copy(i, 0)
            cp.start()
            cp.wait()
            return carry

        lax.fori_loop(n_used, n_tiles, zero_tile, 0)


def expert_mlp(pad_tok, first_tile, u2p, wg, wu, wd, layer, tb):
    P = pad_tok.shape[0]
    E, D, F = wg.shape[1:]
    nseg = D // (2 * LANES)
    nrow = D // OUT_ITEM_WIDTH
    fc = F // WEIGHT_CHUNKS
    any_spec = pl.BlockSpec(memory_space=pl.ANY)
    grid_spec = pltpu.PrefetchScalarGridSpec(
        num_scalar_prefetch=2,
        grid=(E,),
        in_specs=[any_spec, any_spec, any_spec, any_spec],
        out_specs=any_spec,
        scratch_shapes=[
            pltpu.VMEM((GATHER_SLOTS, tb * nseg, LANES), jnp.uint32),
            pltpu.VMEM((tb, D), BF16),
            pltpu.VMEM((2, tb * nrow, OUT_ITEM_WIDTH), F32),
            pltpu.VMEM((WEIGHT_CHUNKS, 2, D // WEIGHT_CHUNKS, F), F32),
            pltpu.VMEM((WEIGHT_CHUNKS, fc, D), F32),
            pltpu.VMEM((D, F), BF16),
            pltpu.VMEM((D, F), BF16),
            pltpu.VMEM((F, D), BF16),
            pltpu.SemaphoreType.DMA((GATHER_SLOTS,)),
            pltpu.SemaphoreType.DMA((2,)),
            pltpu.SemaphoreType.DMA((WEIGHT_CHUNKS,)),
        ],
    )
    return pl.pallas_call(
        functools.partial(_expert_kernel, layer=layer, tb=tb, nseg=nseg, nrow=nrow, n_tiles=P // tb),
        grid_spec=grid_spec,
        out_shape=jax.ShapeDtypeStruct((P * nrow, OUT_ITEM_WIDTH), F32),
        compiler_params=_cparams(("arbitrary",)),
        name="expert",
    )(pad_tok, first_tile, u2p, wg, wu, wd)


def _final_kernel(dest_ref, x1_ref, route_ref, g2_ref, lng_ref, lnb_ref, yb_hbm, o_ref, buf_ref, sems,
                  *, alpha, tm, m, nrow):
    i = pl.program_id(0)
    slot = i % GATHER_SLOTS
    last = pl.num_programs(0) - 1

    def start(n, unroll):
        s = n % GATHER_SLOTS
        tile = jnp.minimum(n, last)
        for k in range(TOP_K):
            _start_items(dest_ref, k * m + tile * tm, tm, nrow, yb_hbm, buf_ref.at[s, k], sems.at[s], unroll)

    def wait(n):
        s = n % GATHER_SLOTS
        for k in range(TOP_K):
            _wait_items(tm, nrow, yb_hbm, buf_ref.at[s, k], sems.at[s])

    @pl.when(i == 0)
    def _():
        for n in range(GATHER_SLOTS - 1):
            start(n, False)

    wait(i)
    start(i + GATHER_SLOTS - 1, True)

    def expert_out(k):
        chunks = [buf_ref[slot, k, pl.ds(s, tm, stride=nrow), :] for s in range(nrow)]
        return jnp.concatenate(chunks, axis=1)

    route = route_ref[...]
    w1 = route[:, ROUTE_W1:ROUTE_W1 + 1]
    w2 = route[:, ROUTE_W2:ROUTE_W2 + 1]
    y = w1 * expert_out(0) + w2 * expert_out(1)
    h = alpha * x1_ref[...] + (1.0 + g2_ref[...]) * y
    o_ref[...] = _layer_norm(h, lng_ref[...], lnb_ref[...])

    @pl.when(i == last)
    def _():
        for n in range(1, GATHER_SLOTS):
            wait(i + n)


def combine_ln(dest_flat, x1, route, g2, ln_g, ln_b, yb, seq, alpha, tm=256):
    M, D = x1.shape
    bpb = seq // tm
    nrow = D // OUT_ITEM_WIDTH
    grid_spec = pltpu.PrefetchScalarGridSpec(
        num_scalar_prefetch=1,
        grid=(M // tm,),
        in_specs=[
            pl.BlockSpec((tm, D), lambda i, d: (i, 0)),
            pl.BlockSpec((tm, LANES), lambda i, d: (i, 0)),
            pl.BlockSpec((None, 1, D), lambda i, d: (i // bpb, 0, 0)),
            pl.BlockSpec((1, D), lambda i, d: (0, 0)),
            pl.BlockSpec((1, D), lambda i, d: (0, 0)),
            pl.BlockSpec(memory_space=pl.ANY),
        ],
        out_specs=pl.BlockSpec((tm, D), lambda i, d: (i, 0)),
        scratch_shapes=[pltpu.VMEM((GATHER_SLOTS, TOP_K, tm * nrow, OUT_ITEM_WIDTH), F32),
                        pltpu.SemaphoreType.DMA((GATHER_SLOTS,))],
    )
    return pl.pallas_call(
        functools.partial(_final_kernel, alpha=alpha, tm=tm, m=M, nrow=nrow),
        grid_spec=grid_spec,
        out_shape=jax.ShapeDtypeStruct((M, D), F32),
        compiler_params=_cparams(("arbitrary",)),
        name="final",
    )(dest_flat, x1, route, g2, ln_g.reshape(1, D), ln_b.reshape(1, D), yb)


def moe_layout(route_t, cnt, tb):
    M = route_t.shape[1]
    n_assign = M * TOP_K
    counts = cnt[0, :N_EXPERTS].astype(jnp.int32)
    padded = ((counts + tb - 1) // tb) * tb
    pends = jnp.cumsum(padded)
    pstarts = pends - padded
    n_tiles = (n_assign + N_EXPERTS * (tb - 1) + tb - 1) // tb
    e = route_t[ROUTE_E1:ROUTE_E2 + 1].astype(jnp.int32)
    rank = route_t[ROUTE_R1:ROUTE_R2 + 1].astype(jnp.int32)
    before = jnp.arange(N_EXPERTS, dtype=jnp.int32)[:, None, None] < e[None]
    dest = jnp.sum(jnp.where(before, padded[:, None, None], 0), axis=0) + rank
    dest_flat = dest.reshape(-1)
    tok = jnp.tile(jnp.arange(M, dtype=jnp.int32), TOP_K)
    pad_tok = jnp.zeros((n_tiles * tb,), jnp.int32).at[dest_flat].set(tok)
    first_tile = (jnp.concatenate([pstarts, pends[-1:]]) // tb).astype(jnp.int32)
    return dest_flat, pad_tok, first_tile


def _forward(x, c, w_ada, b_ada, w_in, diff_lambda, diff_subln_g, rel_bias, w_o, ln_g, ln_b, w_group, b_group,
             w_router, b_router, w_gate, w_up, w_down, *, diff_tile, sb_tile, moe_tile, ada_tn):
    B, S, D = x.shape
    depth = w_ada.shape[0]
    M = B * S
    alpha = (2 * depth) ** 0.25

    c8 = jnp.zeros((8, D), F32).at[:B].set(c)
    mod = ada_modulation(c8, w_ada, b_ada, ada_tn)[:, :B]
    mod = mod.reshape(depth, B, 6, 1, D)
    bias_tiles = rel_bias_tiles(rel_bias, diff_tile)
    w_in_b, w_o_b = w_in.astype(BF16), w_o.astype(BF16)

    x2d = x.reshape(M, D)
    for l in range(depth):
        sh1, sc1, g1, sh2, sc2, g2 = (mod[l, :, n] for n in range(6))
        proj = in_projection(x2d, sc1, sh1, w_in_b, l, S)

        lam_init = 0.8 - 0.6 * math.exp(-0.3 * l)
        lp = diff_lambda[l].astype(F32)
        lam = jnp.exp(jnp.sum(lp[0] * lp[1])) - jnp.exp(jnp.sum(lp[2] * lp[3])) + lam_init
        a_out = diff_attention(proj, lam, bias_tiles, diff_subln_g[l], B, S, lam_init, diff_tile)
        b_out = sb_attention(proj, B, S, sb_tile)

        w_rt = jnp.zeros((D, LANES), F32).at[:, :N_GROUPS].set(w_group[l])
        w_rt = w_rt.at[:, N_GROUPS:N_GROUPS + N_EXPERTS].set(w_router[l])
        b_rt = jnp.zeros((1, LANES), F32).at[0, :N_GROUPS].set(b_group[l])
        b_rt = b_rt.at[0, N_GROUPS:N_GROUPS + N_EXPERTS].set(b_router[l])
        x1, u2, route, route_t, cnt = mix_ln_route(a_out, b_out, w_o_b, l, x2d, g1, sc2, sh2,
                                                   ln_g[l, 0], ln_b[l, 0], w_rt, b_rt, S, alpha)

        dest_flat, pad_tok, first_tile = moe_layout(route_t, cnt, moe_tile)
        yb = expert_mlp(pad_tok, first_tile, u2, w_gate, w_up, w_down, l, moe_tile)
        x2d = combine_ln(dest_flat, x1, route, g2, ln_g[l, 1], ln_b[l, 1], yb, S, alpha)
    return x2d.reshape(B, S, D)


def kernel(x, c, w_ada, b_ada, w_in, diff_lambda, diff_subln_g, rel_bias, w_o, ln_g, ln_b, w_group, b_group,
           w_router, b_router, w_gate, w_up, w_down):
    return _forward(x, c, w_ada, b_ada, w_in, diff_lambda, diff_subln_g, rel_bias, w_o, ln_g, ln_b, w_group,
                    b_group, w_router, b_router, w_gate, w_up, w_down,
                    diff_tile=512, sb_tile=256, moe_tile=256, ada_tn=2048)
```

```python
import functools
import math

import jax
import jax.numpy as jnp
import numpy as np
from jax import lax
from jax.experimental import pallas as pl
from jax.experimental.pallas import tpu as pltpu

F32 = jnp.float32
BF16 = jnp.bfloat16

N_DIFF_HEADS = 8
N_SB_HEADS = 8
HEAD_DIM = 128
DIFF_QK_DIM = 64
NUM_BUCKETS = 32
MAX_DISTANCE = 128
N_GROUPS = 4
EXPERTS_PER_GROUP = 8
N_EXPERTS = N_GROUPS * EXPERTS_PER_GROUP
TOP_K = 2
LN_EPS = 1e-5
SUBLN_EPS = 1e-5

LANES = 128
MASK_VALUE = -1e30
LOG2E = math.log2(math.e)
SB_EXIT_LOG2 = 104.0 * LOG2E
DIFF_ROW_CHUNK = 256
SB_HEADS_PER_STEP = 8
HI_HALF = 0xFFFF0000
OUT_ITEM_WIDTH = LANES
GATHER_SLOTS = 3
WEIGHT_CHUNKS = 4
WEIGHT_DMA_PRIORITY = 1
VMEM_LIMIT = 56 * 1024 * 1024


def _cparams(sem):
    return pltpu.CompilerParams(dimension_semantics=sem, vmem_limit_bytes=VMEM_LIMIT)


def _dot(a, b):
    return jnp.dot(a, b, preferred_element_type=F32)


def _dot_nt(a, b):
    return lax.dot_general(a, b, (((1,), (1,)), ((), ())), preferred_element_type=F32)


def _split(a):
    hi = a.astype(BF16)
    lo = (a - hi.astype(F32)).astype(BF16)
    return hi, lo


def _dot3(a, b):
    a_hi, a_lo = _split(a)
    b_hi, b_lo = _split(b)
    return _dot(a_hi, b_hi) + (_dot(a_hi, b_lo) + _dot(a_lo, b_hi))


def _layer_norm(h, g, b):
    mu = jnp.mean(h, axis=-1, keepdims=True)
    d = h - mu
    var = jnp.mean(d * d, axis=-1, keepdims=True)
    return d * lax.rsqrt(var + LN_EPS) * g + b


def _ada_kernel(c_ref, w_ref, b_ref, o_ref):
    c = c_ref[...]
    s = c * jax.nn.sigmoid(c)
    o_ref[...] = _dot3(s, w_ref[...]) + b_ref[...]


def ada_modulation(c8, w_ada, b_ada, tn=1024):
    L, D, N = w_ada.shape
    return pl.pallas_call(
        _ada_kernel,
        grid=(L, N // tn),
        in_specs=[
            pl.BlockSpec((8, D), lambda l, j: (0, 0)),
            pl.BlockSpec((None, D, tn), lambda l, j: (l, 0, j)),
            pl.BlockSpec((None, 1, tn), lambda l, j: (l, 0, j)),
        ],
        out_specs=pl.BlockSpec((None, 8, tn), lambda l, j: (l, 0, j)),
        out_shape=jax.ShapeDtypeStruct((L, 8, N), F32),
        compiler_params=_cparams(("arbitrary", "arbitrary")),
        name="ada",
    )(c8, w_ada, b_ada.reshape(L, 1, N))


def _inproj_kernel(x_ref, sc_ref, sh_ref, w_ref, o_ref, u_ref):
    @pl.when(pl.program_id(1) == 0)
    def _():
        u_ref[...] = (x_ref[...] * (1.0 + sc_ref[...]) + sh_ref[...]).astype(BF16)

    o_ref[...] = _dot(u_ref[...], w_ref[...]).astype(o_ref.dtype)


def in_projection(x2d, sc, sh, w_bf16, layer, seq, tm=1024, tn=1536):
    M, D = x2d.shape
    N = w_bf16.shape[2]
    tm = min(tm, seq)
    tn = min(tn, N)
    bpb = seq // tm
    return pl.pallas_call(
        _inproj_kernel,
        grid=(M // tm, N // tn),
        in_specs=[
            pl.BlockSpec((tm, D), lambda i, j: (i, 0)),
            pl.BlockSpec((None, 1, D), lambda i, j: (i // bpb, 0, 0)),
            pl.BlockSpec((None, 1, D), lambda i, j: (i // bpb, 0, 0)),
            pl.BlockSpec((None, D, tn), lambda i, j: (layer, 0, j)),
        ],
        out_specs=pl.BlockSpec((tm, tn), lambda i, j: (i, j)),
        out_shape=jax.ShapeDtypeStruct((M, N), BF16),
        scratch_shapes=[pltpu.VMEM((tm, D), BF16)],
        compiler_params=_cparams(("arbitrary", "arbitrary")),
        name="inproj",
    )(x2d, sc, sh, w_bf16)


def _t5_bucket_np(rel):
    n = np.maximum(rel, 0)
    max_exact = NUM_BUCKETS // 2
    ratio = np.maximum(n, 1).astype(np.float32) / np.float32(max_exact)
    large = max_exact + (np.log(ratio).astype(np.float32) / np.float32(math.log(MAX_DISTANCE / max_exact))
                         * np.float32(NUM_BUCKETS - max_exact)).astype(np.int32)
    large = np.minimum(large, NUM_BUCKETS - 1)
    return np.where(n < max_exact, n, large)


def _bucket_thresholds():
    buckets = _t5_bucket_np(np.arange(2 * MAX_DISTANCE))
    assert np.all(np.diff(buckets) >= 0) and buckets[MAX_DISTANCE] == NUM_BUCKETS - 1
    return [int(np.argmax(buckets >= k)) for k in range(NUM_BUCKETS)]


def _relbias_kernel(tab_ref, o_ref, *, t, n_heads, thresholds):
    h = pl.program_id(0)
    row = lax.broadcasted_iota(jnp.int32, (t, t), 0)
    col = lax.broadcasted_iota(jnp.int32, (t, t), 1)
    for blk in range(2):
        rel = row - col + blk * t
        for mp in range(2):
            entry = lambda k: tab_ref[(k * n_heads + h) * 2 + mp]
            far = entry(NUM_BUCKETS - 1)
            val = jnp.full((t, t), entry(0) - far, F32)
            for k in range(1, NUM_BUCKETS):
                val = jnp.where(rel >= thresholds[k], entry(k) - far, val)
            if blk == 0:
                val = jnp.where(rel < 0, MASK_VALUE, val)
            o_ref[blk, mp * t:(mp + 1) * t, :] = val


def rel_bias_tiles(rel_bias, t):
    assert t >= MAX_DISTANCE
    n_heads = rel_bias.shape[1]
    kern = functools.partial(_relbias_kernel, t=t, n_heads=n_heads, thresholds=_bucket_thresholds())
    grid_spec = pltpu.PrefetchScalarGridSpec(
        num_scalar_prefetch=1,
        grid=(n_heads,),
        in_specs=[],
        out_specs=pl.BlockSpec((None, 2, 2 * t, t), lambda h, tab: (h, 0, 0, 0)),
    )
    return pl.pallas_call(
        kern,
        grid_spec=grid_spec,
        out_shape=jax.ShapeDtypeStruct((n_heads, 2, 2 * t, t), F32),
        compiler_params=_cparams(("arbitrary",)),
        name="relbias",
    )(rel_bias.reshape(-1).astype(F32))


def _diff_kernel(lam_ref, q_ref, k_ref, v_ref, bias_ref, g_ref, o_ref, qs_ref, qn_ref, s_ref, m_ref, acc_ref,
                 *, t, rc, nq, out_scale):
    ones = jnp.ones((t, HEAD_DIM), BF16)

    def keys(j):
        return pl.ds(pl.multiple_of(j * t, t), t)

    def stack_queries(qi, dst_ref):
        q = q_ref[keys(qi), :] * BF16(DIFF_QK_DIM ** -0.5)
        lane = lax.broadcasted_iota(jnp.int32, q.shape, 1)
        zero = jnp.zeros_like(q)
        dst_ref[0:t, :] = jnp.where(lane < DIFF_QK_DIM, q, zero)
        dst_ref[t:2 * t, :] = jnp.where(lane >= DIFF_QK_DIM, q, zero)

    def step(j, bias_blk, next_queries_ref, next_j):
        vb = v_ref[keys(j), :]
        vb1 = jnp.concatenate([vb, ones], axis=1)
        kb_next = k_ref[keys(next_j), :]
        diagonal = bias_blk == 0
        bias_reach = rc + MAX_DISTANCE
        for r in range(2 * t // rc):
            rows = pl.ds(r * rc, rc)
            nc = (r % (t // rc) + 1) * rc if diagonal else t
            s = s_ref[rows, :nc]
            s_ref[rows, :] = _dot_nt(next_queries_ref[rows, :], kb_next)
            if bias_blk is not None:
                c = r % (t // rc)
                if diagonal or c == 0:
                    b0 = max(nc - bias_reach, 0)
                    biased = s[:, b0:] + bias_ref[bias_blk, rows, b0:nc]
                    s = biased if b0 == 0 else jnp.concatenate([s[:, :b0], biased], axis=1)
            m_old = m_ref[rows, :]
            m_new = jnp.maximum(m_old, jnp.max(s, axis=1, keepdims=True))
            p = jnp.exp(s - jnp.concatenate([m_new] * (nc // LANES), axis=1)).astype(BF16)
            alpha = jnp.exp(m_old - m_new)
            acc_ref[rows, :] = jnp.concatenate([alpha, alpha], axis=1) * acc_ref[rows, :] + _dot(p, vb1[:nc])
            m_ref[rows, :] = m_new

    stack_queries(0, qs_ref)
    kb0 = k_ref[keys(0), :]
    for r in range(2 * t // rc):
        rows = pl.ds(r * rc, rc)
        s_ref[rows, :] = _dot_nt(qs_ref[rows, :], kb0)

    def reset():
        m_ref[...] = jnp.full(m_ref.shape, MASK_VALUE, F32)
        acc_ref[...] = jnp.zeros(acc_ref.shape, F32)

    def diagonal_step_and_output(qi):
        stack_queries(jnp.minimum(qi + 1, nq - 1), qn_ref)
        step(qi, 0, qn_ref, 0)
        acc = acc_ref[...]
        ratio = acc[:, :HEAD_DIM] / acc[:, HEAD_DIM:]
        o = ratio[:t] - lam_ref[0] * ratio[t:]
        o = o * lax.rsqrt(jnp.mean(o * o, axis=-1, keepdims=True) + SUBLN_EPS)
        o_ref[keys(qi), :] = (o * g_ref[...] * out_scale).astype(o_ref.dtype)
        qs_ref[...] = qn_ref[...]

    reset()
    diagonal_step_and_output(0)

    def tile_body(qi, carry):
        reset()
        n_far = qi - 1

        def far_pair(p, c):
            step(2 * p, None, qs_ref, 2 * p + 1)
            step(2 * p + 1, None, qs_ref, 2 * p + 2)
            return c

        lax.fori_loop(0, n_far // 2, far_pair, 0)

        @pl.when(n_far % 2 == 1)
        def _():
            step(n_far - 1, None, qs_ref, n_far)

        step(qi - 1, 1, qs_ref, qi)
        diagonal_step_and_output(qi)
        return carry

    lax.fori_loop(1, nq, tile_body, 0)


def diff_attention(proj, lam, bias_tiles, subln_g, batch, seq, lam_init, t):
    M = proj.shape[0]
    H = N_DIFF_HEADS
    nq = seq // t
    rc = min(DIFF_ROW_CHUNK, t)
    assert rc % LANES == 0 and rc >= MAX_DISTANCE and MAX_DISTANCE % LANES == 0
    kern = functools.partial(_diff_kernel, t=t, rc=rc, nq=nq, out_scale=1.0 - lam_init)
    grid_spec = pltpu.PrefetchScalarGridSpec(
        num_scalar_prefetch=1,
        grid=(H, batch),
        in_specs=[
            pl.BlockSpec((seq, HEAD_DIM), lambda h, b, s: (b, h)),
            pl.BlockSpec((seq, HEAD_DIM), lambda h, b, s: (b, H + h)),
            pl.BlockSpec((seq, HEAD_DIM), lambda h, b, s: (b, 2 * H + h)),
            pl.BlockSpec((None, 2, 2 * t, t), lambda h, b, s: (h, 0, 0, 0)),
            pl.BlockSpec((1, HEAD_DIM), lambda h, b, s: (0, 0)),
        ],
        out_specs=pl.BlockSpec((seq, HEAD_DIM), lambda h, b, s: (b, h)),
        scratch_shapes=[
            pltpu.VMEM((2 * t, HEAD_DIM), BF16),
            pltpu.VMEM((2 * t, HEAD_DIM), BF16),
            pltpu.VMEM((2 * t, t), F32),
            pltpu.VMEM((2 * t, LANES), F32),
            pltpu.VMEM((2 * t, 2 * HEAD_DIM), F32),
        ],
    )
    return pl.pallas_call(
        kern,
        grid_spec=grid_spec,
        out_shape=jax.ShapeDtypeStruct((M, H * HEAD_DIM), BF16),
        compiler_params=_cparams(("arbitrary", "arbitrary")),
        name="diffattn",
    )(lam.reshape(1).astype(F32), proj, proj, proj, bias_tiles, subln_g.reshape(1, HEAD_DIM))


def _sb_kernel(q_ref, k_ref, v_ref, o_ref, carry_ref, acc_ref, *, t, hp, scale2):
    qi = pl.program_id(2)

    def keys(j):
        return pl.ds(pl.multiple_of(j * t, t), t)

    def upper_ones(n):
        row = lax.broadcasted_iota(jnp.int32, (n, n), 0)
        col = lax.broadcasted_iota(jnp.int32, (n, n), 1)
        return jnp.where(row > col, 1.0, 0.0).astype(BF16)

    def sticks(z2, valid):
        lp = jnp.log2(1.0 + jnp.exp2(-jnp.abs(z2)))
        drop = jnp.maximum(z2, 0.0) + lp
        log2_beta = z2 - drop
        if valid is not None:
            drop = jnp.where(valid, drop, 0.0)
        return drop, log2_beta

    prev = jnp.maximum(qi - 1, 0)
    qrow = lax.broadcasted_iota(jnp.int32, (t, 2 * t), 0)
    kcol = lax.broadcasted_iota(jnp.int32, (t, 2 * t), 1)
    n_prev = jnp.where(qi > 0, t, 0)
    valid = (kcol < n_prev) | ((kcol >= t) & (kcol - t < qrow))
    upper = upper_ones(t)
    for hh in range(hp):
        cols = slice(hh * HEAD_DIM, (hh + 1) * HEAD_DIM)
        kw = jnp.concatenate([k_ref[keys(prev), cols], k_ref[keys(qi), cols]], axis=0)
        vw = jnp.concatenate([v_ref[keys(prev), cols], v_ref[keys(qi), cols]], axis=0)
        drop, log2_beta = sticks(_dot_nt(q_ref[:, cols], kw) * scale2, valid)
        hi, lo = _split(drop)
        stacked = jnp.concatenate([hi[:, :t], lo[:, :t], hi[:, t:], lo[:, t:]], axis=0)
        within = _dot(stacked, upper)
        diag_total = jnp.sum(drop[:, t:], axis=1, keepdims=True)
        later = jnp.concatenate([within[0:t] + within[t:2 * t] + diag_total,
                                 within[2 * t:3 * t] + within[3 * t:4 * t]], axis=1)
        a = jnp.where(valid, jnp.exp2(log2_beta - later), 0.0)
        acc_ref[:, cols] = _dot(a.astype(BF16), vw)
        prev_total = jnp.sum(drop[:, :t], axis=1, keepdims=True)
        carry_ref[hh] = jnp.broadcast_to(prev_total + diag_total, (t, LANES))

    def step(j):
        for hh in range(hp):
            cols = slice(hh * HEAD_DIM, (hh + 1) * HEAD_DIM)
            drop, log2_beta = sticks(_dot_nt(q_ref[:, cols], k_ref[keys(j), cols]) * scale2, None)
            hi, lo = _split(drop)
            later = _dot(hi, upper) + _dot(lo, upper)
            carry = carry_ref[hh]
            a = jnp.exp2(log2_beta - later - jnp.concatenate([carry] * (t // LANES), axis=1))
            acc_ref[:, cols] += _dot(a.astype(BF16), v_ref[keys(j), cols])
            carry_ref[hh] = carry + jnp.sum(drop, axis=1, keepdims=True)

    def cond(state):
        n, live = state
        return jnp.logical_and(n < qi - 1, live < SB_EXIT_LOG2)

    def body(state):
        n, _ = state
        step(qi - 2 - n)
        return n + 1, jnp.min(carry_ref[...])

    lax.while_loop(cond, body, (jnp.int32(0), jnp.min(carry_ref[...])))
    o_ref[...] = acc_ref[...].astype(o_ref.dtype)


def sb_attention(proj, batch, seq, t, hp=SB_HEADS_PER_STEP):
    M = proj.shape[0]
    H = N_SB_HEADS
    base = 3 * N_DIFF_HEADS // hp
    nq = seq // t
    w = hp * HEAD_DIM
    kern = functools.partial(_sb_kernel, t=t, hp=hp, scale2=HEAD_DIM ** -0.5 * LOG2E)
    return pl.pallas_call(
        kern,
        grid=(H // hp, batch, nq),
        in_specs=[
            pl.BlockSpec((t, w), lambda h, b, qi: (b * nq + qi, base + h)),
            pl.BlockSpec((seq, w), lambda h, b, qi: (b, base + H // hp + h)),
            pl.BlockSpec((seq, w), lambda h, b, qi: (b, base + 2 * (H // hp) + h)),
        ],
        out_specs=pl.BlockSpec((t, w), lambda h, b, qi: (b * nq + qi, h)),
        out_shape=jax.ShapeDtypeStruct((M, H * HEAD_DIM), BF16),
        scratch_shapes=[pltpu.VMEM((hp, t, LANES), F32), pltpu.VMEM((t, w), F32)],
        compiler_params=_cparams(("arbitrary", "arbitrary", "arbitrary")),
        name="sbattn",
    )(proj, proj, proj)


ROUTE_E1, ROUTE_E2, ROUTE_W1, ROUTE_W2, ROUTE_R1, ROUTE_R2 = range(6)


def _mix_kernel(a_ref, b_ref, wo_ref, x_ref, g1_ref, sc2_ref, sh2_ref, lng_ref, lnb_ref, wrt_ref, brt_ref,
                x1_ref, u2_ref, route_ref, route_t_ref, cnt_ref, carry_ref, wsplit_ref, wob_ref, *, alpha, half):
    i = pl.program_id(0)

    @pl.when(i == 0)
    def _():
        carry_ref[...] = jnp.zeros(carry_ref.shape, F32)
        w_hi, w_lo = _split(wrt_ref[...])
        wsplit_ref[...] = jnp.concatenate([w_hi, w_lo], axis=1)
        wob_ref[...] = wo_ref[...].astype(BF16)

    mix = _dot(a_ref[...], wob_ref[pl.ds(0, half), :]) + _dot(b_ref[...], wob_ref[pl.ds(half, half), :])
    h = alpha * x_ref[...] + (1.0 + g1_ref[...]) * mix
    x1 = _layer_norm(h, lng_ref[...], lnb_ref[...])
    x1_ref[...] = x1
    u2 = x1 * (1.0 + sc2_ref[...]) + sh2_ref[...]
    nseg = u2.shape[1] // (2 * LANES)
    for s in range(nseg):
        lo = u2[:, 2 * s * LANES:(2 * s + 1) * LANES].astype(BF16).astype(F32)
        hi = u2[:, (2 * s + 1) * LANES:(2 * s + 2) * LANES].astype(BF16).astype(F32)
        lo_bits = lax.shift_right_logical(lax.bitcast_convert_type(lo, jnp.uint32), jnp.uint32(16))
        hi_bits = lax.bitcast_convert_type(hi, jnp.uint32) & jnp.uint32(HI_HALF)
        u2_ref[pl.ds(s, u2.shape[0], stride=nseg), :] = lo_bits | hi_bits

    u_hi, u_lo = _split(u2)
    hh_hl = _dot(u_hi, wsplit_ref[...])
    lh = _dot(u_lo, wsplit_ref[:, :LANES])
    logits = hh_hl[:, :LANES] + (hh_hl[:, LANES:] + lh) + brt_ref[...]
    tm = logits.shape[0]
    lane = lax.broadcasted_iota(jnp.int32, (tm, LANES), 1).astype(F32)
    gmask = lane < N_GROUPS
    gl = jnp.where(gmask, logits, MASK_VALUE)
    gmax = jnp.max(gl, axis=1, keepdims=True)
    gidx = jnp.min(jnp.where(gmask & (gl == gmax), lane, float(LANES)), axis=1, keepdims=True)
    gsum = jnp.sum(jnp.where(gmask, jnp.exp(gl - gmax), 0.0), axis=1, keepdims=True)
    gp = 1.0 / gsum

    lo = N_GROUPS + gidx * EXPERTS_PER_GROUP
    emask = (lane >= lo) & (lane < lo + EXPERTS_PER_GROUP)
    el = jnp.where(emask, logits, MASK_VALUE)
    v1 = jnp.max(el, axis=1, keepdims=True)
    i1 = jnp.min(jnp.where(emask & (el == v1), lane, float(LANES)), axis=1, keepdims=True)
    emask2 = emask & (lane != i1)
    el2 = jnp.where(emask2, logits, MASK_VALUE)
    v2 = jnp.max(el2, axis=1, keepdims=True)
    i2 = jnp.min(jnp.where(emask2 & (el2 == v2), lane, float(LANES)), axis=1, keepdims=True)
    tt = jnp.exp(v2 - v1)
    w1 = gp / (1.0 + tt)
    w2 = gp * tt / (1.0 + tt)
    e1 = i1 - N_GROUPS
    e2 = i2 - N_GROUPS

    oh1 = lane == e1
    oh2 = lane == e2
    oh1b = jnp.where(oh1, 1.0, 0.0).astype(BF16)
    oh2b = jnp.where(oh2, 1.0, 0.0).astype(BF16)
    row = lax.broadcasted_iota(jnp.int32, (tm, tm), 0)
    col = lax.broadcasted_iota(jnp.int32, (tm, tm), 1)
    lower = jnp.where(col < row, 1.0, 0.0).astype(BF16)
    c1 = jnp.sum(jnp.where(oh1, 1.0, 0.0), axis=0, keepdims=True)
    c2 = jnp.sum(jnp.where(oh2, 1.0, 0.0), axis=0, keepdims=True)
    carry = carry_ref[...]
    r1 = _dot(lower, oh1b) + carry
    r2 = _dot(lower, oh2b) + (carry + c1)
    rank1 = jnp.sum(jnp.where(oh1, r1, 0.0), axis=1, keepdims=True)
    rank2 = jnp.sum(jnp.where(oh2, r2, 0.0), axis=1, keepdims=True)
    carry = carry + c1 + c2
    carry_ref[...] = carry
    cnt_ref[...] = jnp.broadcast_to(carry, cnt_ref.shape)

    out = jnp.zeros((tm, LANES), F32)
    for ln, val in ((ROUTE_E1, e1), (ROUTE_E2, e2), (ROUTE_W1, w1), (ROUTE_W2, w2),
                    (ROUTE_R1, rank1), (ROUTE_R2, rank2)):
        out = jnp.where(lane == float(ln), val, out)
    route_ref[...] = out
    route_t_ref[...] = out.T[:8]


def mix_ln_route(a_out, b_out, w_o, layer, x2d, g1, sc2, sh2, ln_g, ln_b, w_rt, b_rt, seq, alpha, tm=512):
    M, D = x2d.shape
    half = a_out.shape[1]
    bpb = seq // tm
    nseg = D // (2 * LANES)
    row = lambda i: (i // bpb, 0, 0)
    full = lambda i: (0, 0)
    kern = functools.partial(_mix_kernel, alpha=alpha, half=half)
    return pl.pallas_call(
        kern,
        grid=(M // tm,),
        in_specs=[
            pl.BlockSpec((tm, half), lambda i: (i, 0)),
            pl.BlockSpec((tm, half), lambda i: (i, 0)),
            pl.BlockSpec((None, 2 * half, D), lambda i: (layer, 0, 0), pipeline_mode=pl.Buffered(1)),
            pl.BlockSpec((tm, D), lambda i: (i, 0)),
            pl.BlockSpec((None, 1, D), row),
            pl.BlockSpec((None, 1, D), row),
            pl.BlockSpec((None, 1, D), row),
            pl.BlockSpec((1, D), full),
            pl.BlockSpec((1, D), full),
            pl.BlockSpec((D, LANES), full),
            pl.BlockSpec((1, LANES), full),
        ],
        out_specs=[
            pl.BlockSpec((tm, D), lambda i: (i, 0)),
            pl.BlockSpec((tm * nseg, LANES), lambda i: (i, 0)),
            pl.BlockSpec((tm, LANES), lambda i: (i, 0)),
            pl.BlockSpec((8, tm), lambda i: (0, i)),
            pl.BlockSpec((8, LANES), full),
        ],
        out_shape=[
            jax.ShapeDtypeStruct((M, D), F32),
            jax.ShapeDtypeStruct((M * nseg, LANES), jnp.uint32),
            jax.ShapeDtypeStruct((M, LANES), F32),
            jax.ShapeDtypeStruct((8, M), F32),
            jax.ShapeDtypeStruct((8, LANES), F32),
        ],
        scratch_shapes=[pltpu.VMEM((1, LANES), F32), pltpu.VMEM((D, 2 * LANES), BF16),
                        pltpu.VMEM((2 * half, D), BF16)],
        compiler_params=_cparams(("arbitrary",)),
        name="mix",
    )(a_out, b_out, w_o, x2d, g1, sc2, sh2, ln_g.reshape(1, D), ln_b.reshape(1, D), w_rt, b_rt)


def _item_copy(src_hbm, dst_vmem, src_item, dst_item, rpi, sem):
    src = src_hbm.at[pl.ds(pl.multiple_of(src_item * rpi, rpi), rpi), :]
    dst = dst_vmem.at[pl.ds(pl.multiple_of(dst_item * rpi, rpi), rpi), :]
    return pltpu.make_async_copy(src, dst, sem)


def _start_items(idx_ref, base, n_items, rpi, src_hbm, dst_vmem, sem, unroll=False):
    def issue(r, c):
        _item_copy(src_hbm, dst_vmem, idx_ref[base + r], r, rpi, sem).start()
        return c

    lax.fori_loop(0, n_items, issue, 0, unroll=unroll)


def _wait_items(n_items, rpi, src_hbm, dst_vmem, sem):
    pltpu.make_async_copy(src_hbm.at[pl.ds(0, n_items * rpi), :], dst_vmem, sem).wait()


def _expert_kernel(tok_ref, first_ref, u_hbm, wg_hbm, wu_hbm, wd_hbm, y_hbm,
                   xbuf_ref, xs_ref, obuf_ref, stage_gu_ref, stage_d_ref, wg_ref, wu_ref, wd_ref,
                   gsems, osems, wsems, *, layer, tb, nseg, nrow, n_tiles):
    e = pl.program_id(0)
    n_exp = pl.num_programs(0)
    n_used = first_ref[n_exp]
    dc = wg_ref.shape[0] // WEIGHT_CHUNKS
    fc = wd_ref.shape[0] // WEIGHT_CHUNKS

    def weight_copies(expert, c):
        drows = pl.ds(c * dc, dc)
        frows = pl.ds(c * fc, fc)
        return (pltpu.make_async_copy(wg_hbm.at[layer, expert, drows, :], stage_gu_ref.at[c, 0], wsems.at[c]),
                pltpu.make_async_copy(wu_hbm.at[layer, expert, drows, :], stage_gu_ref.at[c, 1], wsems.at[c]),
                pltpu.make_async_copy(wd_hbm.at[layer, expert, frows, :], stage_d_ref.at[c], wsems.at[c]))

    def start_gather(k, unroll):
        s = k % GATHER_SLOTS
        tile = jnp.minimum(k, n_used - 1)
        _start_items(tok_ref, tile * tb, tb, nseg, u_hbm, xbuf_ref.at[s], gsems.at[s], unroll)

    def wait_gather(k):
        s = k % GATHER_SLOTS
        _wait_items(tb, nseg, u_hbm, xbuf_ref.at[s], gsems.at[s])

    def out_copy(tile, s):
        rows = pl.ds(pl.multiple_of(tile * (tb * nrow), tb * nrow), tb * nrow)
        return pltpu.make_async_copy(obuf_ref.at[s], y_hbm.at[rows, :], osems.at[s])

    @pl.when(e == 0)
    def _():
        for c in range(WEIGHT_CHUNKS):
            for cp in weight_copies(0, c):
                cp.start(priority=WEIGHT_DMA_PRIORITY)
        for k in range(GATHER_SLOTS - 1):
            start_gather(k, False)

    for c in range(WEIGHT_CHUNKS):
        for cp in weight_copies(e, c):
            cp.wait()
        wg_ref[c * dc:(c + 1) * dc, :] = stage_gu_ref[c, 0].astype(BF16)
        wu_ref[c * dc:(c + 1) * dc, :] = stage_gu_ref[c, 1].astype(BF16)
        wd_ref[c * fc:(c + 1) * fc, :] = stage_d_ref[c].astype(BF16)

        @pl.when(e + 1 < n_exp)
        def _():
            for cp in weight_copies(e + 1, c):
                cp.start(priority=WEIGHT_DMA_PRIORITY)

    def tile_body(i, carry):
        slot = i % GATHER_SLOTS
        oslot = i % 2

        @pl.when(i >= 2)
        def _():
            out_copy(i - 2, oslot).wait()

        wait_gather(i)
        start_gather(i + GATHER_SLOTS - 1, True)
        for s in range(nseg):
            w = xbuf_ref[slot, pl.ds(s, tb, stride=nseg), :]
            lo = lax.bitcast_convert_type(lax.shift_left(w, jnp.uint32(16)), F32)
            hi = lax.bitcast_convert_type(w & jnp.uint32(HI_HALF), F32)
            xs_ref[:, 2 * s * LANES:(2 * s + 1) * LANES] = lo.astype(BF16)
            xs_ref[:, (2 * s + 1) * LANES:(2 * s + 2) * LANES] = hi.astype(BF16)
        x = xs_ref[...]
        g = _dot(x, wg_ref[...])
        u = _dot(x, wu_ref[...])
        h = (g * jax.nn.sigmoid(g) * u).astype(BF16)
        y = _dot(h, wd_ref[...])
        for s in range(nrow):
            obuf_ref[oslot, pl.ds(s, tb, stride=nrow), :] = y[:, s * OUT_ITEM_WIDTH:(s + 1) * OUT_ITEM_WIDTH]
        out_copy(i, oslot).start()
        return carry

    lax.fori_loop(first_ref[e], first_ref[e + 1], tile_body, 0)

    @pl.when(e == n_exp - 1)
    def _():
        for k in range(GATHER_SLOTS - 1):
            wait_gather(n_used + k)
        for back in (1, 2):

            @pl.when(n_used >= back)
            def _():
                out_copy(n_used - back, (n_used - back) % 2).wait()

        obuf_ref[0] = jnp.zeros(obuf_ref.shape[1:], obuf_ref.dtype)

        def zero_start(i, carry):
            out_copy(i, 0).start()
            return carry

        def zero_wait(i, carry):
            out_copy(i, 0).wait()
            return carry

        lax.fori_loop(n_used, n_tiles, zero_start, 0)
        lax.fori_loop(n_used, n_tiles, zero_wait, 0)


def expert_mlp(pad_tok, first_tile, u2p, wg, wu, wd, layer, tb):
    P = pad_tok.shape[0]
    E, D, F = wg.shape[1:]
    nseg = D // (2 * LANES)
    nrow = D // OUT_ITEM_WIDTH
    fc = F // WEIGHT_CHUNKS
    any_spec = pl.BlockSpec(memory_space=pl.ANY)
    grid_spec = pltpu.PrefetchScalarGridSpec(
        num_scalar_prefetch=2,
        grid=(E,),
        in_specs=[any_spec, any_spec, any_spec, any_spec],
        out_specs=any_spec,
        scratch_shapes=[
            pltpu.VMEM((GATHER_SLOTS, tb * nseg, LANES), jnp.uint32),
            pltpu.VMEM((tb, D), BF16),
            pltpu.VMEM((2, tb * nrow, OUT_ITEM_WIDTH), F32),
            pltpu.VMEM((WEIGHT_CHUNKS, 2, D // WEIGHT_CHUNKS, F), F32),
            pltpu.VMEM((WEIGHT_CHUNKS, fc, D), F32),
            pltpu.VMEM((D, F), BF16),
            pltpu.VMEM((D, F), BF16),
            pltpu.VMEM((F, D), BF16),
            pltpu.SemaphoreType.DMA((GATHER_SLOTS,)),
            pltpu.SemaphoreType.DMA((2,)),
            pltpu.SemaphoreType.DMA((WEIGHT_CHUNKS,)),
        ],
    )
    return pl.pallas_call(
        functools.partial(_expert_kernel, layer=layer, tb=tb, nseg=nseg, nrow=nrow, n_tiles=P // tb),
        grid_spec=grid_spec,
        out_shape=jax.ShapeDtypeStruct((P * nrow, OUT_ITEM_WIDTH), F32),
        compiler_params=_cparams(("arbitrary",)),
        name="expert",
    )(pad_tok, first_tile, u2p, wg, wu, wd)


def _final_kernel(dest_ref, x1_ref, route_ref, g2_ref, lng_ref, lnb_ref, yb_hbm, o_ref, buf_ref, sems,
                  *, alpha, tm, m, nrow):
    i = pl.program_id(0)
    slot = i % GATHER_SLOTS
    last = pl.num_programs(0) - 1

    def start(n, unroll):
        s = n % GATHER_SLOTS
        tile = jnp.minimum(n, last)
        for k in range(TOP_K):
            _start_items(dest_ref, k * m + tile * tm, tm, nrow, yb_hbm, buf_ref.at[s, k], sems.at[s], unroll)

    def wait(n):
        s = n % GATHER_SLOTS
        for k in range(TOP_K):
            _wait_items(tm, nrow, yb_hbm, buf_ref.at[s, k], sems.at[s])

    @pl.when(i == 0)
    def _():
        for n in range(GATHER_SLOTS - 1):
            start(n, False)

    wait(i)
    start(i + GATHER_SLOTS - 1, True)

    def expert_out(k):
        chunks = [buf_ref[slot, k, pl.ds(s, tm, stride=nrow), :] for s in range(nrow)]
        return jnp.concatenate(chunks, axis=1)

    route = route_ref[...]
    w1 = route[:, ROUTE_W1:ROUTE_W1 + 1]
    w2 = route[:, ROUTE_W2:ROUTE_W2 + 1]
    y = w1 * expert_out(0) + w2 * expert_out(1)
    h = alpha * x1_ref[...] + (1.0 + g2_ref[...]) * y
    o_ref[...] = _layer_norm(h, lng_ref[...], lnb_ref[...])

    @pl.when(i == last)
    def _():
        for n in range(1, GATHER_SLOTS):
            wait(i + n)


def combine_ln(dest_flat, x1, route, g2, ln_g, ln_b, yb, seq, alpha, tm=256):
    M, D = x1.shape
    bpb = seq // tm
    nrow = D // OUT_ITEM_WIDTH
    grid_spec = pltpu.PrefetchScalarGridSpec(
        num_scalar_prefetch=1,
        grid=(M // tm,),
        in_specs=[
            pl.BlockSpec((tm, D), lambda i, d: (i, 0)),
            pl.BlockSpec((tm, LANES), lambda i, d: (i, 0)),
            pl.BlockSpec((None, 1, D), lambda i, d: (i // bpb, 0, 0)),
            pl.BlockSpec((1, D), lambda i, d: (0, 0)),
            pl.BlockSpec((1, D), lambda i, d: (0, 0)),
            pl.BlockSpec(memory_space=pl.ANY),
        ],
        out_specs=pl.BlockSpec((tm, D), lambda i, d: (i, 0)),
        scratch_shapes=[pltpu.VMEM((GATHER_SLOTS, TOP_K, tm * nrow, OUT_ITEM_WIDTH), F32),
                        pltpu.SemaphoreType.DMA((GATHER_SLOTS,))],
    )
    return pl.pallas_call(
        functools.partial(_final_kernel, alpha=alpha, tm=tm, m=M, nrow=nrow),
        grid_spec=grid_spec,
        out_shape=jax.ShapeDtypeStruct((M, D), F32),
        compiler_params=_cparams(("arbitrary",)),
        name="final",
    )(dest_flat, x1, route, g2, ln_g.reshape(1, D), ln_b.reshape(1, D), yb)


def moe_layout(route_t, cnt, tb):
    M = route_t.shape[1]
    n_assign = M * TOP_K
    counts = cnt[0, :N_EXPERTS].astype(jnp.int32)
    padded = ((counts + tb - 1) // tb) * tb
    pends = jnp.cumsum(padded)
    pstarts = pends - padded
    n_tiles = (n_assign + N_EXPERTS * (tb - 1) + tb - 1) // tb
    e = route_t[ROUTE_E1:ROUTE_E2 + 1].astype(jnp.int32)
    rank = route_t[ROUTE_R1:ROUTE_R2 + 1].astype(jnp.int32)
    before = jnp.arange(N_EXPERTS, dtype=jnp.int32)[:, None, None] < e[None]
    dest = jnp.sum(jnp.where(before, padded[:, None, None], 0), axis=0) + rank
    dest_flat = dest.reshape(-1)
    tok = jnp.tile(jnp.arange(M, dtype=jnp.int32), TOP_K)
    pad_tok = jnp.zeros((n_tiles * tb,), jnp.int32).at[dest_flat].set(tok)
    first_tile = (jnp.concatenate([pstarts, pends[-1:]]) // tb).astype(jnp.int32)
    return dest_flat, pad_tok, first_tile


def _forward(x, c, w_ada, b_ada, w_in, diff_lambda, diff_subln_g, rel_bias, w_o, ln_g, ln_b, w_group, b_group,
             w_router, b_router, w_gate, w_up, w_down, *, diff_tile, sb_tile, moe_tile, ada_tn):
    B, S, D = x.shape
    depth = w_ada.shape[0]
    M = B * S
    alpha = (2 * depth) ** 0.25

    c8 = jnp.zeros((8, D), F32).at[:B].set(c)
    mod = ada_modulation(c8, w_ada, b_ada, ada_tn)[:, :B]
    mod = mod.reshape(depth, B, 6, 1, D)
    bias_tiles = rel_bias_tiles(rel_bias, diff_tile)
    w_in_b = w_in.astype(BF16)

    x2d = x.reshape(M, D)
    for l in range(depth):
        sh1, sc1, g1, sh2, sc2, g2 = (mod[l, :, n] for n in range(6))
        proj = in_projection(x2d, sc1, sh1, w_in_b, l, S)

        lam_init = 0.8 - 0.6 * math.exp(-0.3 * l)
        lp = diff_lambda[l].astype(F32)
        lam = jnp.exp(jnp.sum(lp[0] * lp[1])) - jnp.exp(jnp.sum(lp[2] * lp[3])) + lam_init
        a_out = diff_attention(proj, lam, bias_tiles, diff_subln_g[l], B, S, lam_init, diff_tile)
        b_out = sb_attention(proj, B, S, sb_tile)

        w_rt = jnp.zeros((D, LANES), F32).at[:, :N_GROUPS].set(w_group[l])
        w_rt = w_rt.at[:, N_GROUPS:N_GROUPS + N_EXPERTS].set(w_router[l])
        b_rt = jnp.zeros((1, LANES), F32).at[0, :N_GROUPS].set(b_group[l])
        b_rt = b_rt.at[0, N_GROUPS:N_GROUPS + N_EXPERTS].set(b_router[l])
        x1, u2, route, route_t, cnt = mix_ln_route(a_out, b_out, w_o, l, x2d, g1, sc2, sh2,
                                                   ln_g[l, 0], ln_b[l, 0], w_rt, b_rt, S, alpha)

        dest_flat, pad_tok, first_tile = moe_layout(route_t, cnt, moe_tile)
        yb = expert_mlp(pad_tok, first_tile, u2, w_gate, w_up, w_down, l, moe_tile)
        x2d = combine_ln(dest_flat, x1, route, g2, ln_g[l, 1], ln_b[l, 1], yb, S, alpha)
    return x2d.reshape(B, S, D)


def kernel(x, c, w_ada, b_ada, w_in, diff_lambda, diff_subln_g, rel_bias, w_o, ln_g, ln_b, w_group, b_group,
           w_router, b_router, w_gate, w_up, w_down):
    return _forward(x, c, w_ada, b_ada, w_in, diff_lambda, diff_subln_g, rel_bias, w_o, ln_g, ln_b, w_group,
                    b_group, w_router, b_router, w_gate, w_up, w_down,
                    diff_tile=512, sb_tile=256, moe_tile=256, ada_tn=1024)
```

```python
import functools
import math

import jax
import jax.numpy as jnp
import numpy as np
from jax import lax
from jax.experimental import pallas as pl
from jax.experimental.pallas import tpu as pltpu

F32 = jnp.float32
BF16 = jnp.bfloat16

N_DIFF_HEADS = 8
N_SB_HEADS = 8
HEAD_DIM = 128
DIFF_QK_DIM = 64
NUM_BUCKETS = 32
MAX_DISTANCE = 128
N_GROUPS = 4
EXPERTS_PER_GROUP = 8
N_EXPERTS = N_GROUPS * EXPERTS_PER_GROUP
TOP_K = 2
LN_EPS = 1e-5
SUBLN_EPS = 1e-5

LANES = 128
MASK_VALUE = -1e30
LOG2E = math.log2(math.e)
SB_EXIT_LOG2 = 104.0 * LOG2E
DIFF_ROW_CHUNK = 256
SB_HEADS_PER_STEP = 8
HI_HALF = 0xFFFF0000
OUT_ITEM_WIDTH = LANES
GATHER_SLOTS = 3
WEIGHT_CHUNKS = 4
WEIGHT_DMA_PRIORITY = 1
VMEM_LIMIT = 56 * 1024 * 1024


def _cparams(sem):
    return pltpu.CompilerParams(dimension_semantics=sem, vmem_limit_bytes=VMEM_LIMIT)


def _dot(a, b):
    return jnp.dot(a, b, preferred_element_type=F32)


def _dot_nt(a, b):
    return lax.dot_general(a, b, (((1,), (1,)), ((), ())), preferred_element_type=F32)


def _split(a):
    hi = a.astype(BF16)
    lo = (a - hi.astype(F32)).astype(BF16)
    return hi, lo


def _dot3(a, b):
    a_hi, a_lo = _split(a)
    b_hi, b_lo = _split(b)
    return _dot(a_hi, b_hi) + (_dot(a_hi, b_lo) + _dot(a_lo, b_hi))


def _layer_norm(h, g, b):
    mu = jnp.mean(h, axis=-1, keepdims=True)
    d = h - mu
    var = jnp.mean(d * d, axis=-1, keepdims=True)
    return d * lax.rsqrt(var + LN_EPS) * g + b


def _ada_kernel(c_ref, w_ref, b_ref, o_ref):
    c = c_ref[...]
    s = c * jax.nn.sigmoid(c)
    o_ref[...] = _dot3(s, w_ref[...]) + b_ref[...]


def ada_modulation(c8, w_ada, b_ada, tn=1024):
    L, D, N = w_ada.shape
    return pl.pallas_call(
        _ada_kernel,
        grid=(L, N // tn),
        in_specs=[
            pl.BlockSpec((8, D), lambda l, j: (0, 0)),
            pl.BlockSpec((None, D, tn), lambda l, j: (l, 0, j)),
            pl.BlockSpec((None, 1, tn), lambda l, j: (l, 0, j)),
        ],
        out_specs=pl.BlockSpec((None, 8, tn), lambda l, j: (l, 0, j)),
        out_shape=jax.ShapeDtypeStruct((L, 8, N), F32),
        compiler_params=_cparams(("arbitrary", "arbitrary")),
        name="ada",
    )(c8, w_ada, b_ada.reshape(L, 1, N))


def _inproj_kernel(x_ref, sc_ref, sh_ref, w_ref, o_ref, u_ref):
    @pl.when(pl.program_id(1) == 0)
    def _():
        u_ref[...] = (x_ref[...] * (1.0 + sc_ref[...]) + sh_ref[...]).astype(BF16)

    o_ref[...] = _dot(u_ref[...], w_ref[...]).astype(o_ref.dtype)


def in_projection(x2d, sc, sh, w_bf16, layer, seq, tm=1024, tn=1536):
    M, D = x2d.shape
    N = w_bf16.shape[2]
    tm = min(tm, seq)
    tn = min(tn, N)
    bpb = seq // tm
    return pl.pallas_call(
        _inproj_kernel,
        grid=(M // tm, N // tn),
        in_specs=[
            pl.BlockSpec((tm, D), lambda i, j: (i, 0)),
            pl.BlockSpec((None, 1, D), lambda i, j: (i // bpb, 0, 0)),
            pl.BlockSpec((None, 1, D), lambda i, j: (i // bpb, 0, 0)),
            pl.BlockSpec((None, D, tn), lambda i, j: (layer, 0, j)),
        ],
        out_specs=pl.BlockSpec((tm, tn), lambda i, j: (i, j)),
        out_shape=jax.ShapeDtypeStruct((M, N), BF16),
        scratch_shapes=[pltpu.VMEM((tm, D), BF16)],
        compiler_params=_cparams(("arbitrary", "arbitrary")),
        name="inproj",
    )(x2d, sc, sh, w_bf16)


def _t5_bucket_np(rel):
    n = np.maximum(rel, 0)
    max_exact = NUM_BUCKETS // 2
    ratio = np.maximum(n, 1).astype(np.float32) / np.float32(max_exact)
    large = max_exact + (np.log(ratio).astype(np.float32) / np.float32(math.log(MAX_DISTANCE / max_exact))
                         * np.float32(NUM_BUCKETS - max_exact)).astype(np.int32)
    large = np.minimum(large, NUM_BUCKETS - 1)
    return np.where(n < max_exact, n, large)


def _bucket_thresholds():
    buckets = _t5_bucket_np(np.arange(2 * MAX_DISTANCE))
    assert np.all(np.diff(buckets) >= 0) and buckets[MAX_DISTANCE] == NUM_BUCKETS - 1
    return [int(np.argmax(buckets >= k)) for k in range(NUM_BUCKETS)]


def _relbias_kernel(tab_ref, o_ref, *, t, n_heads, thresholds):
    h = pl.program_id(0)
    row = lax.broadcasted_iota(jnp.int32, (t, t), 0)
    col = lax.broadcasted_iota(jnp.int32, (t, t), 1)
    for blk in range(2):
        rel = row - col + blk * t
        for mp in range(2):
            entry = lambda k: tab_ref[(k * n_heads + h) * 2 + mp]
            far = entry(NUM_BUCKETS - 1)
            val = jnp.full((t, t), entry(0) - far, F32)
            for k in range(1, NUM_BUCKETS):
                val = jnp.where(rel >= thresholds[k], entry(k) - far, val)
            if blk == 0:
                val = jnp.where(rel < 0, MASK_VALUE, val)
            o_ref[blk, mp * t:(mp + 1) * t, :] = val


def rel_bias_tiles(rel_bias, t):
    assert t >= MAX_DISTANCE
    n_heads = rel_bias.shape[1]
    kern = functools.partial(_relbias_kernel, t=t, n_heads=n_heads, thresholds=_bucket_thresholds())
    grid_spec = pltpu.PrefetchScalarGridSpec(
        num_scalar_prefetch=1,
        grid=(n_heads,),
        in_specs=[],
        out_specs=pl.BlockSpec((None, 2, 2 * t, t), lambda h, tab: (h, 0, 0, 0)),
    )
    return pl.pallas_call(
        kern,
        grid_spec=grid_spec,
        out_shape=jax.ShapeDtypeStruct((n_heads, 2, 2 * t, t), F32),
        compiler_params=_cparams(("arbitrary",)),
        name="relbias",
    )(rel_bias.reshape(-1).astype(F32))


def _diff_kernel(lam_ref, q_ref, k_ref, v_ref, bias_ref, g_ref, o_ref, qs_ref, qn_ref, s_ref, m_ref, acc_ref,
                 *, t, rc, nq, out_scale):
    ones = jnp.ones((t, HEAD_DIM), BF16)

    def keys(j):
        return pl.ds(pl.multiple_of(j * t, t), t)

    def stack_queries(qi, dst_ref):
        q = q_ref[keys(qi), :] * BF16(DIFF_QK_DIM ** -0.5)
        lane = lax.broadcasted_iota(jnp.int32, q.shape, 1)
        zero = jnp.zeros_like(q)
        dst_ref[0:t, :] = jnp.where(lane < DIFF_QK_DIM, q, zero)
        dst_ref[t:2 * t, :] = jnp.where(lane >= DIFF_QK_DIM, q, zero)

    def step(j, bias_blk, next_queries_ref, next_j):
        vb = v_ref[keys(j), :]
        vb1 = jnp.concatenate([vb, ones], axis=1)
        kb_next = k_ref[keys(next_j), :]
        diagonal = bias_blk == 0
        bias_reach = rc + MAX_DISTANCE
        for r in range(2 * t // rc):
            rows = pl.ds(r * rc, rc)
            nc = (r % (t // rc) + 1) * rc if diagonal else t
            s = s_ref[rows, :nc]
            s_ref[rows, :] = _dot_nt(next_queries_ref[rows, :], kb_next)
            if bias_blk is not None:
                c = r % (t // rc)
                if diagonal or c == 0:
                    b0 = max(nc - bias_reach, 0)
                    biased = s[:, b0:] + bias_ref[bias_blk, rows, b0:nc]
                    s = biased if b0 == 0 else jnp.concatenate([s[:, :b0], biased], axis=1)
            m_old = m_ref[rows, :]
            m_new = jnp.maximum(m_old, jnp.max(s, axis=1, keepdims=True))
            p = jnp.exp(s - jnp.concatenate([m_new] * (nc // LANES), axis=1)).astype(BF16)
            alpha = jnp.exp(m_old - m_new)
            acc_ref[rows, :] = jnp.concatenate([alpha, alpha], axis=1) * acc_ref[rows, :] + _dot(p, vb1[:nc])
            m_ref[rows, :] = m_new

    stack_queries(0, qs_ref)
    kb0 = k_ref[keys(0), :]
    for r in range(2 * t // rc):
        rows = pl.ds(r * rc, rc)
        s_ref[rows, :] = _dot_nt(qs_ref[rows, :], kb0)

    def reset():
        m_ref[...] = jnp.full(m_ref.shape, MASK_VALUE, F32)
        acc_ref[...] = jnp.zeros(acc_ref.shape, F32)

    def diagonal_step_and_output(qi):
        stack_queries(jnp.minimum(qi + 1, nq - 1), qn_ref)
        step(qi, 0, qn_ref, 0)
        acc = acc_ref[...]
        ratio = acc[:, :HEAD_DIM] / acc[:, HEAD_DIM:]
        o = ratio[:t] - lam_ref[0] * ratio[t:]
        o = o * lax.rsqrt(jnp.mean(o * o, axis=-1, keepdims=True) + SUBLN_EPS)
        o_ref[keys(qi), :] = (o * g_ref[...] * out_scale).astype(o_ref.dtype)
        qs_ref[...] = qn_ref[...]

    reset()
    diagonal_step_and_output(0)

    def tile_body(qi, carry):
        reset()
        n_far = qi - 1

        def far_pair(p, c):
            step(2 * p, None, qs_ref, 2 * p + 1)
            step(2 * p + 1, None, qs_ref, 2 * p + 2)
            return c

        lax.fori_loop(0, n_far // 2, far_pair, 0)

        @pl.when(n_far % 2 == 1)
        def _():
            step(n_far - 1, None, qs_ref, n_far)

        step(qi - 1, 1, qs_ref, qi)
        diagonal_step_and_output(qi)
        return carry

    lax.fori_loop(1, nq, tile_body, 0)


def diff_attention(proj, lam, bias_tiles, subln_g, batch, seq, lam_init, t):
    M = proj.shape[0]
    H = N_DIFF_HEADS
    nq = seq // t
    rc = min(DIFF_ROW_CHUNK, t)
    assert rc % LANES == 0 and rc >= MAX_DISTANCE and MAX_DISTANCE % LANES == 0
    kern = functools.partial(_diff_kernel, t=t, rc=rc, nq=nq, out_scale=1.0 - lam_init)
    grid_spec = pltpu.PrefetchScalarGridSpec(
        num_scalar_prefetch=1,
        grid=(H, batch),
        in_specs=[
            pl.BlockSpec((seq, HEAD_DIM), lambda h, b, s: (b, h)),
            pl.BlockSpec((seq, HEAD_DIM), lambda h, b, s: (b, H + h)),
            pl.BlockSpec((seq, HEAD_DIM), lambda h, b, s: (b, 2 * H + h)),
            pl.BlockSpec((None, 2, 2 * t, t), lambda h, b, s: (h, 0, 0, 0)),
            pl.BlockSpec((1, HEAD_DIM), lambda h, b, s: (0, 0)),
        ],
        out_specs=pl.BlockSpec((seq, HEAD_DIM), lambda h, b, s: (b, h)),
        scratch_shapes=[
            pltpu.VMEM((2 * t, HEAD_DIM), BF16),
            pltpu.VMEM((2 * t, HEAD_DIM), BF16),
            pltpu.VMEM((2 * t, t), F32),
            pltpu.VMEM((2 * t, LANES), F32),
            pltpu.VMEM((2 * t, 2 * HEAD_DIM), F32),
        ],
    )
    return pl.pallas_call(
        kern,
        grid_spec=grid_spec,
        out_shape=jax.ShapeDtypeStruct((M, H * HEAD_DIM), BF16),
        compiler_params=_cparams(("arbitrary", "arbitrary")),
        name="diffattn",
    )(lam.reshape(1).astype(F32), proj, proj, proj, bias_tiles, subln_g.reshape(1, HEAD_DIM))


def _sb_kernel(q_ref, k_ref, v_ref, o_ref, carry_ref, acc_ref, *, t, hp, scale2):
    qi = pl.program_id(2)

    def keys(j):
        return pl.ds(pl.multiple_of(j * t, t), t)

    def upper_ones(n):
        row = lax.broadcasted_iota(jnp.int32, (n, n), 0)
        col = lax.broadcasted_iota(jnp.int32, (n, n), 1)
        return jnp.where(row > col, 1.0, 0.0).astype(BF16)

    def sticks(z2, valid):
        lp = jnp.log2(1.0 + jnp.exp2(-jnp.abs(z2)))
        drop = jnp.maximum(z2, 0.0) + lp
        log2_beta = z2 - drop
        if valid is not None:
            drop = jnp.where(valid, drop, 0.0)
        return drop, log2_beta

    prev = jnp.maximum(qi - 1, 0)
    qrow = lax.broadcasted_iota(jnp.int32, (t, 2 * t), 0)
    kcol = lax.broadcasted_iota(jnp.int32, (t, 2 * t), 1)
    n_prev = jnp.where(qi > 0, t, 0)
    valid = (kcol < n_prev) | ((kcol >= t) & (kcol - t < qrow))
    upper = upper_ones(t)
    for hh in range(hp):
        cols = slice(hh * HEAD_DIM, (hh + 1) * HEAD_DIM)
        kw = jnp.concatenate([k_ref[keys(prev), cols], k_ref[keys(qi), cols]], axis=0)
        vw = jnp.concatenate([v_ref[keys(prev), cols], v_ref[keys(qi), cols]], axis=0)
        drop, log2_beta = sticks(_dot_nt(q_ref[:, cols], kw) * scale2, valid)
        hi, lo = _split(drop)
        stacked = jnp.concatenate([hi[:, :t], lo[:, :t], hi[:, t:], lo[:, t:]], axis=0)
        within = _dot(stacked, upper)
        diag_total = jnp.sum(drop[:, t:], axis=1, keepdims=True)
        later = jnp.concatenate([within[0:t] + within[t:2 * t] + diag_total,
                                 within[2 * t:3 * t] + within[3 * t:4 * t]], axis=1)
        a = jnp.where(valid, jnp.exp2(log2_beta - later), 0.0)
        acc_ref[:, cols] = _dot(a.astype(BF16), vw)
        prev_total = jnp.sum(drop[:, :t], axis=1, keepdims=True)
        carry_ref[hh] = jnp.broadcast_to(prev_total + diag_total, (t, LANES))

    def step(j):
        for hh in range(hp):
            cols = slice(hh * HEAD_DIM, (hh + 1) * HEAD_DIM)
            drop, log2_beta = sticks(_dot_nt(q_ref[:, cols], k_ref[keys(j), cols]) * scale2, None)
            hi, lo = _split(drop)
            later = _dot(hi, upper) + _dot(lo, upper)
            carry = carry_ref[hh]
            a = jnp.exp2(log2_beta - later - jnp.concatenate([carry] * (t // LANES), axis=1))
            acc_ref[:, cols] += _dot(a.astype(BF16), v_ref[keys(j), cols])
            carry_ref[hh] = carry + jnp.sum(drop, axis=1, keepdims=True)

    def cond(state):
        n, live = state
        return jnp.logical_and(n < qi - 1, live < SB_EXIT_LOG2)

    def body(state):
        n, _ = state
        step(qi - 2 - n)
        return n + 1, jnp.min(carry_ref[...])

    lax.while_loop(cond, body, (jnp.int32(0), jnp.min(carry_ref[...])))
    o_ref[...] = acc_ref[...].astype(o_ref.dtype)


def sb_attention(proj, batch, seq, t, hp=SB_HEADS_PER_STEP):
    M = proj.shape[0]
    H = N_SB_HEADS
    base = 3 * N_DIFF_HEADS // hp
    nq = seq // t
    w = hp * HEAD_DIM
    kern = functools.partial(_sb_kernel, t=t, hp=hp, scale2=HEAD_DIM ** -0.5 * LOG2E)
    return pl.pallas_call(
        kern,
        grid=(H // hp, batch, nq),
        in_specs=[
            pl.BlockSpec((t, w), lambda h, b, qi: (b * nq + qi, base + h)),
            pl.BlockSpec((seq, w), lambda h, b, qi: (b, base + H // hp + h)),
            pl.BlockSpec((seq, w), lambda h, b, qi: (b, base + 2 * (H // hp) + h)),
        ],
        out_specs=pl.BlockSpec((t, w), lambda h, b, qi: (b * nq + qi, h)),
        out_shape=jax.ShapeDtypeStruct((M, H * HEAD_DIM), BF16),
        scratch_shapes=[pltpu.VMEM((hp, t, LANES), F32), pltpu.VMEM((t, w), F32)],
        compiler_params=_cparams(("arbitrary", "arbitrary", "arbitrary")),
        name="sbattn",
    )(proj, proj, proj)


ROUTE_E1, ROUTE_E2, ROUTE_W1, ROUTE_W2, ROUTE_R1, ROUTE_R2 = range(6)


def _mix_kernel(a_ref, b_ref, wo_ref, x_ref, g1_ref, sc2_ref, sh2_ref, lng_ref, lnb_ref, wrt_ref, brt_ref,
                x1_ref, u2_ref, route_ref, route_t_ref, cnt_ref, carry_ref, wsplit_ref, wob_ref, *, alpha, half):
    i = pl.program_id(0)

    @pl.when(i == 0)
    def _():
        carry_ref[...] = jnp.zeros(carry_ref.shape, F32)
        w_hi, w_lo = _split(wrt_ref[...])
        wsplit_ref[...] = jnp.concatenate([w_hi, w_lo], axis=1)
        wob_ref[...] = wo_ref[...].astype(BF16)

    mix = _dot(a_ref[...], wob_ref[pl.ds(0, half), :]) + _dot(b_ref[...], wob_ref[pl.ds(half, half), :])
    h = alpha * x_ref[...] + (1.0 + g1_ref[...]) * mix
    x1 = _layer_norm(h, lng_ref[...], lnb_ref[...])
    x1_ref[...] = x1
    u2 = x1 * (1.0 + sc2_ref[...]) + sh2_ref[...]
    nseg = u2.shape[1] // (2 * LANES)
    for s in range(nseg):
        lo = u2[:, 2 * s * LANES:(2 * s + 1) * LANES].astype(BF16).astype(F32)
        hi = u2[:, (2 * s + 1) * LANES:(2 * s + 2) * LANES].astype(BF16).astype(F32)
        lo_bits = lax.shift_right_logical(lax.bitcast_convert_type(lo, jnp.uint32), jnp.uint32(16))
        hi_bits = lax.bitcast_convert_type(hi, jnp.uint32) & jnp.uint32(HI_HALF)
        u2_ref[pl.ds(s, u2.shape[0], stride=nseg), :] = lo_bits | hi_bits

    u_hi, u_lo = _split(u2)
    hh_hl = _dot(u_hi, wsplit_ref[...])
    lh = _dot(u_lo, wsplit_ref[:, :LANES])
    logits = hh_hl[:, :LANES] + (hh_hl[:, LANES:] + lh) + brt_ref[...]
    tm = logits.shape[0]
    lane = lax.broadcasted_iota(jnp.int32, (tm, LANES), 1).astype(F32)
    gmask = lane < N_GROUPS
    gl = jnp.where(gmask, logits, MASK_VALUE)
    gmax = jnp.max(gl, axis=1, keepdims=True)
    gidx = jnp.min(jnp.where(gmask & (gl == gmax), lane, float(LANES)), axis=1, keepdims=True)
    gsum = jnp.sum(jnp.where(gmask, jnp.exp(gl - gmax), 0.0), axis=1, keepdims=True)
    gp = 1.0 / gsum

    lo = N_GROUPS + gidx * EXPERTS_PER_GROUP
    emask = (lane >= lo) & (lane < lo + EXPERTS_PER_GROUP)
    el = jnp.where(emask, logits, MASK_VALUE)
    v1 = jnp.max(el, axis=1, keepdims=True)
    i1 = jnp.min(jnp.where(emask & (el == v1), lane, float(LANES)), axis=1, keepdims=True)
    emask2 = emask & (lane != i1)
    el2 = jnp.where(emask2, logits, MASK_VALUE)
    v2 = jnp.max(el2, axis=1, keepdims=True)
    i2 = jnp.min(jnp.where(emask2 & (el2 == v2), lane, float(LANES)), axis=1, keepdims=True)
    tt = jnp.exp(v2 - v1)
    w1 = gp / (1.0 + tt)
    w2 = gp * tt / (1.0 + tt)
    e1 = i1 - N_GROUPS
    e2 = i2 - N_GROUPS

    oh1 = lane == e1
    oh2 = lane == e2
    oh1b = jnp.where(oh1, 1.0, 0.0).astype(BF16)
    oh2b = jnp.where(oh2, 1.0, 0.0).astype(BF16)
    row = lax.broadcasted_iota(jnp.int32, (tm, tm), 0)
    col = lax.broadcasted_iota(jnp.int32, (tm, tm), 1)
    lower = jnp.where(col < row, 1.0, 0.0).astype(BF16)
    c1 = jnp.sum(jnp.where(oh1, 1.0, 0.0), axis=0, keepdims=True)
    c2 = jnp.sum(jnp.where(oh2, 1.0, 0.0), axis=0, keepdims=True)
    carry = carry_ref[...]
    r1 = _dot(lower, oh1b) + carry
    r2 = _dot(lower, oh2b) + (carry + c1)
    rank1 = jnp.sum(jnp.where(oh1, r1, 0.0), axis=1, keepdims=True)
    rank2 = jnp.sum(jnp.where(oh2, r2, 0.0), axis=1, keepdims=True)
    carry = carry + c1 + c2
    carry_ref[...] = carry
    cnt_ref[...] = jnp.broadcast_to(carry, cnt_ref.shape)

    out = jnp.zeros((tm, LANES), F32)
    for ln, val in ((ROUTE_E1, e1), (ROUTE_E2, e2), (ROUTE_W1, w1), (ROUTE_W2, w2),
                    (ROUTE_R1, rank1), (ROUTE_R2, rank2)):
        out = jnp.where(lane == float(ln), val, out)
    route_ref[...] = out
    route_t_ref[...] = out.T[:8]


def mix_ln_route(a_out, b_out, w_o, layer, x2d, g1, sc2, sh2, ln_g, ln_b, w_rt, b_rt, seq, alpha, tm=512):
    M, D = x2d.shape
    half = a_out.shape[1]
    bpb = seq // tm
    nseg = D // (2 * LANES)
    row = lambda i: (i // bpb, 0, 0)
    full = lambda i: (0, 0)
    kern = functools.partial(_mix_kernel, alpha=alpha, half=half)
    return pl.pallas_call(
        kern,
        grid=(M // tm,),
        in_specs=[
            pl.BlockSpec((tm, half), lambda i: (i, 0)),
            pl.BlockSpec((tm, half), lambda i: (i, 0)),
            pl.BlockSpec((None, 2 * half, D), lambda i: (layer, 0, 0), pipeline_mode=pl.Buffered(1)),
            pl.BlockSpec((tm, D), lambda i: (i, 0)),
            pl.BlockSpec((None, 1, D), row),
            pl.BlockSpec((None, 1, D), row),
            pl.BlockSpec((None, 1, D), row),
            pl.BlockSpec((1, D), full),
            pl.BlockSpec((1, D), full),
            pl.BlockSpec((D, LANES), full),
            pl.BlockSpec((1, LANES), full),
        ],
        out_specs=[
            pl.BlockSpec((tm, D), lambda i: (i, 0)),
            pl.BlockSpec((tm * nseg, LANES), lambda i: (i, 0)),
            pl.BlockSpec((tm, LANES), lambda i: (i, 0)),
            pl.BlockSpec((8, tm), lambda i: (0, i)),
            pl.BlockSpec((8, LANES), full),
        ],
        out_shape=[
            jax.ShapeDtypeStruct((M, D), F32),
            jax.ShapeDtypeStruct((M * nseg, LANES), jnp.uint32),
            jax.ShapeDtypeStruct((M, LANES), F32),
            jax.ShapeDtypeStruct((8, M), F32),
            jax.ShapeDtypeStruct((8, LANES), F32),
        ],
        scratch_shapes=[pltpu.VMEM((1, LANES), F32), pltpu.VMEM((D, 2 * LANES), BF16),
                        pltpu.VMEM((2 * half, D), BF16)],
        compiler_params=_cparams(("arbitrary",)),
        name="mix",
    )(a_out, b_out, w_o, x2d, g1, sc2, sh2, ln_g.reshape(1, D), ln_b.reshape(1, D), w_rt, b_rt)


def _item_copy(src_hbm, dst_vmem, src_item, dst_item, rpi, sem):
    src = src_hbm.at[pl.ds(pl.multiple_of(src_item * rpi, rpi), rpi), :]
    dst = dst_vmem.at[pl.ds(pl.multiple_of(dst_item * rpi, rpi), rpi), :]
    return pltpu.make_async_copy(src, dst, sem)


def _start_items(idx_ref, base, n_items, rpi, src_hbm, dst_vmem, sem, unroll=False, both_queues=False):
    if unroll and both_queues:
        for r in range(n_items):
            _item_copy(src_hbm, dst_vmem, idx_ref[base + r], r, rpi, sem).start(priority=r % 2)
        return

    def issue(r, c):
        _item_copy(src_hbm, dst_vmem, idx_ref[base + r], r, rpi, sem).start()
        return c

    lax.fori_loop(0, n_items, issue, 0, unroll=unroll)


def _wait_items(n_items, rpi, src_hbm, dst_vmem, sem):
    pltpu.make_async_copy(src_hbm.at[pl.ds(0, n_items * rpi), :], dst_vmem, sem).wait()


def _expert_kernel(tok_ref, first_ref, u_hbm, wg_hbm, wu_hbm, wd_hbm, y_hbm,
                   xbuf_ref, xs_ref, obuf_ref, stage_gu_ref, stage_d_ref, wg_ref, wu_ref, wd_ref,
                   gsems, osems, wsems, *, layer, tb, nseg, nrow, n_tiles):
    e = pl.program_id(0)
    n_exp = pl.num_programs(0)
    n_used = first_ref[n_exp]
    dc = wg_ref.shape[0] // WEIGHT_CHUNKS
    fc = wd_ref.shape[0] // WEIGHT_CHUNKS

    def weight_copies(expert, c):
        drows = pl.ds(c * dc, dc)
        frows = pl.ds(c * fc, fc)
        return (pltpu.make_async_copy(wg_hbm.at[layer, expert, drows, :], stage_gu_ref.at[c, 0], wsems.at[c]),
                pltpu.make_async_copy(wu_hbm.at[layer, expert, drows, :], stage_gu_ref.at[c, 1], wsems.at[c]),
                pltpu.make_async_copy(wd_hbm.at[layer, expert, frows, :], stage_d_ref.at[c], wsems.at[c]))

    def start_gather(k, unroll):
        s = k % GATHER_SLOTS
        tile = jnp.minimum(k, n_used - 1)
        _start_items(tok_ref, tile * tb, tb, nseg, u_hbm, xbuf_ref.at[s], gsems.at[s], unroll)

    def wait_gather(k):
        s = k % GATHER_SLOTS
        _wait_items(tb, nseg, u_hbm, xbuf_ref.at[s], gsems.at[s])

    def out_copy(tile, s):
        rows = pl.ds(pl.multiple_of(tile * (tb * nrow), tb * nrow), tb * nrow)
        return pltpu.make_async_copy(obuf_ref.at[s], y_hbm.at[rows, :], osems.at[s])

    @pl.when(e == 0)
    def _():
        for c in range(WEIGHT_CHUNKS):
            for cp in weight_copies(0, c):
                cp.start(priority=WEIGHT_DMA_PRIORITY)
        for k in range(GATHER_SLOTS - 1):
            start_gather(k, False)

    for c in range(WEIGHT_CHUNKS):
        for cp in weight_copies(e, c):
            cp.wait()
        wg_ref[c * dc:(c + 1) * dc, :] = stage_gu_ref[c, 0].astype(BF16)
        wu_ref[c * dc:(c + 1) * dc, :] = stage_gu_ref[c, 1].astype(BF16)
        wd_ref[c * fc:(c + 1) * fc, :] = stage_d_ref[c].astype(BF16)

        @pl.when(e + 1 < n_exp)
        def _():
            for cp in weight_copies(e + 1, c):
                cp.start(priority=WEIGHT_DMA_PRIORITY)

    def tile_body(i, carry):
        slot = i % GATHER_SLOTS
        oslot = i % 2

        @pl.when(i >= 2)
        def _():
            out_copy(i - 2, oslot).wait()

        wait_gather(i)
        start_gather(i + GATHER_SLOTS - 1, True)
        for s in range(nseg):
            w = xbuf_ref[slot, pl.ds(s, tb, stride=nseg), :]
            lo = lax.bitcast_convert_type(lax.shift_left(w, jnp.uint32(16)), F32)
            hi = lax.bitcast_convert_type(w & jnp.uint32(HI_HALF), F32)
            xs_ref[:, 2 * s * LANES:(2 * s + 1) * LANES] = lo.astype(BF16)
            xs_ref[:, (2 * s + 1) * LANES:(2 * s + 2) * LANES] = hi.astype(BF16)
        x = xs_ref[...]
        g = _dot(x, wg_ref[...])
        u = _dot(x, wu_ref[...])
        h = (g * jax.nn.sigmoid(g) * u).astype(BF16)
        y = _dot(h, wd_ref[...])
        for s in range(nrow):
            obuf_ref[oslot, pl.ds(s, tb, stride=nrow), :] = y[:, s * OUT_ITEM_WIDTH:(s + 1) * OUT_ITEM_WIDTH]
        out_copy(i, oslot).start()
        return carry

    lax.fori_loop(first_ref[e], first_ref[e + 1], tile_body, 0)

    @pl.when(e == n_exp - 1)
    def _():
        for k in range(GATHER_SLOTS - 1):
            wait_gather(n_used + k)
        for back in (1, 2):

            @pl.when(n_used >= back)
            def _():
                out_copy(n_used - back, (n_used - back) % 2).wait()

        obuf_ref[0] = jnp.zeros(obuf_ref.shape[1:], obuf_ref.dtype)

        def zero_start(i, carry):
            out_copy(i, 0).start()
            return carry

        def zero_wait(i, carry):
            out_copy(i, 0).wait()
            return carry

        lax.fori_loop(n_used, n_tiles, zero_start, 0)
        lax.fori_loop(n_used, n_tiles, zero_wait, 0)


def expert_mlp(pad_tok, first_tile, u2p, wg, wu, wd, layer, tb):
    P = pad_tok.shape[0]
    E, D, F = wg.shape[1:]
    nseg = D // (2 * LANES)
    nrow = D // OUT_ITEM_WIDTH
    fc = F // WEIGHT_CHUNKS
    any_spec = pl.BlockSpec(memory_space=pl.ANY)
    grid_spec = pltpu.PrefetchScalarGridSpec(
        num_scalar_prefetch=2,
        grid=(E,),
        in_specs=[any_spec, any_spec, any_spec, any_spec],
        out_specs=any_spec,
        scratch_shapes=[
            pltpu.VMEM((GATHER_SLOTS, tb * nseg, LANES), jnp.uint32),
            pltpu.VMEM((tb, D), BF16),
            pltpu.VMEM((2, tb * nrow, OUT_ITEM_WIDTH), F32),
            pltpu.VMEM((WEIGHT_CHUNKS, 2, D // WEIGHT_CHUNKS, F), F32),
            pltpu.VMEM((WEIGHT_CHUNKS, fc, D), F32),
            pltpu.VMEM((D, F), BF16),
            pltpu.VMEM((D, F), BF16),
            pltpu.VMEM((F, D), BF16),
            pltpu.SemaphoreType.DMA((GATHER_SLOTS,)),
            pltpu.SemaphoreType.DMA((2,)),
            pltpu.SemaphoreType.DMA((WEIGHT_CHUNKS,)),
        ],
    )
    return pl.pallas_call(
        functools.partial(_expert_kernel, layer=layer, tb=tb, nseg=nseg, nrow=nrow, n_tiles=P // tb),
        grid_spec=grid_spec,
        out_shape=jax.ShapeDtypeStruct((P * nrow, OUT_ITEM_WIDTH), F32),
        compiler_params=_cparams(("arbitrary",)),
        name="expert",
    )(pad_tok, first_tile, u2p, wg, wu, wd)


def _final_kernel(dest_ref, x1_ref, route_ref, g2_ref, lng_ref, lnb_ref, yb_hbm, o_ref, buf_ref, sems,
                  *, alpha, tm, m, nrow):
    i = pl.program_id(0)
    slot = i % GATHER_SLOTS
    last = pl.num_programs(0) - 1

    def start(n, unroll):
        s = n % GATHER_SLOTS
        tile = jnp.minimum(n, last)
        for k in range(TOP_K):
            _start_items(dest_ref, k * m + tile * tm, tm, nrow, yb_hbm, buf_ref.at[s, k], sems.at[s], unroll,
                         both_queues=True)

    def wait(n):
        s = n % GATHER_SLOTS
        for k in range(TOP_K):
            _wait_items(tm, nrow, yb_hbm, buf_ref.at[s, k], sems.at[s])

    @pl.when(i == 0)
    def _():
        for n in range(GATHER_SLOTS - 1):
            start(n, False)

    wait(i)
    start(i + GATHER_SLOTS - 1, True)

    def expert_out(k):
        chunks = [buf_ref[slot, k, pl.ds(s, tm, stride=nrow), :] for s in range(nrow)]
        return jnp.concatenate(chunks, axis=1)

    route = route_ref[...]
    w1 = route[:, ROUTE_W1:ROUTE_W1 + 1]
    w2 = route[:, ROUTE_W2:ROUTE_W2 + 1]
    y = w1 * expert_out(0) + w2 * expert_out(1)
    h = alpha * x1_ref[...] + (1.0 + g2_ref[...]) * y
    o_ref[...] = _layer_norm(h, lng_ref[...], lnb_ref[...])

    @pl.when(i == last)
    def _():
        for n in range(1, GATHER_SLOTS):
            wait(i + n)


def combine_ln(dest_flat, x1, route, g2, ln_g, ln_b, yb, seq, alpha, tm=256):
    M, D = x1.shape
    bpb = seq // tm
    nrow = D // OUT_ITEM_WIDTH
    grid_spec = pltpu.PrefetchScalarGridSpec(
        num_scalar_prefetch=1,
        grid=(M // tm,),
        in_specs=[
            pl.BlockSpec((tm, D), lambda i, d: (i, 0)),
            pl.BlockSpec((tm, LANES), lambda i, d: (i, 0)),
            pl.BlockSpec((None, 1, D), lambda i, d: (i // bpb, 0, 0)),
            pl.BlockSpec((1, D), lambda i, d: (0, 0)),
            pl.BlockSpec((1, D), lambda i, d: (0, 0)),
            pl.BlockSpec(memory_space=pl.ANY),
        ],
        out_specs=pl.BlockSpec((tm, D), lambda i, d: (i, 0)),
        scratch_shapes=[pltpu.VMEM((GATHER_SLOTS, TOP_K, tm * nrow, OUT_ITEM_WIDTH), F32),
                        pltpu.SemaphoreType.DMA((GATHER_SLOTS,))],
    )
    return pl.pallas_call(
        functools.partial(_final_kernel, alpha=alpha, tm=tm, m=M, nrow=nrow),
        grid_spec=grid_spec,
        out_shape=jax.ShapeDtypeStruct((M, D), F32),
        compiler_params=_cparams(("arbitrary",)),
        name="final",
    )(dest_flat, x1, route, g2, ln_g.reshape(1, D), ln_b.reshape(1, D), yb)


def moe_layout(route_t, cnt, tb):
    M = route_t.shape[1]
    n_assign = M * TOP_K
    counts = cnt[0, :N_EXPERTS].astype(jnp.int32)
    padded = ((counts + tb - 1) // tb) * tb
    pends = jnp.cumsum(padded)
    pstarts = pends - padded
    n_tiles = (n_assign + N_EXPERTS * (tb - 1) + tb - 1) // tb
    e = route_t[ROUTE_E1:ROUTE_E2 + 1].astype(jnp.int32)
    rank = route_t[ROUTE_R1:ROUTE_R2 + 1].astype(jnp.int32)
    before = jnp.arange(N_EXPERTS, dtype=jnp.int32)[:, None, None] < e[None]
    dest = jnp.sum(jnp.where(before, padded[:, None, None], 0), axis=0) + rank
    dest_flat = dest.reshape(-1)
    tok = jnp.tile(jnp.arange(M, dtype=jnp.int32), TOP_K)
    pad_tok = jnp.zeros((n_tiles * tb,), jnp.int32).at[dest_flat].set(tok)
    first_tile = (jnp.concatenate([pstarts, pends[-1:]]) // tb).astype(jnp.int32)
    return dest_flat, pad_tok, first_tile


def _forward(x, c, w_ada, b_ada, w_in, diff_lambda, diff_subln_g, rel_bias, w_o, ln_g, ln_b, w_group, b_group,
             w_router, b_router, w_gate, w_up, w_down, *, diff_tile, sb_tile, moe_tile, ada_tn):
    B, S, D = x.shape
    depth = w_ada.shape[0]
    M = B * S
    alpha = (2 * depth) ** 0.25

    c8 = jnp.zeros((8, D), F32).at[:B].set(c)
    mod = ada_modulation(c8, w_ada, b_ada, ada_tn)[:, :B]
    mod = mod.reshape(depth, B, 6, 1, D)
    bias_tiles = rel_bias_tiles(rel_bias, diff_tile)
    w_in_b = w_in.astype(BF16)

    x2d = x.reshape(M, D)
    for l in range(depth):
        sh1, sc1, g1, sh2, sc2, g2 = (mod[l, :, n] for n in range(6))
        proj = in_projection(x2d, sc1, sh1, w_in_b, l, S)

        lam_init = 0.8 - 0.6 * math.exp(-0.3 * l)
        lp = diff_lambda[l].astype(F32)
        lam = jnp.exp(jnp.sum(lp[0] * lp[1])) - jnp.exp(jnp.sum(lp[2] * lp[3])) + lam_init
        a_out = diff_attention(proj, lam, bias_tiles, diff_subln_g[l], B, S, lam_init, diff_tile)
        b_out = sb_attention(proj, B, S, sb_tile)

        w_rt = jnp.zeros((D, LANES), F32).at[:, :N_GROUPS].set(w_group[l])
        w_rt = w_rt.at[:, N_GROUPS:N_GROUPS + N_EXPERTS].set(w_router[l])
        b_rt = jnp.zeros((1, LANES), F32).at[0, :N_GROUPS].set(b_group[l])
        b_rt = b_rt.at[0, N_GROUPS:N_GROUPS + N_EXPERTS].set(b_router[l])
        x1, u2, route, route_t, cnt = mix_ln_route(a_out, b_out, w_o, l, x2d, g1, sc2, sh2,
                                                   ln_g[l, 0], ln_b[l, 0], w_rt, b_rt, S, alpha)

        dest_flat, pad_tok, first_tile = moe_layout(route_t, cnt, moe_tile)
        yb = expert_mlp(pad_tok, first_tile, u2, w_gate, w_up, w_down, l, moe_tile)
        x2d = combine_ln(dest_flat, x1, route, g2, ln_g[l, 1], ln_b[l, 1], yb, S, alpha)
    return x2d.reshape(B, S, D)


def kernel(x, c, w_ada, b_ada, w_in, diff_lambda, diff_subln_g, rel_bias, w_o, ln_g, ln_b, w_group, b_group,
           w_router, b_router, w_gate, w_up, w_down):
    return _forward(x, c, w_ada, b_ada, w_in, diff_lambda, diff_subln_g, rel_bias, w_o, ln_g, ln_b, w_group,
                    b_group, w_router, b_router, w_gate, w_up, w_down,
                    diff_tile=512, sb_tile=256, moe_tile=256, ada_tn=1024)
```
